```python
import math
import jax
import jax.numpy as jnp
from jax import lax
import numpy as np

D_MODEL = 2048
BATCH = 4
SEQ = 2048
DEPTH = 1
DEC_BATCH = 128
DEC_SEQ = 8
PAST_LEN = 16384
PAGE_SIZE = 128

W_A = D_MODEL // 2
GROUP_A = 16
N_GROUPS_A = W_A // GROUP_A
P_STATE = 64
W_B = D_MODEL // 2
N_HEADS_B = 4
DK_B = W_B // 2 // N_HEADS_B
DV_B = W_B // N_HEADS_B
GATE_RANK = 16
GATE_TAU = 16.0
CHUNK = 16
EPS = 1e-6

IN_SPLITS = (W_A, W_A, N_HEADS_B * DK_B, N_HEADS_B * DK_B, W_B, W_B, GATE_RANK, D_MODEL, D_MODEL)
SPLIT_POINTS = tuple(int(s) for s in np.cumsum(IN_SPLITS)[:-1])
IN_COLS = int(sum(IN_SPLITS))

kernel_name = "hybrid_s5_gla_adaln_step"


def rms_norm(x, gain):
    xf = x.astype(jnp.float32)
    y = xf * lax.rsqrt(jnp.mean(xf * xf, axis=-1, keepdims=True) + EPS)
    return (y * gain.astype(jnp.float32)).astype(x.dtype)


def s5_discretise(lambda_re, lambda_im, log_dt):
    f32 = jnp.float32
    dt = jnp.exp(log_dt.astype(f32))[:, None]
    lr, li = lambda_re.astype(f32), lambda_im.astype(f32)
    mag = jnp.exp(lr * dt)
    ab_re, ab_im = mag * jnp.cos(li * dt), mag * jnp.sin(li * dt)
    nr, ni = ab_re - 1.0, ab_im
    den = lr * lr + li * li
    cf_re = (nr * lr + ni * li) / den
    cf_im = (ni * lr - nr * li) / den
    return ab_re, ab_im, cf_re, cf_im


def s5_branch(u, h0_re, h0_im, lambda_re, lambda_im, log_dt, b_re, b_im, c_re, c_im, d_skip, w_glu, b_glu):
    f32 = jnp.float32
    Bn, L, _ = u.shape
    uf = u.astype(f32)
    ab_re, ab_im, cf_re, cf_im = s5_discretise(lambda_re, lambda_im, log_dt)
    ug = uf.reshape(Bn, L, N_GROUPS_A, GROUP_A)
    bu_re = jnp.einsum('blgh,gph->blgp', ug, b_re.astype(f32))
    bu_im = jnp.einsum('blgh,gph->blgp', ug, b_im.astype(f32))
    xr = cf_re * bu_re - cf_im * bu_im
    xi = cf_re * bu_im + cf_im * bu_re
    h0r, h0i = h0_re.astype(f32), h0_im.astype(f32)
    xr = xr.at[:, 0].add(ab_re * h0r - ab_im * h0i)
    xi = xi.at[:, 0].add(ab_re * h0i + ab_im * h0r)
    ar = jnp.broadcast_to(ab_re, xr.shape)
    ai = jnp.broadcast_to(ab_im, xr.shape)

    def combine(e1, e2):
        a1r, a1i, b1r, b1i = e1
        a2r, a2i, b2r, b2i = e2
        return (a2r * a1r - a2i * a1i,
                a2r * a1i + a2i * a1r,
                a2r * b1r - a2i * b1i + b2r,
                a2r * b1i + a2i * b1r + b2i)

    _, _, hr, hi = lax.associative_scan(combine, (ar, ai, xr, xi), axis=1)
    y = (jnp.einsum('blgp,ghp->blgh', hr, c_re.astype(f32))
         - jnp.einsum('blgp,ghp->blgh', hi, c_im.astype(f32))).reshape(Bn, L, W_A)
    y = y + d_skip.astype(f32) * uf
    g = jax.nn.gelu(y)
    out = g * jax.nn.sigmoid(g @ w_glu.astype(f32) + b_glu.astype(f32))
    return out.astype(u.dtype), hr[:, -1], hi[:, -1]


def gla_branch(q, k, v, log_a, s0, head_gain):
    f32 = jnp.float32
    Bn, L = q.shape[0], q.shape[1]
    pad = (-L) % CHUNK

    def prep(t):
        t = jnp.pad(t.astype(f32), ((0, 0), (0, pad), (0, 0), (0, 0)))
        return t.reshape(Bn, (L + pad) // CHUNK, CHUNK, t.shape[2], t.shape[3])

    qc, kc, vc, gc = prep(q), prep(k), prep(v), prep(log_a)
    b = jnp.cumsum(gc, axis=2)
    b_end = b[:, :, -1]
    causal = jnp.tril(jnp.ones((CHUNK, CHUNK), dtype=bool))[None, None, :, :, None, None]
    diff = b[:, :, :, None] - b[:, :, None, :]
    decay = jnp.where(causal, jnp.exp(jnp.where(causal, diff, 0.0)), 0.0)
    scores = jnp.einsum('bntshk,bnthk,bnshk->bntsh', decay, qc, kc)
    o_intra = jnp.einsum('bntsh,bnshv->bnthv', scores, vc)
    q_dec = qc * jnp.exp(b)
    k_dec = kc * jnp.exp(b_end[:, :, None] - b)
    d_end = jnp.exp(b_end)

    def step(S, inp):
        qd, kd, vv, de = inp
        o = jnp.einsum('bchk,bhkv->bchv', qd, S)
        S = de[..., None] * S + jnp.einsum('bchk,bchv->bhkv', kd, vv)
        return S, o

    xs = (jnp.moveaxis(q_dec, 1, 0), jnp.moveaxis(k_dec, 1, 0), jnp.moveaxis(vc, 1, 0), jnp.moveaxis(d_end, 1, 0))
    s_fin, o_inter = lax.scan(step, s0.astype(f32), xs)
    o = (jnp.moveaxis(o_inter, 0, 1) + o_intra).reshape(Bn, -1, N_HEADS_B, DV_B)[:, :L]
    o = o * lax.rsqrt(jnp.mean(o * o, axis=-1, keepdims=True) + EPS)
    o = o.reshape(Bn, L, N_HEADS_B * DV_B) * head_gain.astype(f32)
    return o, s_fin


def hybrid_layer(x, c, s0_re, s0_im, s0_gla, lw):
    (w_ada, b_ada, norm_gain, w_in, lambda_re, lambda_im, log_dt, ssm_b_re, ssm_b_im,
     ssm_c_re, ssm_c_im, d_skip, w_glu, b_glu, w_gate_up, b_gate, gla_norm_gain,
     w_a_out, w_b_out, w_out) = lw
    Bn, L, _ = x.shape
    mod = jax.nn.silu(c) @ w_ada + b_ada
    shift, scale, gate = jnp.split(mod, 3, axis=-1)
    h = rms_norm(x, norm_gain) * (1 + scale[:, None]) + shift[:, None]
    u_a, z_a, q, k, v, z_b, g_lr, m_a, m_b = jnp.split(h @ w_in, SPLIT_POINTS, axis=-1)
    a_y, sa_re, sa_im = s5_branch(u_a, s0_re, s0_im, lambda_re, lambda_im, log_dt, ssm_b_re, ssm_b_im,
                                  ssm_c_re, ssm_c_im, d_skip, w_glu, b_glu)
    a_out = (a_y * jax.nn.silu(z_a)) @ w_a_out
    log_a = jax.nn.log_sigmoid((g_lr @ w_gate_up + b_gate).astype(jnp.float32)) / GATE_TAU
    b_y, s_gla = gla_branch(q.reshape(Bn, L, N_HEADS_B, DK_B) * DK_B ** -0.5,
                            k.reshape(Bn, L, N_HEADS_B, DK_B),
                            v.reshape(Bn, L, N_HEADS_B, DV_B),
                            log_a.reshape(Bn, L, N_HEADS_B, DK_B), s0_gla, gla_norm_gain)
    b_out = (b_y.astype(x.dtype) * jax.nn.silu(z_b)) @ w_b_out
    merged = jax.nn.sigmoid(m_a) * a_out + jax.nn.sigmoid(m_b) * b_out
    y = x + gate[:, None] * (merged @ w_out)
    return y, sa_re, sa_im, s_gla


def setup_inputs(seed: int = 0) -> dict:
    key = jax.random.key(seed)
    ks = jax.random.split(key, 32)
    f32 = jnp.float32

    def nrm(k, shape, scale):
        return jax.random.normal(k, shape, f32) * scale

    G, P, H = N_GROUPS_A, P_STATE, N_HEADS_B
    n = jnp.arange(P, dtype=f32)
    return {
        'x_prompt': nrm(ks[0], (BATCH, SEQ, D_MODEL), 1.0),
        'x_sample': nrm(ks[1], (DEC_BATCH, DEC_SEQ, D_MODEL), 1.0),
        'c_prompt': nrm(ks[2], (BATCH, D_MODEL), 1.0),
        'c_sample': nrm(ks[3], (DEC_BATCH, D_MODEL), 1.0),
        'state_ssm_re': nrm(ks[4], (DEPTH, DEC_BATCH, G, P), 0.5),
        'state_ssm_im': nrm(ks[5], (DEPTH, DEC_BATCH, G, P), 0.5),
        'state_gla': nrm(ks[6], (DEPTH, DEC_BATCH, H, DK_B, DV_B), 1.0),
        'w_ada': nrm(ks[7], (DEPTH, D_MODEL, 3 * D_MODEL), D_MODEL ** -0.5),
        'b_ada': nrm(ks[8], (DEPTH, 3 * D_MODEL), 0.02),
        'norm_gain': 1.0 + nrm(ks[9], (DEPTH, D_MODEL), 0.02),
        'w_in': nrm(ks[10], (DEPTH, D_MODEL, IN_COLS), D_MODEL ** -0.5),
        'lambda_re': -0.5 + nrm(ks[11], (DEPTH, G, P), 0.01),
        'lambda_im': math.pi * n + nrm(ks[12], (DEPTH, G, P), 0.01),
        'log_dt': jax.random.uniform(ks[13], (DEPTH, G), f32, math.log(1e-3), math.log(1e-1)),
        'ssm_b_re': nrm(ks[14], (DEPTH, G, P, GROUP_A), (2 * GROUP_A) ** -0.5),
        'ssm_b_im': nrm(ks[15], (DEPTH, G, P, GROUP_A), (2 * GROUP_A) ** -0.5),
        'ssm_c_re': nrm(ks[16], (DEPTH, G, GROUP_A, P), (2 * P) ** -0.5),
        'ssm_c_im': nrm(ks[17], (DEPTH, G, GROUP_A, P), (2 * P) ** -0.5),
        'd_skip': nrm(ks[18], (DEPTH, W_A), 1.0),
        'w_glu': nrm(ks[19], (DEPTH, W_A, W_A), W_A ** -0.5),
        'b_glu': nrm(ks[20], (DEPTH, W_A), 0.02),
        'w_gate_up': nrm(ks[21], (DEPTH, GATE_RANK, H * DK_B), GATE_RANK ** -0.5),
        'b_gate': nrm(ks[22], (DEPTH, H * DK_B), 0.1),
        'gla_norm_gain': 1.0 + nrm(ks[23], (DEPTH, W_B), 0.02),
        'w_a_out': nrm(ks[24], (DEPTH, W_A, D_MODEL), W_A ** -0.5),
        'w_b_out': nrm(ks[25], (DEPTH, W_B, D_MODEL), W_B ** -0.5),
        'w_out': nrm(ks[26], (DEPTH, D_MODEL, D_MODEL), D_MODEL ** -0.5),
        'final_norm_gain': 1.0 + nrm(ks[27], (D_MODEL,), 0.02),
    }


def reference(x_prompt, x_sample, c_prompt, c_sample, state_ssm_re, state_ssm_im, state_gla,
              w_ada, b_ada, norm_gain, w_in, lambda_re, lambda_im, log_dt, ssm_b_re, ssm_b_im,
              ssm_c_re, ssm_c_im, d_skip, w_glu, b_glu, w_gate_up, b_gate, gla_norm_gain,
              w_a_out, w_b_out, w_out, final_norm_gain):
    bp = x_prompt.shape[0]
    z_ssm = jnp.zeros((bp, N_GROUPS_A, P_STATE), state_ssm_re.dtype)
    z_gla = jnp.zeros((bp, N_HEADS_B, DK_B, DV_B), state_gla.dtype)
    hp, hs = x_prompt, x_sample
    p_re, p_im, p_gla, s_re, s_im, s_gla = [], [], [], [], [], []
    for l in range(DEPTH):
        lw = (w_ada[l], b_ada[l], norm_gain[l], w_in[l], lambda_re[l], lambda_im[l], log_dt[l],
              ssm_b_re[l], ssm_b_im[l], ssm_c_re[l], ssm_c_im[l], d_skip[l], w_glu[l], b_glu[l],
              w_gate_up[l], b_gate[l], gla_norm_gain[l], w_a_out[l], w_b_out[l], w_out[l])
        hp, r1, i1, g1 = hybrid_layer(hp, c_prompt, z_ssm, z_ssm, z_gla, lw)
        hs, r2, i2, g2 = hybrid_layer(hs, c_sample, state_ssm_re[l], state_ssm_im[l], state_gla[l], lw)
        p_re.append(r1); p_im.append(i1); p_gla.append(g1)
        s_re.append(r2); s_im.append(i2); s_gla.append(g2)
    y_prompt = rms_norm(hp, final_norm_gain)
    y_sample = rms_norm(hs, final_norm_gain)
    sd, gd = state_ssm_re.dtype, state_gla.dtype
    new_ssm_re_prompt = jnp.stack(p_re).astype(sd)
    new_ssm_im_prompt = jnp.stack(p_im).astype(sd)
    new_gla_prompt = jnp.stack(p_gla).astype(gd)
    new_ssm_re_sample = jnp.stack(s_re).astype(sd)
    new_ssm_im_sample = jnp.stack(s_im).astype(sd)
    new_gla_sample = jnp.stack(s_gla).astype(gd)
    return (y_prompt, y_sample, new_ssm_re_prompt, new_ssm_im_prompt, new_gla_prompt,
            new_ssm_re_sample, new_ssm_im_sample, new_gla_sample)
```

```python
import functools

import numpy as np
import jax
import jax.numpy as jnp
from jax import lax
from jax.experimental import pallas as pl
from jax.experimental.pallas import tpu as pltpu

F32 = jnp.float32
BF16 = jnp.bfloat16

D_MODEL = 2048
W_A = D_MODEL // 2
GROUP_A = 16
N_GROUPS_A = W_A // GROUP_A
P_STATE = 64
N_STATE = N_GROUPS_A * P_STATE
W_B = D_MODEL // 2
N_HEADS_B = 4
DK_B = W_B // 2 // N_HEADS_B
DV_B = W_B // N_HEADS_B
QK_W = N_HEADS_B * DK_B
GATE_RANK = 16
GATE_TAU = 16.0
EPS = 1e-6

LANES = 128
SUBLANES = 8
V7X_VMEM_BYTES = 64 * 1024 * 1024
VMEM_LIMIT_BYTES = 56 * 1024 * 1024

STATE_SLABS = N_STATE // (SUBLANES * LANES)
STATE_ROW = N_STATE // SUBLANES
S5_TILE_GROUPS = 16
S5_TILES = N_GROUPS_A // S5_TILE_GROUPS
S5_TILE_IN = S5_TILE_GROUPS * GROUP_A
S5_TILE_ST = S5_TILE_GROUPS * P_STATE

COL_UA, COL_ZA, COL_MA, COL_MB, COL_V, COL_ZB, COL_Q, COL_K = 0, 1024, 2048, 4096, 6144, 7168, 8192, 8704
PROJ_COLS = 9216


def _dot(a, b):
    return jnp.dot(a, b, preferred_element_type=F32)


def _dot_nt(a, b):
    return lax.dot_general(a, b, (((1,), (1,)), ((), ())), preferred_element_type=F32)


def _dot_tn(a, b):
    return lax.dot_general(a, b, (((0,), (0,)), ((), ())), preferred_element_type=F32)


def _shr(x, pow2):
    sh = int(pow2).bit_length() - 1
    assert 1 << sh == pow2
    return jnp.right_shift(x, sh)


def _cparams(n_axes):
    return pltpu.CompilerParams(dimension_semantics=("arbitrary",) * n_axes,
                                vmem_limit_bytes=VMEM_LIMIT_BYTES)


def _seq_rows(ref, row0, rc, rows_per_seq, seq_base):
    s = seq_base + lax.div(row0, jnp.int32(rows_per_seq))
    if rows_per_seq >= rc:
        return ref[pl.ds(s, 1), :]
    assert rc == 2 * rows_per_seq
    rid = lax.broadcasted_iota(jnp.int32, (rc, 1), 0)
    return jnp.where(rid < rows_per_seq, ref[pl.ds(s, 1), :], ref[pl.ds(s + 1, 1), :])


def _mod_kernel(c_ref, w_ref, b_ref, o_ref):
    s = jax.nn.silu(c_ref[...]).astype(BF16)
    o_ref[...] = _dot(s, w_ref[...].astype(BF16)) + b_ref[...]


def _mod_call(c_all, w_ada, b_ada):
    ns = c_all.shape[0]
    tn = 512
    return pl.pallas_call(
        _mod_kernel,
        grid=(3 * D_MODEL // tn,),
        in_specs=[pl.BlockSpec((ns, D_MODEL), lambda n: (0, 0)),
                  pl.BlockSpec((D_MODEL, tn), lambda n: (0, n)),
                  pl.BlockSpec((1, tn), lambda n: (0, n))],
        out_specs=pl.BlockSpec((ns, tn), lambda n: (0, n)),
        out_shape=jax.ShapeDtypeStruct((ns, 3 * D_MODEL), F32),
        compiler_params=_cparams(1),
        name="adaln_mod",
    )(c_all, w_ada, b_ada)


def _s5prep_kernel(lr_ref, li_ref, ldt_ref, bre_ref, bim_ref, cre_ref, cim_ref,
                   are_ref, aim_ref, wbre_ref, wbim_ref, wc_ref):
    dt = jnp.exp(ldt_ref[...])
    lr, li = lr_ref[...], li_ref[...]
    mag = jnp.exp(lr * dt)
    ab_re, ab_im = mag * jnp.cos(li * dt), mag * jnp.sin(li * dt)
    nr, ni = ab_re - 1.0, ab_im
    den = lr * lr + li * li
    cf_re = (nr * lr + ni * li) / den
    cf_im = (ni * lr - nr * li) / den
    are_ref[...] = ab_re
    aim_ref[...] = ab_im
    r = lax.broadcasted_iota(jnp.int32, (S5_TILE_IN, S5_TILE_ST), 0)
    c = lax.broadcasted_iota(jnp.int32, (S5_TILE_IN, S5_TILE_ST), 1)
    on_b = _shr(r, GROUP_A) == _shr(c, P_STATE)
    bre, bim = bre_ref[...], bim_ref[...]
    wbre_ref[...] = jnp.where(on_b, cf_re * bre - cf_im * bim, 0.0).astype(BF16)
    wbim_ref[...] = jnp.where(on_b, cf_re * bim + cf_im * bre, 0.0).astype(BF16)
    r = lax.broadcasted_iota(jnp.int32, (S5_TILE_ST, S5_TILE_IN), 0)
    c = lax.broadcasted_iota(jnp.int32, (S5_TILE_ST, S5_TILE_IN), 1)
    on_c = _shr(r, P_STATE) == _shr(c, GROUP_A)
    wc_ref[0:S5_TILE_ST, :] = jnp.where(on_c, cre_ref[...], 0.0).astype(BF16)
    wc_ref[S5_TILE_ST:2 * S5_TILE_ST, :] = jnp.where(on_c, -cim_ref[...], 0.0).astype(BF16)


def _s5prep_call(lam_re, lam_im, log_dt, b_re, b_im, c_re, c_im):
    row = lambda a: a.reshape(1, N_STATE)
    ldt = jnp.broadcast_to(log_dt[:, None], (N_GROUPS_A, P_STATE))
    bt = lambda b: jnp.tile(b.transpose(2, 0, 1).reshape(GROUP_A, N_STATE), (S5_TILE_GROUPS, 1))
    ct = lambda c: jnp.tile(c.transpose(0, 2, 1).reshape(N_STATE, GROUP_A), (1, S5_TILE_GROUPS))
    rspec = pl.BlockSpec((1, S5_TILE_ST), lambda i: (0, i))
    bspec = pl.BlockSpec((S5_TILE_IN, S5_TILE_ST), lambda i: (0, i))
    cspec = pl.BlockSpec((S5_TILE_ST, S5_TILE_IN), lambda i: (i, 0))
    return pl.pallas_call(
        _s5prep_kernel,
        grid=(S5_TILES,),
        in_specs=[rspec, rspec, rspec, bspec, bspec, cspec, cspec],
        out_specs=[rspec, rspec, bspec, bspec,
                   pl.BlockSpec((None, 2 * S5_TILE_ST, S5_TILE_IN), lambda i: (i, 0, 0))],
        out_shape=[jax.ShapeDtypeStruct((1, N_STATE), F32), jax.ShapeDtypeStruct((1, N_STATE), F32),
                   jax.ShapeDtypeStruct((S5_TILE_IN, N_STATE), BF16),
                   jax.ShapeDtypeStruct((S5_TILE_IN, N_STATE), BF16),
                   jax.ShapeDtypeStruct((S5_TILES, 2 * S5_TILE_ST, S5_TILE_IN), BF16)],
        compiler_params=_cparams(1),
        name="s5_prep",
    )(row(lam_re), row(lam_im), row(ldt), bt(b_re), bt(b_im), ct(c_re), ct(c_im))


def _inproj_kernel(x_ref, shift_ref, scale_ref, gain_ref, w_ref, wg_ref, p_ref, pg_ref, h_scr,
                   *, tm, rc, rows_per_seq, seq_base):
    mt, n = pl.program_id(0), pl.program_id(1)

    @pl.when(n == 0)
    def _():
        gain = gain_ref[...]

        def body(i, carry):
            r0 = pl.multiple_of(i * rc, rc)
            row0 = mt * tm + r0
            xx = x_ref[pl.ds(r0, rc), :]
            y = xx * lax.rsqrt(jnp.mean(xx * xx, axis=-1, keepdims=True) + EPS) * gain
            sc = _seq_rows(scale_ref, row0, rc, rows_per_seq, seq_base)
            sh = _seq_rows(shift_ref, row0, rc, rows_per_seq, seq_base)
            h_scr[pl.ds(r0, rc), :] = (y * (1.0 + sc) + sh).astype(BF16)
            return carry

        lax.fori_loop(0, tm // rc, body, 0)
        pg_ref[...] = _dot(h_scr[...], wg_ref[...])

    p_ref[...] = _dot(h_scr[...], w_ref[...])


def _inproj_call(x2d, mod, gain, w_cat, w_g, *, rows_per_seq, seq_base, tm, tn):
    tokens = x2d.shape[0]
    ns = mod.shape[0]
    rc = 128 if rows_per_seq >= 128 else 2 * rows_per_seq
    kern = functools.partial(_inproj_kernel, tm=tm, rc=rc, rows_per_seq=rows_per_seq, seq_base=seq_base)
    return pl.pallas_call(
        kern,
        grid=(tokens // tm, PROJ_COLS // tn),
        in_specs=[pl.BlockSpec((tm, D_MODEL), lambda m, n: (m, 0)),
                  pl.BlockSpec((ns, D_MODEL), lambda m, n: (0, 0)),
                  pl.BlockSpec((ns, D_MODEL), lambda m, n: (0, 1)),
                  pl.BlockSpec((1, D_MODEL), lambda m, n: (0, 0)),
                  pl.BlockSpec((D_MODEL, tn), lambda m, n: (0, n)),
                  pl.BlockSpec((D_MODEL, LANES), lambda m, n: (0, 0))],
        out_specs=[pl.BlockSpec((tm, tn), lambda m, n: (m, n)),
                   pl.BlockSpec((tm, LANES), lambda m, n: (m, 0))],
        out_shape=[jax.ShapeDtypeStruct((tokens, PROJ_COLS), F32),
                   jax.ShapeDtypeStruct((tokens, LANES), F32)],
        scratch_shapes=[pltpu.VMEM((tm, D_MODEL), BF16)],
        compiler_params=_cparams(2),
        name="in_proj",
    )(x2d, mod, mod, gain, w_cat, w_g)


def _s5_kernel(*refs, tc, ls, has_h0):
    if has_h0:
        (u_ref, za_ref, ma_ref, h0re_ref, h0im_ref, wbre_ref, wbim_ref, wc_ref, are_ref, aim_ref,
         dskip_ref, wglu_ref, bglu_ref, waout_ref, apart_ref, fre_ref, fim_ref, xr, xi, hc_re, hc_im) = refs
    else:
        (u_ref, za_ref, ma_ref, wbre_ref, wbim_ref, wc_ref, are_ref, aim_ref,
         dskip_ref, wglu_ref, bglu_ref, waout_ref, apart_ref, fre_ref, fim_ref, xr, xi, hc_re, hc_im) = refs
    ct = pl.program_id(1)
    n_seq = tc // ls
    u = u_ref[...]
    ub = u.astype(BF16)

    for i in range(S5_TILES):
        ui = ub[:, i * S5_TILE_IN:(i + 1) * S5_TILE_IN]
        for w_ref, x_scr in ((wbre_ref, xr), (wbim_ref, xi)):
            res = _dot(ui, w_ref[:, i * S5_TILE_ST:(i + 1) * S5_TILE_ST])
            for jj in range(S5_TILE_ST // STATE_ROW):
                j = i * (S5_TILE_ST // STATE_ROW) + jj
                for k in range(STATE_SLABS):
                    c0 = jj * STATE_ROW + k * LANES
                    x_scr[k, pl.ds(j, tc, stride=SUBLANES), :] = res[:, c0:c0 + LANES]

    a_re, a_im = are_ref[...], aim_ref[...]
    slabs = lambda v: jnp.stack([v[:, k * LANES:(k + 1) * LANES] for k in range(STATE_SLABS)])
    unslab = lambda v: jnp.concatenate([v[k] for k in range(STATE_SLABS)], axis=1)

    def seq_body(s, carry):
        if has_h0:
            h_re, h_im = slabs(h0re_ref[s]), slabs(h0im_ref[s])
        else:
            first = ct == 0
            h_re = jnp.where(first, 0.0, hc_re[...])
            h_im = jnp.where(first, 0.0, hc_im[...])

        def step(t, h):
            hr, hi = h
            off = pl.multiple_of((s * ls + t) * SUBLANES, SUBLANES)
            nr = a_re * hr - a_im * hi + xr[:, pl.ds(off, SUBLANES), :]
            ni = a_re * hi + a_im * hr + xi[:, pl.ds(off, SUBLANES), :]
            xr[:, pl.ds(off, SUBLANES), :] = nr
            xi[:, pl.ds(off, SUBLANES), :] = ni
            return nr, ni

        h_re, h_im = lax.fori_loop(0, ls, step, (h_re, h_im), unroll=8)
        if has_h0:
            fre_ref[s] = unslab(h_re)
            fim_ref[s] = unslab(h_im)
        else:
            hc_re[...] = h_re
            hc_im[...] = h_im
        return carry

    lax.fori_loop(0, n_seq, seq_body, 0)
    if not has_h0:
        fre_ref[0] = unslab(hc_re[...])
        fim_ref[0] = unslab(hc_im[...])

    ys = []
    for i in range(S5_TILES):
        parts = []
        for x_scr in (xr, xi):
            for jj in range(S5_TILE_ST // STATE_ROW):
                j = i * (S5_TILE_ST // STATE_ROW) + jj
                for k in range(STATE_SLABS):
                    parts.append(x_scr[k, pl.ds(j, tc, stride=SUBLANES), :])
        hcat = jnp.concatenate(parts, axis=1).astype(BF16)
        ys.append(_dot(hcat, wc_ref[i]))
    y = jnp.concatenate(ys, axis=1) + dskip_ref[...] * u
    g = jax.nn.gelu(y)
    a_y = g * jax.nn.sigmoid(_dot(g.astype(BF16), wglu_ref[...]) + bglu_ref[...])
    gated = (a_y * jax.nn.silu(za_ref[...])).astype(BF16)
    apart_ref[...] = jax.nn.sigmoid(ma_ref[...]) * _dot(gated, waout_ref[...])


def _s5_call(proj, prep, d_skip, w_glu, b_glu, w_a_out, h0, *, n_batch, seq_len, tc):
    a_re, a_im, wb_re, wb_im, wc = prep
    tokens = proj.shape[0]
    has_h0 = h0 is not None
    if has_h0:
        ls = seq_len
        grid = (tokens // tc, 1)
        tok = lambda b, c: b
        seq_block = tc // ls
    else:
        ls = tc
        grid = (n_batch, seq_len // tc)
        nct = seq_len // tc
        tok = lambda b, c: b * nct + c
        seq_block = 1
    const = lambda shape: pl.BlockSpec(shape, lambda b, c: (0,) * len(shape))
    st_spec = pl.BlockSpec((seq_block, SUBLANES, STATE_ROW), lambda b, c: (b, 0, 0))
    in_specs = [pl.BlockSpec((tc, W_A), lambda b, c: (tok(b, c), COL_UA // W_A)),
                pl.BlockSpec((tc, W_A), lambda b, c: (tok(b, c), COL_ZA // W_A)),
                pl.BlockSpec((tc, D_MODEL), lambda b, c: (tok(b, c), COL_MA // D_MODEL))]
    args = [proj, proj, proj]
    if has_h0:
        in_specs += [st_spec, st_spec]
        args += [h0[0], h0[1]]
    in_specs += [const((S5_TILE_IN, N_STATE)), const((S5_TILE_IN, N_STATE)),
                 const((S5_TILES, 2 * S5_TILE_ST, S5_TILE_IN)),
                 const((STATE_SLABS, SUBLANES, LANES)), const((STATE_SLABS, SUBLANES, LANES)),
                 const((1, W_A)), const((W_A, W_A)), const((1, W_A)), const((W_A, D_MODEL))]
    args += [wb_re, wb_im, wc, a_re, a_im, d_skip, w_glu, b_glu, w_a_out]
    n_seq_total = tokens // seq_len
    return pl.pallas_call(
        functools.partial(_s5_kernel, tc=tc, ls=ls, has_h0=has_h0),
        grid=grid,
        in_specs=in_specs,
        out_specs=[pl.BlockSpec((tc, D_MODEL), lambda b, c: (tok(b, c), 0)), st_spec, st_spec],
        out_shape=[jax.ShapeDtypeStruct((tokens, D_MODEL), F32),
                   jax.ShapeDtypeStruct((n_seq_total, SUBLANES, STATE_ROW), F32),
                   jax.ShapeDtypeStruct((n_seq_total, SUBLANES, STATE_ROW), F32)],
        scratch_shapes=[pltpu.VMEM((STATE_SLABS, tc * SUBLANES, LANES), F32),
                        pltpu.VMEM((STATE_SLABS, tc * SUBLANES, LANES), F32),
                        pltpu.VMEM((STATE_SLABS, SUBLANES, LANES), F32),
                        pltpu.VMEM((STATE_SLABS, SUBLANES, LANES), F32)],
        compiler_params=_cparams(2),
        name="s5_branch",
    )(*args)


def _gla_levels(t, ls):
    ms = []
    m = ls
    while m >= 1:
        ms.append(m)
        m //= 2
    idx = np.arange(t)
    mats = []
    for m in ms:
        same = (idx[:, None] // m) == (idx[None, :] // m)
        mats.append(same & (idx[None, :] <= idx[:, None]))
        mats.append(same)
    return ms, np.concatenate(mats, axis=0).astype(np.float32)


def _gla_kernel(*refs, t, ls, has_s0, ms):
    if has_s0:
        (q_ref, k_ref, v_ref, zb_ref, mb_ref, glr_ref, apart_ref, s0_ref, lv_ref, wg_ref, bg_ref,
         gg_ref, wbout_ref, merged_ref, sfin_ref, s_scr) = refs
    else:
        (q_ref, k_ref, v_ref, zb_ref, mb_ref, glr_ref, apart_ref, lv_ref, wg_ref, bg_ref,
         gg_ref, wbout_ref, merged_ref, sfin_ref, s_scr) = refs
    ct = pl.program_id(1)
    n_seq = t // ls

    ga = jax.nn.log_sigmoid(_dot(glr_ref[...].astype(BF16), wg_ref[...]) + bg_ref[...]) * (1.0 / GATE_TAU)
    hi = ga.astype(BF16)
    lo = (ga - hi.astype(F32)).astype(BF16)
    e2 = _dot(lv_ref[...], jnp.concatenate([hi, lo], axis=1))
    e_all = e2[:, :QK_W] + e2[:, QK_W:]
    cum = lambda li: e_all[2 * li * t:(2 * li + 1) * t]
    tot = lambda li: e_all[(2 * li + 1) * t:(2 * li + 2) * t]

    ti = lax.broadcasted_iota(jnp.int32, (t, t), 0)
    si = lax.broadcasted_iota(jnp.int32, (t, t), 1)
    masks = {}
    for li in range(1, len(ms)):
        m = ms[li]
        masks[li] = ((_shr(ti, 2 * m) == _shr(si, 2 * m)) & ((_shr(ti, m) & 1) == 1) & ((_shr(si, m) & 1) == 0))
    rid = lax.broadcasted_iota(jnp.int32, (t, 1), 0)

    q, k, v = q_ref[...], k_ref[...], v_ref[...]
    heads = []
    for h in range(N_HEADS_B):
        hs = slice(h * DK_B, (h + 1) * DK_B)
        qs = q[:, hs] * (DK_B ** -0.5)
        kh = k[:, hs]
        vh = v[:, h * DV_B:(h + 1) * DV_B]
        vb = vh.astype(BF16)
        b = cum(0)[:, hs]
        bend = tot(0)[:, hs]
        q_dec = (qs * jnp.exp(b)).astype(BF16)
        k_dec = kh * jnp.exp(bend - b)
        scores = jnp.zeros((t, t), F32)
        for li in range(1, len(ms)):
            eq = cum(li)[:, hs]
            qm = (qs * jnp.exp(eq)).astype(BF16)
            km = (kh * jnp.exp(tot(li)[:, hs] - eq)).astype(BF16)
            scores = scores + jnp.where(masks[li], _dot_nt(qm, km), 0.0)
        o = _dot(scores.astype(BF16), vb) + jnp.sum(qs * kh, axis=-1, keepdims=True) * vh
        o_inter = []
        for i in range(n_seq):
            r0 = i * ls
            if has_s0:
                s_old = s0_ref[i, h]
            else:
                s_old = jnp.where(ct == 0, 0.0, s_scr[h])
            o_inter.append(_dot(q_dec[r0:r0 + ls], s_old.astype(BF16)))
            kd = k_dec if n_seq == 1 else jnp.where((rid >= r0) & (rid < r0 + ls), k_dec, 0.0)
            d_end = jnp.exp(bend[r0:r0 + 1, :])
            d_col = jnp.broadcast_to(d_end, (DK_B, DK_B)).T
            s_new = jnp.concatenate([d_col] * (DV_B // DK_B), axis=1) * s_old + _dot_tn(kd.astype(BF16), vb)
            if has_s0:
                sfin_ref[i, h] = s_new
            else:
                s_scr[h] = s_new
                sfin_ref[0, h] = s_new
        o = o + (o_inter[0] if n_seq == 1 else jnp.concatenate(o_inter, axis=0))
        heads.append(o * lax.rsqrt(jnp.mean(o * o, axis=-1, keepdims=True) + EPS))
    b_y = jnp.concatenate(heads, axis=1) * gg_ref[...]
    gated = (b_y * jax.nn.silu(zb_ref[...])).astype(BF16)
    merged_ref[...] = apart_ref[...] + jax.nn.sigmoid(mb_ref[...]) * _dot(gated, wbout_ref[...])


def _gla_call(proj, proj_g, a_part, w_gate, b_gate, g_gain, w_b_out, s0, *, n_batch, seq_len, t):
    tokens = proj.shape[0]
    has_s0 = s0 is not None
    if has_s0:
        ls = seq_len
        grid = (tokens // t, 1)
        tok = lambda b, c: b
        seq_block = t // ls
    else:
        ls = t
        nct = seq_len // t
        grid = (n_batch, nct)
        tok = lambda b, c: b * nct + c
        seq_block = 1
    ms, lv = _gla_levels(t, ls)
    lv = jnp.asarray(lv, dtype=BF16)
    const = lambda shape: pl.BlockSpec(shape, lambda b, c: (0,) * len(shape))
    st_spec = pl.BlockSpec((seq_block, N_HEADS_B, DK_B, DV_B), lambda b, c: (b, 0, 0, 0))
    in_specs = [pl.BlockSpec((t, QK_W), lambda b, c: (tok(b, c), COL_Q // QK_W)),
                pl.BlockSpec((t, QK_W), lambda b, c: (tok(b, c), COL_K // QK_W)),
                pl.BlockSpec((t, W_B), lambda b, c: (tok(b, c), COL_V // W_B)),
                pl.BlockSpec((t, W_B), lambda b, c: (tok(b, c), COL_ZB // W_B)),
                pl.BlockSpec((t, D_MODEL), lambda b, c: (tok(b, c), COL_MB // D_MODEL)),
                pl.BlockSpec((t, LANES), lambda b, c: (tok(b, c), 0)),
                pl.BlockSpec((t, D_MODEL), lambda b, c: (tok(b, c), 0))]
    args = [proj, proj, proj, proj, proj, proj_g, a_part]
    if has_s0:
        in_specs.append(st_spec)
        args.append(s0)
    in_specs += [const(lv.shape), const((LANES, QK_W)), const((1, QK_W)), const((1, W_B)),
                 const((W_B, D_MODEL))]
    args += [lv, w_gate, b_gate, g_gain, w_b_out]
    n_seq_total = tokens // seq_len
    return pl.pallas_call(
        functools.partial(_gla_kernel, t=t, ls=ls, has_s0=has_s0, ms=tuple(ms)),
        grid=grid,
        in_specs=in_specs,
        out_specs=[pl.BlockSpec((t, D_MODEL), lambda b, c: (tok(b, c), 0)), st_spec],
        out_shape=[jax.ShapeDtypeStruct((tokens, D_MODEL), F32),
                   jax.ShapeDtypeStruct((n_seq_total, N_HEADS_B, DK_B, DV_B), F32)],
        scratch_shapes=[pltpu.VMEM((N_HEADS_B, DK_B, DV_B), F32)],
        compiler_params=_cparams(2),
        name="gla_branch",
    )(*args)


def _out_kernel(m_ref, x_ref, gate_ref, w_ref, fg_ref, y_ref, acc, *, tm, rc, rows_per_seq, seq_base):
    mt = pl.program_id(0)
    acc[...] = _dot(m_ref[...].astype(BF16), w_ref[...])
    fg = fg_ref[...]

    def body(i, carry):
        r0 = pl.multiple_of(i * rc, rc)
        gate = _seq_rows(gate_ref, mt * tm + r0, rc, rows_per_seq, seq_base)
        yy = x_ref[pl.ds(r0, rc), :] + gate * acc[pl.ds(r0, rc), :]
        y_ref[pl.ds(r0, rc), :] = yy * lax.rsqrt(jnp.mean(yy * yy, axis=-1, keepdims=True) + EPS) * fg
        return carry

    lax.fori_loop(0, tm // rc, body, 0)


def _out_call(merged, x2d, mod, w_out, fgain, *, rows_per_seq, seq_base, tm):
    tokens = x2d.shape[0]
    ns = mod.shape[0]
    rc = 128 if rows_per_seq >= 128 else 2 * rows_per_seq
    kern = functools.partial(_out_kernel, tm=tm, rc=rc, rows_per_seq=rows_per_seq, seq_base=seq_base)
    return pl.pallas_call(
        kern,
        grid=(tokens // tm,),
        in_specs=[pl.BlockSpec((tm, D_MODEL), lambda m: (m, 0)),
                  pl.BlockSpec((tm, D_MODEL), lambda m: (m, 0)),
                  pl.BlockSpec((ns, D_MODEL), lambda m: (0, 2)),
                  pl.BlockSpec((D_MODEL, D_MODEL), lambda m: (0, 0)),
                  pl.BlockSpec((1, D_MODEL), lambda m: (0, 0))],
        out_specs=pl.BlockSpec((tm, D_MODEL), lambda m: (m, 0)),
        out_shape=jax.ShapeDtypeStruct((tokens, D_MODEL), F32),
        scratch_shapes=[pltpu.VMEM((tm, D_MODEL), F32)],
        compiler_params=_cparams(1),
        name="out_proj",
    )(merged, x2d, mod, w_out, fgain)


def _tile(n, target):
    t = min(n, target)
    assert n % t == 0
    return t


def _layer(x, n_seq_before, mod, weights, prep, s0_ssm, s0_gla):
    (gain, w_cat, w_g, d_skip, w_glu, b_glu, w_a_out, w_gate, b_gate, g_gain, w_b_out, w_out, fgain) = weights
    n_batch, seq_len, _ = x.shape
    tokens = n_batch * seq_len
    x2d = x.reshape(tokens, D_MODEL)
    long_seq = s0_ssm is None
    proj, proj_g = _inproj_call(x2d, mod, gain, w_cat, w_g, rows_per_seq=seq_len, seq_base=n_seq_before,
                                tm=_tile(seq_len if long_seq else tokens, 1024), tn=512)
    a_part, f_re, f_im = _s5_call(proj, prep, d_skip, w_glu, b_glu, w_a_out, s0_ssm,
                                  n_batch=n_batch, seq_len=seq_len,
                                  tc=_tile(seq_len if long_seq else tokens, 256))
    merged, s_fin = _gla_call(proj, proj_g, a_part, w_gate, b_gate, g_gain, w_b_out, s0_gla,
                              n_batch=n_batch, seq_len=seq_len,
                              t=_tile(seq_len if long_seq else tokens, 128 if long_seq else 64))
    y = _out_call(merged, x2d, mod, w_out, fgain, rows_per_seq=seq_len, seq_base=n_seq_before,
                  tm=_tile(seq_len if long_seq else tokens, 512))
    st = lambda f: f.reshape(1, n_batch, N_GROUPS_A, P_STATE)
    return (y.reshape(n_batch, seq_len, D_MODEL), st(f_re), st(f_im),
            s_fin.reshape(1, n_batch, N_HEADS_B, DK_B, DV_B))


def kernel(x_prompt, x_sample, c_prompt, c_sample, state_ssm_re, state_ssm_im, state_gla, w_ada, b_ada, norm_gain, w_in, lambda_re, lambda_im, log_dt, ssm_b_re, ssm_b_im, ssm_c_re, ssm_c_im, d_skip, w_glu, b_glu, w_gate_up, b_gate, gla_norm_gain, w_a_out, w_b_out, w_out, final_norm_gain):
    assert w_ada.shape[0] == 1, "single-layer step"
    n_prompt, n_sample = x_prompt.shape[0], x_sample.shape[0]

    mod = _mod_call(jnp.concatenate([c_prompt, c_sample], axis=0), w_ada[0], b_ada)
    prep = _s5prep_call(lambda_re[0], lambda_im[0], log_dt[0], ssm_b_re[0], ssm_b_im[0], ssm_c_re[0], ssm_c_im[0])
    a_re, a_im, wb_re, wb_im, wc = prep
    tile = lambda a: a.reshape(SUBLANES, STATE_SLABS, LANES).transpose(1, 0, 2)
    prep = (tile(a_re), tile(a_im), wb_re, wb_im, wc)

    w = w_in[0]
    o_q, o_k, o_v, o_zb = 2 * W_A, 2 * W_A + QK_W, 2 * W_A + 2 * QK_W, 2 * W_A + 2 * QK_W + W_B
    o_g = o_zb + W_B
    o_ma = o_g + GATE_RANK
    w_cat = jnp.concatenate([w[:, :2 * W_A], w[:, o_ma:o_ma + 2 * D_MODEL], w[:, o_v:o_g], w[:, o_q:o_v]],
                            axis=1).astype(BF16)
    w_g = jnp.pad(w[:, o_g:o_ma], ((0, 0), (0, LANES - GATE_RANK))).astype(BF16)
    w_gate = jnp.pad(w_gate_up[0], ((0, LANES - GATE_RANK), (0, 0))).astype(BF16)
    weights = (norm_gain, w_cat, w_g, d_skip, w_glu[0].astype(BF16), b_glu, w_a_out[0].astype(BF16),
               w_gate, b_gate, gla_norm_gain, w_b_out[0].astype(BF16), w_out[0].astype(BF16),
               final_norm_gain.reshape(1, D_MODEL))

    y_p, pre, pim, pgla = _layer(x_prompt, 0, mod, weights, prep, None, None)
    st = lambda s: s[0].reshape(n_sample, SUBLANES, STATE_ROW)
    y_s, sre, sim, sgla = _layer(x_sample, n_prompt, mod, weights, prep,
                                 (st(state_ssm_re), st(state_ssm_im)), state_gla[0])
    return (y_p, y_s, pre, pim, pgla, sre, sim, sgla)
```

```python
import functools

import numpy as np
import jax
import jax.numpy as jnp
from jax import lax
from jax.experimental import pallas as pl
from jax.experimental.pallas import tpu as pltpu

F32 = jnp.float32
BF16 = jnp.bfloat16

D_MODEL = 2048
W_A = D_MODEL // 2
GROUP_A = 16
N_GROUPS_A = W_A // GROUP_A
P_STATE = 64
N_STATE = N_GROUPS_A * P_STATE
W_B = D_MODEL // 2
N_HEADS_B = 4
DK_B = W_B // 2 // N_HEADS_B
DV_B = W_B // N_HEADS_B
QK_W = N_HEADS_B * DK_B
GATE_RANK = 16
GATE_TAU = 16.0
EPS = 1e-6

LANES = 128
SUBLANES = 8
V7X_VMEM_BYTES = 64 * 1024 * 1024
VMEM_LIMIT_BYTES = 56 * 1024 * 1024

S5_TILE_GROUPS = 16
S5_TILES = N_GROUPS_A // S5_TILE_GROUPS
S5_TILE_IN = S5_TILE_GROUPS * GROUP_A
S5_TILE_ST = S5_TILE_GROUPS * P_STATE
assert S5_TILE_ST == SUBLANES * LANES

COL_UA = 0
COL_ZA = COL_UA + W_A
COL_Q = COL_ZA + W_A
COL_K = COL_Q + QK_W
COL_V = COL_K + QK_W
COL_ZB = COL_V + W_B
COL_G = COL_ZB + W_B
COL_MA = COL_G + GATE_RANK
COL_MB = COL_MA + D_MODEL
IN_COLS = COL_MB + D_MODEL
PROJ_COLS = (IN_COLS // LANES) * LANES
GATE_OFF = COL_MA % LANES
assert COL_MB % LANES == GATE_OFF and IN_COLS - PROJ_COLS == GATE_OFF


def _dot(a, b):
    return jnp.dot(a, b, preferred_element_type=F32)


def _dot_nt(a, b):
    return lax.dot_general(a, b, (((1,), (1,)), ((), ())), preferred_element_type=F32)


def _dot_tn(a, b):
    return lax.dot_general(a, b, (((0,), (0,)), ((), ())), preferred_element_type=F32)


def _shr(x, pow2):
    sh = int(pow2).bit_length() - 1
    assert 1 << sh == pow2
    return jnp.right_shift(x, sh)


def _cparams(n_axes):
    return pltpu.CompilerParams(dimension_semantics=("arbitrary",) * n_axes,
                                vmem_limit_bytes=VMEM_LIMIT_BYTES)


def _const_spec(shape, n_axes):
    zeros = (0,) * len(shape)
    index_map = (lambda a: zeros) if n_axes == 1 else (lambda a, b: zeros)
    return pl.BlockSpec(shape, index_map, pipeline_mode=pl.Buffered(1))


def _seq_rows(ref, row0, rc, rows_per_seq, seq_base):
    s = seq_base + lax.div(row0, jnp.int32(rows_per_seq))
    if rows_per_seq >= rc:
        return ref[pl.ds(s, 1), :]
    assert rc == 2 * rows_per_seq
    rid = lax.broadcasted_iota(jnp.int32, (rc, 1), 0)
    return jnp.where(rid < rows_per_seq, ref[pl.ds(s, 1), :], ref[pl.ds(s + 1, 1), :])


def _merge_gate(lo_ref, mid_ref, hi_ref):
    m = jnp.concatenate([lo_ref[...], mid_ref[...], hi_ref[...]], axis=1)
    return jax.nn.sigmoid(m[:, GATE_OFF:GATE_OFF + D_MODEL])


def _mod_kernel(c_ref, w_ref, b_ref, o_ref):
    s = jax.nn.silu(c_ref[...]).astype(BF16)
    o_ref[...] = _dot(s, w_ref[...].astype(BF16)) + b_ref[...]


def _mod_call(c_all, w_ada, b_ada):
    ns = c_all.shape[0]
    tn = 512
    return pl.pallas_call(
        _mod_kernel,
        grid=(3 * D_MODEL // tn,),
        in_specs=[_const_spec((ns, D_MODEL), 1),
                  pl.BlockSpec((D_MODEL, tn), lambda n: (0, n)),
                  pl.BlockSpec((1, tn), lambda n: (0, n))],
        out_specs=pl.BlockSpec((ns, tn), lambda n: (0, n)),
        out_shape=jax.ShapeDtypeStruct((ns, 3 * D_MODEL), F32),
        compiler_params=_cparams(1),
        name="adaln_mod",
    )(c_all, w_ada, b_ada)


def _s5prep_kernel(lr_ref, li_ref, ldt_ref, bre_ref, bim_ref, cre_ref, cim_ref,
                   are_ref, aim_ref, wbre_ref, wbim_ref, wc_ref):
    dt = jnp.exp(ldt_ref[...])
    lr, li = lr_ref[...], li_ref[...]
    mag = jnp.exp(lr * dt)
    ab_re, ab_im = mag * jnp.cos(li * dt), mag * jnp.sin(li * dt)
    nr, ni = ab_re - 1.0, ab_im
    den = lr * lr + li * li
    cf_re = (nr * lr + ni * li) / den
    cf_im = (ni * lr - nr * li) / den
    are_ref[...] = ab_re
    aim_ref[...] = ab_im
    r = lax.broadcasted_iota(jnp.int32, (S5_TILE_IN, S5_TILE_ST), 0)
    c = lax.broadcasted_iota(jnp.int32, (S5_TILE_IN, S5_TILE_ST), 1)
    on_b = _shr(r, GROUP_A) == _shr(c, P_STATE)
    bre, bim = bre_ref[...], bim_ref[...]
    wbre_ref[...] = jnp.where(on_b, cf_re * bre - cf_im * bim, 0.0).astype(BF16)
    wbim_ref[...] = jnp.where(on_b, cf_re * bim + cf_im * bre, 0.0).astype(BF16)
    r = lax.broadcasted_iota(jnp.int32, (S5_TILE_ST, S5_TILE_IN), 0)
    c = lax.broadcasted_iota(jnp.int32, (S5_TILE_ST, S5_TILE_IN), 1)
    on_c = _shr(r, P_STATE) == _shr(c, GROUP_A)
    wc_ref[0:S5_TILE_ST, :] = jnp.where(on_c, cre_ref[...], 0.0).astype(BF16)
    wc_ref[S5_TILE_ST:2 * S5_TILE_ST, :] = jnp.where(on_c, -cim_ref[...], 0.0).astype(BF16)


def _s5prep_call(lam_re, lam_im, log_dt, b_re, b_im, c_re, c_im):
    row = lambda a: a.reshape(1, N_STATE)
    ldt = jnp.broadcast_to(log_dt[:, None], (N_GROUPS_A, P_STATE))
    bt = lambda b: jnp.tile(b.transpose(2, 0, 1).reshape(GROUP_A, N_STATE), (S5_TILE_GROUPS, 1))
    ct = lambda c: jnp.tile(c.transpose(0, 2, 1).reshape(N_STATE, GROUP_A), (1, S5_TILE_GROUPS))
    rspec = pl.BlockSpec((1, S5_TILE_ST), lambda i: (0, i))
    bspec = pl.BlockSpec((S5_TILE_IN, S5_TILE_ST), lambda i: (0, i))
    cspec = pl.BlockSpec((S5_TILE_ST, S5_TILE_IN), lambda i: (i, 0))
    return pl.pallas_call(
        _s5prep_kernel,
        grid=(S5_TILES,),
        in_specs=[rspec, rspec, rspec, bspec, bspec, cspec, cspec],
        out_specs=[rspec, rspec, bspec, bspec,
                   pl.BlockSpec((None, 2 * S5_TILE_ST, S5_TILE_IN), lambda i: (i, 0, 0))],
        out_shape=[jax.ShapeDtypeStruct((1, N_STATE), F32), jax.ShapeDtypeStruct((1, N_STATE), F32),
                   jax.ShapeDtypeStruct((S5_TILE_IN, N_STATE), BF16),
                   jax.ShapeDtypeStruct((S5_TILE_IN, N_STATE), BF16),
                   jax.ShapeDtypeStruct((S5_TILES, 2 * S5_TILE_ST, S5_TILE_IN), BF16)],
        compiler_params=_cparams(1),
        name="s5_prep",
    )(row(lam_re), row(lam_im), row(ldt), bt(b_re), bt(b_im), ct(c_re), ct(c_im))


def _inproj_kernel(x_ref, shift_ref, scale_ref, gain_ref, w_ref, wt_ref, p_ref, pt_ref, h_scr,
                   *, tm, rc, rows_per_seq, seq_base):
    mt, n = pl.program_id(0), pl.program_id(1)

    @pl.when(n == 0)
    def _():
        gain = gain_ref[...]

        def body(i, carry):
            r0 = pl.multiple_of(i * rc, rc)
            row0 = mt * tm + r0
            xx = x_ref[pl.ds(r0, rc), :]
            y = xx * lax.rsqrt(jnp.mean(xx * xx, axis=-1, keepdims=True) + EPS) * gain
            sc = _seq_rows(scale_ref, row0, rc, rows_per_seq, seq_base)
            sh = _seq_rows(shift_ref, row0, rc, rows_per_seq, seq_base)
            h_scr[pl.ds(r0, rc), :] = (y * (1.0 + sc) + sh).astype(BF16)
            return carry

        lax.fori_loop(0, tm // rc, body, 0)
        pt_ref[...] = _dot(h_scr[...], wt_ref[...])

    p_ref[...] = _dot(h_scr[...], w_ref[...])


def _inproj_call(x2d, mod, gain, w_bf, w_tail, *, rows_per_seq, seq_base, tm, tn):
    tokens = x2d.shape[0]
    ns = mod.shape[0]
    rc = 128 if rows_per_seq >= 128 else 2 * rows_per_seq
    kern = functools.partial(_inproj_kernel, tm=tm, rc=rc, rows_per_seq=rows_per_seq, seq_base=seq_base)
    return pl.pallas_call(
        kern,
        grid=(tokens // tm, PROJ_COLS // tn),
        in_specs=[pl.BlockSpec((tm, D_MODEL), lambda m, n: (m, 0)),
                  pl.BlockSpec((ns, D_MODEL), lambda m, n: (0, 0), pipeline_mode=pl.Buffered(1)),
                  pl.BlockSpec((ns, D_MODEL), lambda m, n: (0, 1), pipeline_mode=pl.Buffered(1)),
                  _const_spec((1, D_MODEL), 2),
                  pl.BlockSpec((D_MODEL, tn), lambda m, n: (0, n)),
                  _const_spec((D_MODEL, LANES), 2)],
        out_specs=[pl.BlockSpec((tm, tn), lambda m, n: (m, n)),
                   pl.BlockSpec((tm, LANES), lambda m, n: (m, 0))],
        out_shape=[jax.ShapeDtypeStruct((tokens, PROJ_COLS), F32),
                   jax.ShapeDtypeStruct((tokens, LANES), F32)],
        scratch_shapes=[pltpu.VMEM((tm, D_MODEL), BF16)],
        compiler_params=_cparams(2),
        name="in_proj",
    )(x2d, mod, mod, gain, w_bf, w_tail)


def _s5_kernel(*refs, tc, ls, has_h0):
    if has_h0:
        (u_ref, za_ref, ma0_ref, ma1_ref, ma2_ref, h0re_ref, h0im_ref, wbre_ref, wbim_ref, wc_ref,
         are_ref, aim_ref, dskip_ref, wglu_ref, bglu_ref, waout_ref,
         apart_ref, fre_ref, fim_ref, xr, xi, hc_re, hc_im) = refs
    else:
        (u_ref, za_ref, ma0_ref, ma1_ref, ma2_ref, wbre_ref, wbim_ref, wc_ref,
         are_ref, aim_ref, dskip_ref, wglu_ref, bglu_ref, waout_ref,
         apart_ref, fre_ref, fim_ref, xr, xi, hc_re, hc_im) = refs
    ct = pl.program_id(1)
    n_seq = tc // ls
    u = u_ref[...]
    ub = u.astype(BF16)

    for i in range(S5_TILES):
        ui = ub[:, i * S5_TILE_IN:(i + 1) * S5_TILE_IN]
        for w_ref, x_scr in ((wbre_ref, xr), (wbim_ref, xi)):
            res = _dot(ui, w_ref[:, i * S5_TILE_ST:(i + 1) * S5_TILE_ST])
            for j in range(SUBLANES):
                x_scr[i, pl.ds(j, tc, stride=SUBLANES), :] = res[:, j * LANES:(j + 1) * LANES]

    a_re = [are_ref[i] for i in range(S5_TILES)]
    a_im = [aim_ref[i] for i in range(S5_TILES)]
    for s in range(n_seq):
        if has_h0:
            h_re = [h0re_ref[s, i] for i in range(S5_TILES)]
            h_im = [h0im_ref[s, i] for i in range(S5_TILES)]
        else:
            h_re = [jnp.where(ct == 0, 0.0, hc_re[i]) for i in range(S5_TILES)]
            h_im = [jnp.where(ct == 0, 0.0, hc_im[i]) for i in range(S5_TILES)]
        for tl in range(ls):
            r0 = (s * ls + tl) * SUBLANES
            for i in range(S5_TILES):
                nr = a_re[i] * h_re[i] - a_im[i] * h_im[i] + xr[i, r0:r0 + SUBLANES, :]
                ni = a_re[i] * h_im[i] + a_im[i] * h_re[i] + xi[i, r0:r0 + SUBLANES, :]
                xr[i, r0:r0 + SUBLANES, :] = nr
                xi[i, r0:r0 + SUBLANES, :] = ni
                h_re[i], h_im[i] = nr, ni
        for i in range(S5_TILES):
            if has_h0:
                fre_ref[s, i] = h_re[i]
                fim_ref[s, i] = h_im[i]
            else:
                hc_re[i] = h_re[i]
                hc_im[i] = h_im[i]
                fre_ref[0, i] = h_re[i]
                fim_ref[0, i] = h_im[i]

    ys = []
    for i in range(S5_TILES):
        parts = [x_scr[i, pl.ds(j, tc, stride=SUBLANES), :] for x_scr in (xr, xi) for j in range(SUBLANES)]
        ys.append(_dot(jnp.concatenate(parts, axis=1).astype(BF16), wc_ref[i]))
    y = jnp.concatenate(ys, axis=1) + dskip_ref[...] * u
    g = jax.nn.gelu(y)
    a_y = g * jax.nn.sigmoid(_dot(g.astype(BF16), wglu_ref[...]) + bglu_ref[...])
    gated = (a_y * jax.nn.silu(za_ref[...])).astype(BF16)
    apart_ref[...] = _merge_gate(ma0_ref, ma1_ref, ma2_ref) * _dot(gated, waout_ref[...])


def _s5_call(proj, prep, d_skip, w_glu, b_glu, w_a_out, h0, *, n_batch, seq_len, tc):
    a_re, a_im, wb_re, wb_im, wc = prep
    tokens = proj.shape[0]
    has_h0 = h0 is not None
    if has_h0:
        ls = seq_len
        grid = (tokens // tc, 1)
        tok = lambda b, c: b
        seq_block = tc // ls
    else:
        ls = tc
        nct = seq_len // tc
        grid = (n_batch, nct)
        tok = lambda b, c: b * nct + c
        seq_block = 1
    const = lambda shape: _const_spec(shape, 2)
    st_spec = pl.BlockSpec((seq_block, S5_TILES, SUBLANES, LANES), lambda b, c: (b, 0, 0, 0))
    in_specs = [pl.BlockSpec((tc, W_A), lambda b, c: (tok(b, c), COL_UA // W_A)),
                pl.BlockSpec((tc, W_A), lambda b, c: (tok(b, c), COL_ZA // W_A)),
                pl.BlockSpec((tc, W_A), lambda b, c: (tok(b, c), COL_MA // W_A)),
                pl.BlockSpec((tc, W_A), lambda b, c: (tok(b, c), COL_MA // W_A + 1)),
                pl.BlockSpec((tc, LANES), lambda b, c: (tok(b, c), COL_MB // LANES))]
    args = [proj] * 5
    if has_h0:
        in_specs += [st_spec, st_spec]
        args += [h0[0], h0[1]]
    in_specs += [const((S5_TILE_IN, N_STATE)), const((S5_TILE_IN, N_STATE)),
                 const((S5_TILES, 2 * S5_TILE_ST, S5_TILE_IN)),
                 const((S5_TILES, SUBLANES, LANES)), const((S5_TILES, SUBLANES, LANES)),
                 const((1, W_A)), const((W_A, W_A)), const((1, W_A)), const((W_A, D_MODEL))]
    args += [wb_re, wb_im, wc, a_re, a_im, d_skip, w_glu, b_glu, w_a_out]
    n_seq_total = tokens // seq_len
    st_shape = jax.ShapeDtypeStruct((n_seq_total, S5_TILES, SUBLANES, LANES), F32)
    return pl.pallas_call(
        functools.partial(_s5_kernel, tc=tc, ls=ls, has_h0=has_h0),
        grid=grid,
        in_specs=in_specs,
        out_specs=[pl.BlockSpec((tc, D_MODEL), lambda b, c: (tok(b, c), 0)), st_spec, st_spec],
        out_shape=[jax.ShapeDtypeStruct((tokens, D_MODEL), F32), st_shape, st_shape],
        scratch_shapes=[pltpu.VMEM((S5_TILES, tc * SUBLANES, LANES), F32),
                        pltpu.VMEM((S5_TILES, tc * SUBLANES, LANES), F32),
                        pltpu.VMEM((S5_TILES, SUBLANES, LANES), F32),
                        pltpu.VMEM((S5_TILES, SUBLANES, LANES), F32)],
        compiler_params=_cparams(2),
        name="s5_branch",
    )(*args)


def _gla_levels(t, ls):
    ms = []
    m = ls
    while m >= 1:
        ms.append(m)
        m //= 2
    idx = np.arange(t)
    mats = []
    for m in ms:
        same = (idx[:, None] // m) == (idx[None, :] // m)
        mats.append(same & (idx[None, :] <= idx[:, None]))
        mats.append(same)
    return ms, np.concatenate(mats, axis=0).astype(np.float32)


def _gla_kernel(*refs, t, n_sub, ls, has_s0, ms):
    if has_s0:
        (q_ref, k_ref, v_ref, zb_ref, mb0_ref, mb1_ref, mb2_ref, glr_ref, apart_ref, s0_ref, lv_ref,
         wg_ref, bg_ref, gg_ref, wbout_ref, merged_ref, sfin_ref, s_scr) = refs
        assert n_sub == 1
    else:
        (q_ref, k_ref, v_ref, zb_ref, mb0_ref, mb1_ref, mb2_ref, glr_ref, apart_ref, lv_ref,
         wg_ref, bg_ref, gg_ref, wbout_ref, merged_ref, sfin_ref, s_scr) = refs
        assert ls == t
    ct = pl.program_id(1)
    n_seq = t // ls

    ti = lax.broadcasted_iota(jnp.int32, (t, t), 0)
    si = lax.broadcasted_iota(jnp.int32, (t, t), 1)
    masks = {}
    for li in range(1, len(ms)):
        m = ms[li]
        masks[li] = ((_shr(ti, 2 * m) == _shr(si, 2 * m)) & ((_shr(ti, m) & 1) == 1) & ((_shr(si, m) & 1) == 0))
    rid = lax.broadcasted_iota(jnp.int32, (t, 1), 0)

    if not has_s0:
        states = [jnp.where(ct == 0, 0.0, s_scr[h]) for h in range(N_HEADS_B)]
    gated_rows = []
    for c in range(n_sub):
        rows = slice(c * t, (c + 1) * t)
        ga = jax.nn.log_sigmoid(_dot(glr_ref[rows, :].astype(BF16), wg_ref[...]) + bg_ref[...]) * (1.0 / GATE_TAU)
        hi = ga.astype(BF16)
        lo = (ga - hi.astype(F32)).astype(BF16)
        e2 = _dot(lv_ref[...], jnp.concatenate([hi, lo], axis=1))
        e_all = e2[:, :QK_W] + e2[:, QK_W:]
        cum = lambda li: e_all[2 * li * t:(2 * li + 1) * t]
        tot = lambda li: e_all[(2 * li + 1) * t:(2 * li + 2) * t]

        q, k, v = q_ref[rows, :], k_ref[rows, :], v_ref[rows, :]
        heads = []
        for h in range(N_HEADS_B):
            hs = slice(h * DK_B, (h + 1) * DK_B)
            qs = q[:, hs] * (DK_B ** -0.5)
            kh = k[:, hs]
            vh = v[:, h * DV_B:(h + 1) * DV_B]
            vb = vh.astype(BF16)
            b = cum(0)[:, hs]
            bend = tot(0)[:, hs]
            q_dec = (qs * jnp.exp(b)).astype(BF16)
            k_dec = kh * jnp.exp(bend - b)
            scores = jnp.zeros((t, t), F32)
            for li in range(1, len(ms)):
                eq = cum(li)[:, hs]
                qm = (qs * jnp.exp(eq)).astype(BF16)
                km = (kh * jnp.exp(tot(li)[:, hs] - eq)).astype(BF16)
                scores = scores + jnp.where(masks[li], _dot_nt(qm, km), 0.0)
            o = _dot(scores.astype(BF16), vb) + jnp.sum(qs * kh, axis=-1, keepdims=True) * vh
            o_inter = []
            for i in range(n_seq):
                r0 = i * ls
                s_old = s0_ref[i, h] if has_s0 else states[h]
                o_inter.append(_dot(q_dec[r0:r0 + ls], s_old.astype(BF16)))
                kd = k_dec if n_seq == 1 else jnp.where((rid >= r0) & (rid < r0 + ls), k_dec, 0.0)
                d_end = jnp.exp(bend[r0:r0 + 1, :])
                d_col = jnp.broadcast_to(d_end, (DK_B, DK_B)).T
                s_new = jnp.concatenate([d_col] * (DV_B // DK_B), axis=1) * s_old + _dot_tn(kd.astype(BF16), vb)
                if has_s0:
                    sfin_ref[i, h] = s_new
                else:
                    states[h] = s_new
            o = o + (o_inter[0] if n_seq == 1 else jnp.concatenate(o_inter, axis=0))
            heads.append(o * lax.rsqrt(jnp.mean(o * o, axis=-1, keepdims=True) + EPS))
        b_y = jnp.concatenate(heads, axis=1) * gg_ref[...]
        gated_rows.append((b_y * jax.nn.silu(zb_ref[rows, :])).astype(BF16))
    if not has_s0:
        for h in range(N_HEADS_B):
            s_scr[h] = states[h]
            sfin_ref[0, h] = states[h]
    gated = gated_rows[0] if n_sub == 1 else jnp.concatenate(gated_rows, axis=0)
    merged_ref[...] = apart_ref[...] + _merge_gate(mb0_ref, mb1_ref, mb2_ref) * _dot(gated, wbout_ref[...])


def _gla_call(proj, proj_tail, a_part, w_gate, b_gate, g_gain, w_b_out, s0, *, n_batch, seq_len, t, n_sub):
    tokens = proj.shape[0]
    has_s0 = s0 is not None
    tt = t * n_sub
    if has_s0:
        ls = seq_len
        grid = (tokens // tt, 1)
        tok = lambda b, c: b
        seq_block = tt // ls
    else:
        ls = t
        nct = seq_len // tt
        grid = (n_batch, nct)
        tok = lambda b, c: b * nct + c
        seq_block = 1
    ms, lv = _gla_levels(t, ls)
    lv = jnp.asarray(lv, dtype=BF16)
    const = lambda shape: _const_spec(shape, 2)
    st_spec = pl.BlockSpec((seq_block, N_HEADS_B, DK_B, DV_B), lambda b, c: (b, 0, 0, 0))
    in_specs = [pl.BlockSpec((tt, QK_W), lambda b, c: (tok(b, c), COL_Q // QK_W)),
                pl.BlockSpec((tt, QK_W), lambda b, c: (tok(b, c), COL_K // QK_W)),
                pl.BlockSpec((tt, W_B), lambda b, c: (tok(b, c), COL_V // W_B)),
                pl.BlockSpec((tt, W_B), lambda b, c: (tok(b, c), COL_ZB // W_B)),
                pl.BlockSpec((tt, W_B), lambda b, c: (tok(b, c), COL_MB // W_B)),
                pl.BlockSpec((tt, W_B), lambda b, c: (tok(b, c), COL_MB // W_B + 1)),
                pl.BlockSpec((tt, LANES), lambda b, c: (tok(b, c), 0)),
                pl.BlockSpec((tt, LANES), lambda b, c: (tok(b, c), COL_G // LANES)),
                pl.BlockSpec((tt, D_MODEL), lambda b, c: (tok(b, c), 0))]
    args = [proj, proj, proj, proj, proj, proj, proj_tail, proj, a_part]
    if has_s0:
        in_specs.append(st_spec)
        args.append(s0)
    in_specs += [const(lv.shape), const((LANES, QK_W)), const((1, QK_W)), const((1, W_B)),
                 const((W_B, D_MODEL))]
    args += [lv, w_gate, b_gate, g_gain, w_b_out]
    n_seq_total = tokens // seq_len
    return pl.pallas_call(
        functools.partial(_gla_kernel, t=t, n_sub=n_sub, ls=ls, has_s0=has_s0, ms=tuple(ms)),
        grid=grid,
        in_specs=in_specs,
        out_specs=[pl.BlockSpec((tt, D_MODEL), lambda b, c: (tok(b, c), 0)), st_spec],
        out_shape=[jax.ShapeDtypeStruct((tokens, D_MODEL), F32),
                   jax.ShapeDtypeStruct((n_seq_total, N_HEADS_B, DK_B, DV_B), F32)],
        scratch_shapes=[pltpu.VMEM((N_HEADS_B, DK_B, DV_B), F32)],
        compiler_params=_cparams(2),
        name="gla_branch",
    )(*args)


def _out_kernel(m_ref, x_ref, gate_ref, w_ref, fg_ref, y_ref, acc, *, tm, rc, rows_per_seq, seq_base):
    mt = pl.program_id(0)
    acc[...] = _dot(m_ref[...].astype(BF16), w_ref[...])
    fg = fg_ref[...]

    def body(i, carry):
        r0 = pl.multiple_of(i * rc, rc)
        gate = _seq_rows(gate_ref, mt * tm + r0, rc, rows_per_seq, seq_base)
        yy = x_ref[pl.ds(r0, rc), :] + gate * acc[pl.ds(r0, rc), :]
        y_ref[pl.ds(r0, rc), :] = yy * lax.rsqrt(jnp.mean(yy * yy, axis=-1, keepdims=True) + EPS) * fg
        return carry

    lax.fori_loop(0, tm // rc, body, 0)


def _out_call(merged, x2d, mod, w_out, fgain, *, rows_per_seq, seq_base, tm):
    tokens = x2d.shape[0]
    ns = mod.shape[0]
    rc = 128 if rows_per_seq >= 128 else 2 * rows_per_seq
    kern = functools.partial(_out_kernel, tm=tm, rc=rc, rows_per_seq=rows_per_seq, seq_base=seq_base)
    return pl.pallas_call(
        kern,
        grid=(tokens // tm,),
        in_specs=[pl.BlockSpec((tm, D_MODEL), lambda m: (m, 0)),
                  pl.BlockSpec((tm, D_MODEL), lambda m: (m, 0)),
                  pl.BlockSpec((ns, D_MODEL), lambda m: (0, 2), pipeline_mode=pl.Buffered(1)),
                  _const_spec((D_MODEL, D_MODEL), 1),
                  _const_spec((1, D_MODEL), 1)],
        out_specs=pl.BlockSpec((tm, D_MODEL), lambda m: (m, 0)),
        out_shape=jax.ShapeDtypeStruct((tokens, D_MODEL), F32),
        scratch_shapes=[pltpu.VMEM((tm, D_MODEL), F32)],
        compiler_params=_cparams(1),
        name="out_proj",
    )(merged, x2d, mod, w_out, fgain)


def _tile(n, target):
    t = min(n, target)
    assert n % t == 0
    return t


def _layer(x, n_seq_before, mod, weights, prep, s0_ssm, s0_gla):
    (gain, w_bf, w_tail, d_skip, w_glu, b_glu, w_a_out, w_gate, b_gate, g_gain, w_b_out, w_out, fgain) = weights
    n_batch, seq_len, _ = x.shape
    tokens = n_batch * seq_len
    x2d = x.reshape(tokens, D_MODEL)
    long_seq = s0_ssm is None
    span = seq_len if long_seq else tokens
    proj, proj_tail = _inproj_call(x2d, mod, gain, w_bf, w_tail, rows_per_seq=seq_len, seq_base=n_seq_before,
                                   tm=_tile(span, 1024), tn=1024)
    a_part, f_re, f_im = _s5_call(proj, prep, d_skip, w_glu, b_glu, w_a_out, s0_ssm,
                                  n_batch=n_batch, seq_len=seq_len, tc=_tile(span, 256))
    if long_seq:
        t = _tile(seq_len, 128)
        n_sub = 2 if seq_len % (2 * t) == 0 else 1
    else:
        t, n_sub = _tile(tokens, 64), 1
    merged, s_fin = _gla_call(proj, proj_tail, a_part, w_gate, b_gate, g_gain, w_b_out, s0_gla,
                              n_batch=n_batch, seq_len=seq_len, t=t, n_sub=n_sub)
    y = _out_call(merged, x2d, mod, w_out, fgain, rows_per_seq=seq_len, seq_base=n_seq_before,
                  tm=_tile(span, 512))
    st = lambda f: f.reshape(1, n_batch, N_GROUPS_A, P_STATE)
    return (y.reshape(n_batch, seq_len, D_MODEL), st(f_re), st(f_im),
            s_fin.reshape(1, n_batch, N_HEADS_B, DK_B, DV_B))


def kernel(x_prompt, x_sample, c_prompt, c_sample, state_ssm_re, state_ssm_im, state_gla, w_ada, b_ada, norm_gain, w_in, lambda_re, lambda_im, log_dt, ssm_b_re, ssm_b_im, ssm_c_re, ssm_c_im, d_skip, w_glu, b_glu, w_gate_up, b_gate, gla_norm_gain, w_a_out, w_b_out, w_out, final_norm_gain):
    assert w_ada.shape[0] == 1, "single-layer step"
    assert w_in.shape[2] == IN_COLS
    n_prompt, n_sample = x_prompt.shape[0], x_sample.shape[0]

    mod = _mod_call(jnp.concatenate([c_prompt, c_sample], axis=0), w_ada[0], b_ada)
    a_re, a_im, wb_re, wb_im, wc = _s5prep_call(lambda_re[0], lambda_im[0], log_dt[0], ssm_b_re[0], ssm_b_im[0],
                                                ssm_c_re[0], ssm_c_im[0])
    state_tiles = lambda a: a.reshape(-1, S5_TILES, SUBLANES, LANES)
    prep = (state_tiles(a_re)[0], state_tiles(a_im)[0], wb_re, wb_im, wc)

    w_bf = w_in[0].astype(BF16)
    w_tail = jnp.pad(w_in[0][:, PROJ_COLS:], ((0, 0), (0, LANES - GATE_OFF))).astype(BF16)
    w_gate = jnp.pad(w_gate_up[0], ((0, LANES - GATE_RANK), (0, 0))).astype(BF16)
    weights = (norm_gain, w_bf, w_tail, d_skip, w_glu[0].astype(BF16), b_glu, w_a_out[0].astype(BF16),
               w_gate, b_gate, gla_norm_gain, w_b_out[0].astype(BF16), w_out[0].astype(BF16),
               final_norm_gain.reshape(1, D_MODEL))

    y_p, pre, pim, pgla = _layer(x_prompt, 0, mod, weights, prep, None, None)
    y_s, sre, sim, sgla = _layer(x_sample, n_prompt, mod, weights, prep,
                                 (state_tiles(state_ssm_re[0]), state_tiles(state_ssm_im[0])), state_gla[0])
    return (y_p, y_s, pre, pim, pgla, sre, sim, sgla)
```

```python
import functools

import numpy as np
import jax
import jax.numpy as jnp
from jax import lax
from jax.experimental import pallas as pl
from jax.experimental.pallas import tpu as pltpu

F32 = jnp.float32
BF16 = jnp.bfloat16

D_MODEL = 2048
W_A = D_MODEL // 2
GROUP_A = 16
N_GROUPS_A = W_A // GROUP_A
P_STATE = 64
N_STATE = N_GROUPS_A * P_STATE
W_B = D_MODEL // 2
N_HEADS_B = 4
DK_B = W_B // 2 // N_HEADS_B
DV_B = W_B // N_HEADS_B
QK_W = N_HEADS_B * DK_B
GATE_RANK = 16
GATE_TAU = 16.0
EPS = 1e-6

LANES = 128
SUBLANES = 8
V7X_VMEM_BYTES = 64 * 1024 * 1024
VMEM_LIMIT_BYTES = 56 * 1024 * 1024

S5_TILE_GROUPS = 16
S5_TILES = N_GROUPS_A // S5_TILE_GROUPS
S5_TILE_IN = S5_TILE_GROUPS * GROUP_A
S5_TILE_ST = S5_TILE_GROUPS * P_STATE
assert S5_TILE_ST == SUBLANES * LANES

COL_UA = 0
COL_ZA = COL_UA + W_A
COL_Q = COL_ZA + W_A
COL_K = COL_Q + QK_W
COL_V = COL_K + QK_W
COL_ZB = COL_V + W_B
COL_G = COL_ZB + W_B
COL_MA = COL_G + GATE_RANK
COL_MB = COL_MA + D_MODEL
IN_COLS = COL_MB + D_MODEL
PROJ_COLS = (IN_COLS // LANES) * LANES
GATE_OFF = COL_MA % LANES
assert COL_MB % LANES == GATE_OFF and IN_COLS - PROJ_COLS == GATE_OFF


def _dot(a, b):
    return jnp.dot(a, b, preferred_element_type=F32)


def _dot_nt(a, b):
    return lax.dot_general(a, b, (((1,), (1,)), ((), ())), preferred_element_type=F32)


def _dot_tn(a, b):
    return lax.dot_general(a, b, (((0,), (0,)), ((), ())), preferred_element_type=F32)


def _shr(x, pow2):
    sh = int(pow2).bit_length() - 1
    assert 1 << sh == pow2
    return jnp.right_shift(x, sh)


def _cparams(n_axes):
    return pltpu.CompilerParams(dimension_semantics=("arbitrary",) * n_axes,
                                vmem_limit_bytes=VMEM_LIMIT_BYTES)


def _const_spec(shape, n_axes):
    zeros = (0,) * len(shape)
    index_map = (lambda a: zeros) if n_axes == 1 else (lambda a, b: zeros)
    return pl.BlockSpec(shape, index_map, pipeline_mode=pl.Buffered(1))


def _seq_rows(ref, row0, rc, rows_per_seq, seq_base):
    s = seq_base + lax.div(row0, jnp.int32(rows_per_seq))
    if rows_per_seq >= rc:
        return ref[pl.ds(s, 1), :]
    assert rc == 2 * rows_per_seq
    rid = lax.broadcasted_iota(jnp.int32, (rc, 1), 0)
    return jnp.where(rid < rows_per_seq, ref[pl.ds(s, 1), :], ref[pl.ds(s + 1, 1), :])


def _merge_gate(lo_ref, mid_ref, hi_ref):
    m = jnp.concatenate([lo_ref[...], mid_ref[...], hi_ref[...]], axis=1)
    return jax.nn.sigmoid(m[:, GATE_OFF:GATE_OFF + D_MODEL])


def _mod_kernel(c_ref, w_ref, b_ref, o_ref):
    s = jax.nn.silu(c_ref[...]).astype(BF16)
    o_ref[...] = _dot(s, w_ref[...].astype(BF16)) + b_ref[...]


def _mod_call(c_all, w_ada, b_ada):
    ns = c_all.shape[0]
    tn = 512
    return pl.pallas_call(
        _mod_kernel,
        grid=(3 * D_MODEL // tn,),
        in_specs=[_const_spec((ns, D_MODEL), 1),
                  pl.BlockSpec((D_MODEL, tn), lambda n: (0, n)),
                  pl.BlockSpec((1, tn), lambda n: (0, n))],
        out_specs=pl.BlockSpec((ns, tn), lambda n: (0, n)),
        out_shape=jax.ShapeDtypeStruct((ns, 3 * D_MODEL), F32),
        compiler_params=_cparams(1),
        name="adaln_mod",
    )(c_all, w_ada, b_ada)


def _s5prep_kernel(lr_ref, li_ref, ldt_ref, bre_ref, bim_ref, cre_ref, cim_ref,
                   are_ref, aim_ref, wbre_ref, wbim_ref, wc_ref):
    dt = jnp.exp(ldt_ref[...])
    lr, li = lr_ref[...], li_ref[...]
    mag = jnp.exp(lr * dt)
    ab_re, ab_im = mag * jnp.cos(li * dt), mag * jnp.sin(li * dt)
    nr, ni = ab_re - 1.0, ab_im
    den = lr * lr + li * li
    cf_re = (nr * lr + ni * li) / den
    cf_im = (ni * lr - nr * li) / den
    are_ref[...] = ab_re
    aim_ref[...] = ab_im
    r = lax.broadcasted_iota(jnp.int32, (S5_TILE_IN, S5_TILE_ST), 0)
    c = lax.broadcasted_iota(jnp.int32, (S5_TILE_IN, S5_TILE_ST), 1)
    on_b = _shr(r, GROUP_A) == _shr(c, P_STATE)
    bre, bim = bre_ref[...], bim_ref[...]
    wbre_ref[...] = jnp.where(on_b, cf_re * bre - cf_im * bim, 0.0).astype(BF16)
    wbim_ref[...] = jnp.where(on_b, cf_re * bim + cf_im * bre, 0.0).astype(BF16)
    r = lax.broadcasted_iota(jnp.int32, (S5_TILE_ST, S5_TILE_IN), 0)
    c = lax.broadcasted_iota(jnp.int32, (S5_TILE_ST, S5_TILE_IN), 1)
    on_c = _shr(r, P_STATE) == _shr(c, GROUP_A)
    wc_ref[0:S5_TILE_ST, :] = jnp.where(on_c, cre_ref[...], 0.0).astype(BF16)
    wc_ref[S5_TILE_ST:2 * S5_TILE_ST, :] = jnp.where(on_c, -cim_ref[...], 0.0).astype(BF16)


def _s5prep_call(lam_re, lam_im, log_dt, b_re, b_im, c_re, c_im):
    row = lambda a: a.reshape(1, N_STATE)
    ldt = jnp.broadcast_to(log_dt[:, None], (N_GROUPS_A, P_STATE))
    bt = lambda b: jnp.tile(b.transpose(2, 0, 1).reshape(GROUP_A, N_STATE), (S5_TILE_GROUPS, 1))
    ct = lambda c: jnp.tile(c.transpose(0, 2, 1).reshape(N_STATE, GROUP_A), (1, S5_TILE_GROUPS))
    rspec = pl.BlockSpec((1, S5_TILE_ST), lambda i: (0, i))
    bspec = pl.BlockSpec((S5_TILE_IN, S5_TILE_ST), lambda i: (0, i))
    cspec = pl.BlockSpec((S5_TILE_ST, S5_TILE_IN), lambda i: (i, 0))
    return pl.pallas_call(
        _s5prep_kernel,
        grid=(S5_TILES,),
        in_specs=[rspec, rspec, rspec, bspec, bspec, cspec, cspec],
        out_specs=[rspec, rspec, bspec, bspec,
                   pl.BlockSpec((None, 2 * S5_TILE_ST, S5_TILE_IN), lambda i: (i, 0, 0))],
        out_shape=[jax.ShapeDtypeStruct((1, N_STATE), F32), jax.ShapeDtypeStruct((1, N_STATE), F32),
                   jax.ShapeDtypeStruct((S5_TILE_IN, N_STATE), BF16),
                   jax.ShapeDtypeStruct((S5_TILE_IN, N_STATE), BF16),
                   jax.ShapeDtypeStruct((S5_TILES, 2 * S5_TILE_ST, S5_TILE_IN), BF16)],
        compiler_params=_cparams(1),
        name="s5_prep",
    )(row(lam_re), row(lam_im), row(ldt), bt(b_re), bt(b_im), ct(c_re), ct(c_im))


def _inproj_kernel(x_ref, shift_ref, scale_ref, gain_ref, w_ref, wt_ref, p_ref, pt_ref, h_scr,
                   *, tm, rc, rows_per_seq, seq_base):
    mt, n = pl.program_id(0), pl.program_id(1)

    @pl.when(n == 0)
    def _():
        gain = gain_ref[...]

        def body(i, carry):
            r0 = pl.multiple_of(i * rc, rc)
            row0 = mt * tm + r0
            xx = x_ref[pl.ds(r0, rc), :]
            y = xx * lax.rsqrt(jnp.mean(xx * xx, axis=-1, keepdims=True) + EPS) * gain
            sc = _seq_rows(scale_ref, row0, rc, rows_per_seq, seq_base)
            sh = _seq_rows(shift_ref, row0, rc, rows_per_seq, seq_base)
            h_scr[pl.ds(r0, rc), :] = (y * (1.0 + sc) + sh).astype(BF16)
            return carry

        lax.fori_loop(0, tm // rc, body, 0)
        pt_ref[...] = _dot(h_scr[...], wt_ref[...])

    p_ref[...] = _dot(h_scr[...], w_ref[...].astype(BF16))


def _inproj_call(x2d, mod, gain, w_in, w_tail, *, rows_per_seq, seq_base, tm, tn):
    tokens = x2d.shape[0]
    ns = mod.shape[0]
    rc = 128 if rows_per_seq >= 128 else 2 * rows_per_seq
    kern = functools.partial(_inproj_kernel, tm=tm, rc=rc, rows_per_seq=rows_per_seq, seq_base=seq_base)
    x_buffers = 1 if tokens // tm > 1 and tm * D_MODEL * 4 > 8 * 1024 * 1024 else 2
    return pl.pallas_call(
        kern,
        grid=(tokens // tm, PROJ_COLS // tn),
        in_specs=[pl.BlockSpec((tm, D_MODEL), lambda m, n: (m, 0), pipeline_mode=pl.Buffered(x_buffers)),
                  pl.BlockSpec((ns, D_MODEL), lambda m, n: (0, 0), pipeline_mode=pl.Buffered(1)),
                  pl.BlockSpec((ns, D_MODEL), lambda m, n: (0, 1), pipeline_mode=pl.Buffered(1)),
                  _const_spec((1, D_MODEL), 2),
                  pl.BlockSpec((D_MODEL, tn), lambda m, n: (0, n)),
                  _const_spec((D_MODEL, LANES), 2)],
        out_specs=[pl.BlockSpec((tm, tn), lambda m, n: (m, n)),
                   pl.BlockSpec((tm, LANES), lambda m, n: (m, 0))],
        out_shape=[jax.ShapeDtypeStruct((tokens, PROJ_COLS), F32),
                   jax.ShapeDtypeStruct((tokens, LANES), F32)],
        scratch_shapes=[pltpu.VMEM((tm, D_MODEL), BF16)],
        compiler_params=_cparams(2),
        name="in_proj",
    )(x2d, mod, mod, gain, w_in, w_tail)


def _s5_kernel(*refs, tc, ls, has_h0):
    n_in = 16 if has_h0 else 14
    if has_h0:
        (u_ref, za_ref, ma0_ref, ma1_ref, ma2_ref, h0re_ref, h0im_ref, wbre_ref, wbim_ref, wc_ref,
         are_ref, aim_ref, dskip_ref, wglu_ref, bglu_ref, waout_ref) = refs[:n_in]
    else:
        (u_ref, za_ref, ma0_ref, ma1_ref, ma2_ref, wbre_ref, wbim_ref, wc_ref,
         are_ref, aim_ref, dskip_ref, wglu_ref, bglu_ref, waout_ref) = refs[:n_in]
    apart_ref, fre_ref, fim_ref = refs[n_in:n_in + 3]
    xr = refs[n_in + 3:n_in + 3 + S5_TILES]
    xi = refs[n_in + 3 + S5_TILES:n_in + 3 + 2 * S5_TILES]
    hc_re, hc_im = refs[n_in + 3 + 2 * S5_TILES:]
    ct = pl.program_id(1)
    n_seq = tc // ls
    u = u_ref[...]
    ub = u.astype(BF16)

    for i in range(S5_TILES):
        ui = ub[:, i * S5_TILE_IN:(i + 1) * S5_TILE_IN]
        for w_ref, x_scr in ((wbre_ref, xr[i]), (wbim_ref, xi[i])):
            res = _dot(ui, w_ref[:, i * S5_TILE_ST:(i + 1) * S5_TILE_ST])
            for j in range(SUBLANES):
                x_scr[pl.ds(j, tc, stride=SUBLANES), :] = res[:, j * LANES:(j + 1) * LANES]

    a_re = [are_ref[i] for i in range(S5_TILES)]
    a_im = [aim_ref[i] for i in range(S5_TILES)]
    for s in range(n_seq):
        if has_h0:
            h_re = [h0re_ref[s, i] for i in range(S5_TILES)]
            h_im = [h0im_ref[s, i] for i in range(S5_TILES)]
        else:
            h_re = [jnp.where(ct == 0, 0.0, hc_re[i]) for i in range(S5_TILES)]
            h_im = [jnp.where(ct == 0, 0.0, hc_im[i]) for i in range(S5_TILES)]
        for tl in range(ls):
            r0 = (s * ls + tl) * SUBLANES
            for i in range(S5_TILES):
                nr = a_re[i] * h_re[i] - a_im[i] * h_im[i] + xr[i][r0:r0 + SUBLANES, :]
                ni = a_re[i] * h_im[i] + a_im[i] * h_re[i] + xi[i][r0:r0 + SUBLANES, :]
                xr[i][r0:r0 + SUBLANES, :] = nr
                xi[i][r0:r0 + SUBLANES, :] = ni
                h_re[i], h_im[i] = nr, ni
        for i in range(S5_TILES):
            if has_h0:
                fre_ref[s, i] = h_re[i]
                fim_ref[s, i] = h_im[i]
            else:
                hc_re[i] = h_re[i]
                hc_im[i] = h_im[i]
                fre_ref[0, i] = h_re[i]
                fim_ref[0, i] = h_im[i]

    ys = []
    for i in range(S5_TILES):
        parts = [x_scr[pl.ds(j, tc, stride=SUBLANES), :] for x_scr in (xr[i], xi[i]) for j in range(SUBLANES)]
        ys.append(_dot(jnp.concatenate(parts, axis=1).astype(BF16), wc_ref[i]))
    y = jnp.concatenate(ys, axis=1) + dskip_ref[...] * u
    g = jax.nn.gelu(y)
    a_y = g * jax.nn.sigmoid(_dot(g.astype(BF16), wglu_ref[...]) + bglu_ref[...])
    gated = (a_y * jax.nn.silu(za_ref[...])).astype(BF16)
    apart_ref[...] = _merge_gate(ma0_ref, ma1_ref, ma2_ref) * _dot(gated, waout_ref[...])


def _s5_call(proj, prep, d_skip, w_glu, b_glu, w_a_out, h0, *, n_batch, seq_len, tc):
    a_re, a_im, wb_re, wb_im, wc = prep
    tokens = proj.shape[0]
    has_h0 = h0 is not None
    if has_h0:
        ls = seq_len
        grid = (tokens // tc, 1)
        tok = lambda b, c: b
        seq_block = tc // ls
    else:
        ls = tc
        nct = seq_len // tc
        grid = (n_batch, nct)
        tok = lambda b, c: b * nct + c
        seq_block = 1
    const = lambda shape: _const_spec(shape, 2)
    st_spec = pl.BlockSpec((seq_block, S5_TILES, SUBLANES, LANES), lambda b, c: (b, 0, 0, 0))
    in_specs = [pl.BlockSpec((tc, W_A), lambda b, c: (tok(b, c), COL_UA // W_A)),
                pl.BlockSpec((tc, W_A), lambda b, c: (tok(b, c), COL_ZA // W_A)),
                pl.BlockSpec((tc, W_A), lambda b, c: (tok(b, c), COL_MA // W_A)),
                pl.BlockSpec((tc, W_A), lambda b, c: (tok(b, c), COL_MA // W_A + 1)),
                pl.BlockSpec((tc, LANES), lambda b, c: (tok(b, c), COL_MB // LANES))]
    args = [proj] * 5
    if has_h0:
        in_specs += [st_spec, st_spec]
        args += [h0[0], h0[1]]
    in_specs += [const((S5_TILE_IN, N_STATE)), const((S5_TILE_IN, N_STATE)),
                 const((S5_TILES, 2 * S5_TILE_ST, S5_TILE_IN)),
                 const((S5_TILES, SUBLANES, LANES)), const((S5_TILES, SUBLANES, LANES)),
                 const((1, W_A)), const((W_A, W_A)), const((1, W_A)), const((W_A, D_MODEL))]
    args += [wb_re, wb_im, wc, a_re, a_im, d_skip, w_glu, b_glu, w_a_out]
    n_seq_total = tokens // seq_len
    st_shape = jax.ShapeDtypeStruct((n_seq_total, S5_TILES, SUBLANES, LANES), F32)
    return pl.pallas_call(
        functools.partial(_s5_kernel, tc=tc, ls=ls, has_h0=has_h0),
        grid=grid,
        in_specs=in_specs,
        out_specs=[pl.BlockSpec((tc, D_MODEL), lambda b, c: (tok(b, c), 0)), st_spec, st_spec],
        out_shape=[jax.ShapeDtypeStruct((tokens, D_MODEL), F32), st_shape, st_shape],
        scratch_shapes=([pltpu.VMEM((tc * SUBLANES, LANES), F32)] * (2 * S5_TILES)
                        + [pltpu.VMEM((S5_TILES, SUBLANES, LANES), F32)] * 2),
        compiler_params=_cparams(2),
        name="s5_branch",
    )(*args)


def _gla_levels(t, ls):
    ms = []
    m = ls
    while m >= 1:
        ms.append(m)
        m //= 2
    idx = np.arange(t)
    mats = []
    for m in ms:
        if m == 1:
            continue
        same = (idx[:, None] // m) == (idx[None, :] // m)
        mats.append(same & (idx[None, :] <= idx[:, None]))
        mats.append(same)
    return ms, np.concatenate(mats, axis=0).astype(np.float32)


def _gla_kernel(*refs, t, n_sub, ls, has_s0, ms):
    if has_s0:
        (q_ref, k_ref, v_ref, zb_ref, mb0_ref, mb1_ref, mb2_ref, glr_ref, apart_ref, s0_ref, lv_ref,
         wg_ref, bg_ref, gg_ref, wbout_ref, merged_ref, sfin_ref, s_scr) = refs
        assert n_sub == 1
    else:
        (q_ref, k_ref, v_ref, zb_ref, mb0_ref, mb1_ref, mb2_ref, glr_ref, apart_ref, lv_ref,
         wg_ref, bg_ref, gg_ref, wbout_ref, merged_ref, sfin_ref, s_scr) = refs
        assert ls == t
    ct = pl.program_id(1)
    n_seq = t // ls

    ti = lax.broadcasted_iota(jnp.int32, (t, t), 0)
    si = lax.broadcasted_iota(jnp.int32, (t, t), 1)
    masks = {}
    for li in range(1, len(ms)):
        m = ms[li]
        masks[li] = ((_shr(ti, 2 * m) == _shr(si, 2 * m)) & ((_shr(ti, m) & 1) == 1) & ((_shr(si, m) & 1) == 0))
    rid = lax.broadcasted_iota(jnp.int32, (t, 1), 0)

    if not has_s0:
        states = [jnp.where(ct == 0, 0.0, s_scr[h]) for h in range(N_HEADS_B)]
    ga = jax.nn.log_sigmoid(_dot(glr_ref[...].astype(BF16), wg_ref[...]) + bg_ref[...]) * (1.0 / GATE_TAU)
    ga_b = ga.astype(BF16)
    ga_r = ga_b.astype(F32)
    e_all = _dot(lv_ref[...], jnp.concatenate([ga_b[c * t:(c + 1) * t] for c in range(n_sub)], axis=1))
    gated_rows = []
    for c in range(n_sub):
        rows = slice(c * t, (c + 1) * t)
        cols = slice(c * QK_W, (c + 1) * QK_W)
        cum = lambda li: e_all[2 * li * t:(2 * li + 1) * t, cols]
        tot = lambda li: e_all[(2 * li + 1) * t:(2 * li + 2) * t, cols]

        q, k, v = q_ref[rows, :], k_ref[rows, :], v_ref[rows, :]
        heads = []
        for h in range(N_HEADS_B):
            hs = slice(h * DK_B, (h + 1) * DK_B)
            qs = q[:, hs] * (DK_B ** -0.5)
            kh = k[:, hs]
            vh = v[:, h * DV_B:(h + 1) * DV_B]
            vb = vh.astype(BF16)
            b = cum(0)[:, hs]
            bend = tot(0)[:, hs]
            q_dec = (qs * jnp.exp(b)).astype(BF16)
            k_dec = kh * jnp.exp(bend - b)
            scores = jnp.zeros((t, t), F32)
            for li in range(1, len(ms)):
                if ms[li] == 1:
                    qm = (qs * jnp.exp(ga_r[rows, hs])).astype(BF16)
                    km = kh.astype(BF16)
                else:
                    eq = cum(li)[:, hs]
                    qm = (qs * jnp.exp(eq)).astype(BF16)
                    km = (kh * jnp.exp(tot(li)[:, hs] - eq)).astype(BF16)
                scores = jnp.where(masks[li], _dot_nt(qm, km), scores)
            o = _dot(scores.astype(BF16), vb) + jnp.sum(qs * kh, axis=-1, keepdims=True) * vh
            o_inter = []
            for i in range(n_seq):
                r0 = i * ls
                s_old = s0_ref[i, h] if has_s0 else states[h]
                o_inter.append(_dot(q_dec[r0:r0 + ls], s_old.astype(BF16)))
                kd = k_dec if n_seq == 1 else jnp.where((rid >= r0) & (rid < r0 + ls), k_dec, 0.0)
                d_end = jnp.exp(bend[r0:r0 + 1, :])
                d_col = jnp.broadcast_to(d_end, (DK_B, DK_B)).T
                s_new = jnp.concatenate([d_col] * (DV_B // DK_B), axis=1) * s_old + _dot_tn(kd.astype(BF16), vb)
                if has_s0:
                    sfin_ref[i, h] = s_new
                else:
                    states[h] = s_new
            o = o + (o_inter[0] if n_seq == 1 else jnp.concatenate(o_inter, axis=0))
            heads.append(o * lax.rsqrt(jnp.mean(o * o, axis=-1, keepdims=True) + EPS))
        b_y = jnp.concatenate(heads, axis=1) * gg_ref[...]
        gated_rows.append((b_y * jax.nn.silu(zb_ref[rows, :])).astype(BF16))
    if not has_s0:
        for h in range(N_HEADS_B):
            s_scr[h] = states[h]
            sfin_ref[0, h] = states[h]
    gated = gated_rows[0] if n_sub == 1 else jnp.concatenate(gated_rows, axis=0)
    merged_ref[...] = apart_ref[...] + _merge_gate(mb0_ref, mb1_ref, mb2_ref) * _dot(gated, wbout_ref[...])


def _gla_call(proj, proj_tail, a_part, w_gate, b_gate, g_gain, w_b_out, s0, *, n_batch, seq_len, t, n_sub):
    tokens = proj.shape[0]
    has_s0 = s0 is not None
    tt = t * n_sub
    if has_s0:
        ls = seq_len
        grid = (tokens // tt, 1)
        tok = lambda b, c: b
        seq_block = tt // ls
    else:
        ls = t
        nct = seq_len // tt
        grid = (n_batch, nct)
        tok = lambda b, c: b * nct + c
        seq_block = 1
    ms, lv = _gla_levels(t, ls)
    lv = jnp.asarray(lv, dtype=BF16)
    const = lambda shape: _const_spec(shape, 2)
    st_spec = pl.BlockSpec((seq_block, N_HEADS_B, DK_B, DV_B), lambda b, c: (b, 0, 0, 0))
    in_specs = [pl.BlockSpec((tt, QK_W), lambda b, c: (tok(b, c), COL_Q // QK_W)),
                pl.BlockSpec((tt, QK_W), lambda b, c: (tok(b, c), COL_K // QK_W)),
                pl.BlockSpec((tt, W_B), lambda b, c: (tok(b, c), COL_V // W_B)),
                pl.BlockSpec((tt, W_B), lambda b, c: (tok(b, c), COL_ZB // W_B)),
                pl.BlockSpec((tt, W_B), lambda b, c: (tok(b, c), COL_MB // W_B)),
                pl.BlockSpec((tt, W_B), lambda b, c: (tok(b, c), COL_MB // W_B + 1)),
                pl.BlockSpec((tt, LANES), lambda b, c: (tok(b, c), 0)),
                pl.BlockSpec((tt, LANES), lambda b, c: (tok(b, c), COL_G // LANES)),
                pl.BlockSpec((tt, D_MODEL), lambda b, c: (tok(b, c), 0))]
    args = [proj, proj, proj, proj, proj, proj, proj_tail, proj, a_part]
    if has_s0:
        in_specs.append(st_spec)
        args.append(s0)
    in_specs += [const(lv.shape), const((LANES, QK_W)), const((1, QK_W)), const((1, W_B)),
                 const((W_B, D_MODEL))]
    args += [lv, w_gate, b_gate, g_gain, w_b_out]
    n_seq_total = tokens // seq_len
    return pl.pallas_call(
        functools.partial(_gla_kernel, t=t, n_sub=n_sub, ls=ls, has_s0=has_s0, ms=tuple(ms)),
        grid=grid,
        in_specs=in_specs,
        out_specs=[pl.BlockSpec((tt, D_MODEL), lambda b, c: (tok(b, c), 0)), st_spec],
        out_shape=[jax.ShapeDtypeStruct((tokens, D_MODEL), F32),
                   jax.ShapeDtypeStruct((n_seq_total, N_HEADS_B, DK_B, DV_B), F32)],
        scratch_shapes=[pltpu.VMEM((N_HEADS_B, DK_B, DV_B), F32)],
        compiler_params=_cparams(2),
        name="gla_branch",
    )(*args)


def _out_kernel(m_ref, x_ref, gate_ref, w_ref, fg_ref, y_ref, acc, *, tm, rc, rows_per_seq, seq_base):
    mt = pl.program_id(0)
    acc[...] = _dot(m_ref[...].astype(BF16), w_ref[...])
    fg = fg_ref[...]

    def body(i, carry):
        r0 = pl.multiple_of(i * rc, rc)
        gate = _seq_rows(gate_ref, mt * tm + r0, rc, rows_per_seq, seq_base)
        yy = x_ref[pl.ds(r0, rc), :] + gate * acc[pl.ds(r0, rc), :]
        y_ref[pl.ds(r0, rc), :] = yy * lax.rsqrt(jnp.mean(yy * yy, axis=-1, keepdims=True) + EPS) * fg
        return carry

    lax.fori_loop(0, tm // rc, body, 0)


def _out_call(merged, x2d, mod, w_out, fgain, *, rows_per_seq, seq_base, tm):
    tokens = x2d.shape[0]
    ns = mod.shape[0]
    rc = 128 if rows_per_seq >= 128 else 2 * rows_per_seq
    kern = functools.partial(_out_kernel, tm=tm, rc=rc, rows_per_seq=rows_per_seq, seq_base=seq_base)
    return pl.pallas_call(
        kern,
        grid=(tokens // tm,),
        in_specs=[pl.BlockSpec((tm, D_MODEL), lambda m: (m, 0)),
                  pl.BlockSpec((tm, D_MODEL), lambda m: (m, 0)),
                  pl.BlockSpec((ns, D_MODEL), lambda m: (0, 2), pipeline_mode=pl.Buffered(1)),
                  _const_spec((D_MODEL, D_MODEL), 1),
                  _const_spec((1, D_MODEL), 1)],
        out_specs=pl.BlockSpec((tm, D_MODEL), lambda m: (m, 0)),
        out_shape=jax.ShapeDtypeStruct((tokens, D_MODEL), F32),
        scratch_shapes=[pltpu.VMEM((tm, D_MODEL), F32)],
        compiler_params=_cparams(1),
        name="out_proj",
    )(merged, x2d, mod, w_out, fgain)


def _tile(n, target):
    t = min(n, target)
    assert n % t == 0
    return t


def _layer(x, n_seq_before, mod, weights, prep, s0_ssm, s0_gla):
    (gain, w_in, w_tail, d_skip, w_glu, b_glu, w_a_out, w_gate, b_gate, g_gain, w_b_out, w_out, fgain) = weights
    n_batch, seq_len, _ = x.shape
    tokens = n_batch * seq_len
    x2d = x.reshape(tokens, D_MODEL)
    long_seq = s0_ssm is None
    span = seq_len if long_seq else tokens
    proj, proj_tail = _inproj_call(x2d, mod, gain, w_in, w_tail, rows_per_seq=seq_len, seq_base=n_seq_before,
                                   tm=_tile(span, 2048), tn=512)
    a_part, f_re, f_im = _s5_call(proj, prep, d_skip, w_glu, b_glu, w_a_out, s0_ssm,
                                  n_batch=n_batch, seq_len=seq_len, tc=_tile(span, 256))
    if long_seq:
        t = _tile(seq_len, 128)
        n_sub = 2 if seq_len % (2 * t) == 0 else 1
    else:
        t, n_sub = _tile(tokens, 64), 1
    merged, s_fin = _gla_call(proj, proj_tail, a_part, w_gate, b_gate, g_gain, w_b_out, s0_gla,
                              n_batch=n_batch, seq_len=seq_len, t=t, n_sub=n_sub)
    y = _out_call(merged, x2d, mod, w_out, fgain, rows_per_seq=seq_len, seq_base=n_seq_before,
                  tm=_tile(span, 512))
    st = lambda f: f.reshape(1, n_batch, N_GROUPS_A, P_STATE)
    return (y.reshape(n_batch, seq_len, D_MODEL), st(f_re), st(f_im),
            s_fin.reshape(1, n_batch, N_HEADS_B, DK_B, DV_B))


def kernel(x_prompt, x_sample, c_prompt, c_sample, state_ssm_re, state_ssm_im, state_gla, w_ada, b_ada, norm_gain, w_in, lambda_re, lambda_im, log_dt, ssm_b_re, ssm_b_im, ssm_c_re, ssm_c_im, d_skip, w_glu, b_glu, w_gate_up, b_gate, gla_norm_gain, w_a_out, w_b_out, w_out, final_norm_gain):
    assert w_ada.shape[0] == 1, "single-layer step"
    assert w_in.shape[2] == IN_COLS
    n_prompt, n_sample = x_prompt.shape[0], x_sample.shape[0]

    mod = _mod_call(jnp.concatenate([c_prompt, c_sample], axis=0), w_ada[0], b_ada)
    a_re, a_im, wb_re, wb_im, wc = _s5prep_call(lambda_re[0], lambda_im[0], log_dt[0], ssm_b_re[0], ssm_b_im[0],
                                                ssm_c_re[0], ssm_c_im[0])
    state_tiles = lambda a: a.reshape(-1, S5_TILES, SUBLANES, LANES)
    prep = (state_tiles(a_re)[0], state_tiles(a_im)[0], wb_re, wb_im, wc)

    w_in2d = w_in.reshape(D_MODEL, IN_COLS)
    w_tail = jnp.pad(w_in2d[:, PROJ_COLS:], ((0, 0), (0, LANES - GATE_OFF))).astype(BF16)
    w_gate = jnp.pad(w_gate_up[0], ((0, LANES - GATE_RANK), (0, 0))).astype(BF16)
    weights = (norm_gain, w_in2d, w_tail, d_skip, w_glu[0].astype(BF16), b_glu, w_a_out[0].astype(BF16),
               w_gate, b_gate, gla_norm_gain, w_b_out[0].astype(BF16), w_out[0].astype(BF16),
               final_norm_gain.reshape(1, D_MODEL))

    y_p, pre, pim, pgla = _layer(x_prompt, 0, mod, weights, prep, None, None)
    y_s, sre, sim, sgla = _layer(x_sample, n_prompt, mod, weights, prep,
                                 (state_tiles(state_ssm_re[0]), state_tiles(state_ssm_im[0])), state_gla[0])
    return (y_p, y_s, pre, pim, pgla, sre, sim, sgla)
```

```python
import functools

import numpy as np
import jax
import jax.numpy as jnp
from jax import lax
from jax.experimental import pallas as pl
from jax.experimental.pallas import tpu as pltpu

F32 = jnp.float32
BF16 = jnp.bfloat16

D_MODEL = 2048
W_A = D_MODEL // 2
GROUP_A = 16
N_GROUPS_A = W_A // GROUP_A
P_STATE = 64
N_STATE = N_GROUPS_A * P_STATE
W_B = D_MODEL // 2
N_HEADS_B = 4
DK_B = W_B // 2 // N_HEADS_B
DV_B = W_B // N_HEADS_B
QK_W = N_HEADS_B * DK_B
GATE_RANK = 16
GATE_TAU = 16.0
EPS = 1e-6

LANES = 128
SUBLANES = 8
V7X_VMEM_BYTES = 64 * 1024 * 1024
VMEM_LIMIT_BYTES = 56 * 1024 * 1024

S5_TILE_GROUPS = 16
S5_TILES = N_GROUPS_A // S5_TILE_GROUPS
S5_TILE_IN = S5_TILE_GROUPS * GROUP_A
S5_TILE_ST = S5_TILE_GROUPS * P_STATE
assert S5_TILE_ST == SUBLANES * LANES

COL_UA = 0
COL_ZA = COL_UA + W_A
COL_Q = COL_ZA + W_A
COL_K = COL_Q + QK_W
COL_V = COL_K + QK_W
COL_ZB = COL_V + W_B
COL_G = COL_ZB + W_B
COL_MA = COL_G + GATE_RANK
COL_MB = COL_MA + D_MODEL
IN_COLS = COL_MB + D_MODEL
PROJ_COLS = (IN_COLS // LANES) * LANES
GATE_OFF = COL_MA % LANES
assert COL_MB % LANES == GATE_OFF and IN_COLS - PROJ_COLS == GATE_OFF


def _dot(a, b):
    return jnp.dot(a, b, preferred_element_type=F32)


def _dot_nt(a, b):
    return lax.dot_general(a, b, (((1,), (1,)), ((), ())), preferred_element_type=F32)


def _dot_tn(a, b):
    return lax.dot_general(a, b, (((0,), (0,)), ((), ())), preferred_element_type=F32)


def _shr(x, pow2):
    sh = int(pow2).bit_length() - 1
    assert 1 << sh == pow2
    return jnp.right_shift(x, sh)


def _cparams(n_axes):
    return pltpu.CompilerParams(dimension_semantics=("arbitrary",) * n_axes,
                                vmem_limit_bytes=VMEM_LIMIT_BYTES)


def _const_spec(shape, n_axes):
    zeros = (0,) * len(shape)
    index_map = (lambda a: zeros) if n_axes == 1 else (lambda a, b: zeros)
    return pl.BlockSpec(shape, index_map, pipeline_mode=pl.Buffered(1))


def _seq_rows(ref, row0, rc, rows_per_seq, seq_base):
    s = seq_base + lax.div(row0, jnp.int32(rows_per_seq))
    if rows_per_seq >= rc:
        return ref[pl.ds(s, 1), :]
    assert rc == 2 * rows_per_seq
    rid = lax.broadcasted_iota(jnp.int32, (rc, 1), 0)
    return jnp.where(rid < rows_per_seq, ref[pl.ds(s, 1), :], ref[pl.ds(s + 1, 1), :])


def _merge_gate(lo_ref, mid_ref, hi_ref):
    m = jnp.concatenate([lo_ref[...], mid_ref[...], hi_ref[...]], axis=1)
    return jax.nn.sigmoid(m[:, GATE_OFF:GATE_OFF + D_MODEL])


def _mod_kernel(c_ref, w_ref, b_ref, o_ref):
    s = jax.nn.silu(c_ref[...]).astype(BF16)
    o_ref[...] = _dot(s, w_ref[...].astype(BF16)) + b_ref[...]


def _mod_call(c_all, w_ada, b_ada):
    ns = c_all.shape[0]
    tn = 512
    return pl.pallas_call(
        _mod_kernel,
        grid=(3 * D_MODEL // tn,),
        in_specs=[_const_spec((ns, D_MODEL), 1),
                  pl.BlockSpec((D_MODEL, tn), lambda n: (0, n)),
                  pl.BlockSpec((1, tn), lambda n: (0, n))],
        out_specs=pl.BlockSpec((ns, tn), lambda n: (0, n)),
        out_shape=jax.ShapeDtypeStruct((ns, 3 * D_MODEL), F32),
        compiler_params=_cparams(1),
        name="adaln_mod",
    )(c_all, w_ada, b_ada)


def _s5prep_kernel(lr_ref, li_ref, ldt_ref, bre_ref, bim_ref, cre_ref, cim_ref,
                   are_ref, aim_ref, wbre_ref, wbim_ref, wc_ref):
    dt = jnp.exp(ldt_ref[...])
    lr, li = lr_ref[...], li_ref[...]
    mag = jnp.exp(lr * dt)
    ab_re, ab_im = mag * jnp.cos(li * dt), mag * jnp.sin(li * dt)
    nr, ni = ab_re - 1.0, ab_im
    den = lr * lr + li * li
    cf_re = (nr * lr + ni * li) / den
    cf_im = (ni * lr - nr * li) / den
    are_ref[...] = ab_re
    aim_ref[...] = ab_im
    r = lax.broadcasted_iota(jnp.int32, (S5_TILE_IN, S5_TILE_ST), 0)
    c = lax.broadcasted_iota(jnp.int32, (S5_TILE_IN, S5_TILE_ST), 1)
    on_b = _shr(r, GROUP_A) == _shr(c, P_STATE)
    bre, bim = bre_ref[...], bim_ref[...]
    wbre_ref[...] = jnp.where(on_b, cf_re * bre - cf_im * bim, 0.0).astype(BF16)
    wbim_ref[...] = jnp.where(on_b, cf_re * bim + cf_im * bre, 0.0).astype(BF16)
    r = lax.broadcasted_iota(jnp.int32, (S5_TILE_ST, S5_TILE_IN), 0)
    c = lax.broadcasted_iota(jnp.int32, (S5_TILE_ST, S5_TILE_IN), 1)
    on_c = _shr(r, P_STATE) == _shr(c, GROUP_A)
    wc_ref[0:S5_TILE_ST, :] = jnp.where(on_c, cre_ref[...], 0.0).astype(BF16)
    wc_ref[S5_TILE_ST:2 * S5_TILE_ST, :] = jnp.where(on_c, -cim_ref[...], 0.0).astype(BF16)


def _s5prep_call(lam_re, lam_im, log_dt, b_re, b_im, c_re, c_im):
    row = lambda a: a.reshape(1, N_STATE)
    ldt = jnp.broadcast_to(log_dt[:, None], (N_GROUPS_A, P_STATE))
    bt = lambda b: jnp.tile(b.transpose(2, 0, 1).reshape(GROUP_A, N_STATE), (S5_TILE_GROUPS, 1))
    ct = lambda c: jnp.tile(c.transpose(0, 2, 1).reshape(N_STATE, GROUP_A), (1, S5_TILE_GROUPS))
    rspec = pl.BlockSpec((1, S5_TILE_ST), lambda i: (0, i))
    bspec = pl.BlockSpec((S5_TILE_IN, S5_TILE_ST), lambda i: (0, i))
    cspec = pl.BlockSpec((S5_TILE_ST, S5_TILE_IN), lambda i: (i, 0))
    return pl.pallas_call(
        _s5prep_kernel,
        grid=(S5_TILES,),
        in_specs=[rspec, rspec, rspec, bspec, bspec, cspec, cspec],
        out_specs=[rspec, rspec, bspec, bspec,
                   pl.BlockSpec((None, 2 * S5_TILE_ST, S5_TILE_IN), lambda i: (i, 0, 0))],
        out_shape=[jax.ShapeDtypeStruct((1, N_STATE), F32), jax.ShapeDtypeStruct((1, N_STATE), F32),
                   jax.ShapeDtypeStruct((S5_TILE_IN, N_STATE), BF16),
                   jax.ShapeDtypeStruct((S5_TILE_IN, N_STATE), BF16),
                   jax.ShapeDtypeStruct((S5_TILES, 2 * S5_TILE_ST, S5_TILE_IN), BF16)],
        compiler_params=_cparams(1),
        name="s5_prep",
    )(row(lam_re), row(lam_im), row(ldt), bt(b_re), bt(b_im), ct(c_re), ct(c_im))


def _inproj_kernel(x_ref, shift_ref, scale_ref, gain_ref, w_ref, wt_ref, p_ref, pt_ref, h_scr,
                   *, tm, rc, rows_per_seq, seq_base):
    mt, n = pl.program_id(0), pl.program_id(1)

    @pl.when(n == 0)
    def _():
        gain = gain_ref[...]

        def body(i, carry):
            r0 = pl.multiple_of(i * rc, rc)
            row0 = mt * tm + r0
            xx = x_ref[pl.ds(r0, rc), :]
            y = xx * lax.rsqrt(jnp.mean(xx * xx, axis=-1, keepdims=True) + EPS) * gain
            sc = _seq_rows(scale_ref, row0, rc, rows_per_seq, seq_base)
            sh = _seq_rows(shift_ref, row0, rc, rows_per_seq, seq_base)
            h_scr[pl.ds(r0, rc), :] = (y * (1.0 + sc) + sh).astype(BF16)
            return carry

        lax.fori_loop(0, tm // rc, body, 0)
        pt_ref[...] = _dot_nt(h_scr[...], wt_ref[...].astype(BF16))

    p_ref[...] = _dot_nt(h_scr[...], w_ref[...].astype(BF16))


def _inproj_call(x2d, mod, gain, w_in_t, *, rows_per_seq, seq_base, tm, tn):
    tokens = x2d.shape[0]
    ns = mod.shape[0]
    rc = 128 if rows_per_seq >= 128 else 2 * rows_per_seq
    kern = functools.partial(_inproj_kernel, tm=tm, rc=rc, rows_per_seq=rows_per_seq, seq_base=seq_base)
    x_buffers = 1 if tokens // tm > 1 and tm * D_MODEL * 4 > 8 * 1024 * 1024 else 2
    return pl.pallas_call(
        kern,
        grid=(tokens // tm, PROJ_COLS // tn),
        in_specs=[pl.BlockSpec((tm, D_MODEL), lambda m, n: (m, 0), pipeline_mode=pl.Buffered(x_buffers)),
                  pl.BlockSpec((ns, D_MODEL), lambda m, n: (0, 0), pipeline_mode=pl.Buffered(1)),
                  pl.BlockSpec((ns, D_MODEL), lambda m, n: (0, 1), pipeline_mode=pl.Buffered(1)),
                  _const_spec((1, D_MODEL), 2),
                  pl.BlockSpec((tn, D_MODEL), lambda m, n: (n, 0)),
                  pl.BlockSpec((GATE_OFF, D_MODEL), lambda m, n: (PROJ_COLS // GATE_OFF, 0),
                               pipeline_mode=pl.Buffered(1))],
        out_specs=[pl.BlockSpec((tm, tn), lambda m, n: (m, n)),
                   pl.BlockSpec((tm, GATE_OFF), lambda m, n: (m, 0))],
        out_shape=[jax.ShapeDtypeStruct((tokens, PROJ_COLS), F32),
                   jax.ShapeDtypeStruct((tokens, GATE_OFF), F32)],
        scratch_shapes=[pltpu.VMEM((tm, D_MODEL), BF16)],
        compiler_params=_cparams(2),
        name="in_proj",
    )(x2d, mod, mod, gain, w_in_t, w_in_t)


def _s5_kernel(*refs, tc, n_sub, ls, has_h0):
    n_in = 12 if has_h0 else 10
    if has_h0:
        (u_ref, za_ref, h0re_ref, h0im_ref, wbre_ref, wbim_ref, wc_ref,
         are_ref, aim_ref, dskip_ref, wglu_ref, bglu_ref) = refs[:n_in]
    else:
        (u_ref, za_ref, wbre_ref, wbim_ref, wc_ref,
         are_ref, aim_ref, dskip_ref, wglu_ref, bglu_ref) = refs[:n_in]
    gated_ref, fre_ref, fim_ref = refs[n_in:n_in + 3]
    scr = refs[n_in + 3:n_in + 3 + 2 * n_sub * S5_TILES]
    xr = [scr[2 * c * S5_TILES:(2 * c + 1) * S5_TILES] for c in range(n_sub)]
    xi = [scr[(2 * c + 1) * S5_TILES:(2 * c + 2) * S5_TILES] for c in range(n_sub)]
    hc_re, hc_im = refs[n_in + 3 + 2 * n_sub * S5_TILES:]
    ct = pl.program_id(1)
    n_seq = tc // ls

    us = []
    for c in range(n_sub):
        u = u_ref[c * tc:(c + 1) * tc, :]
        us.append(u)
        ub = u.astype(BF16)
        for i in range(S5_TILES):
            ui = ub[:, i * S5_TILE_IN:(i + 1) * S5_TILE_IN]
            for w_ref, x_scr in ((wbre_ref, xr[c][i]), (wbim_ref, xi[c][i])):
                res = _dot(ui, w_ref[:, i * S5_TILE_ST:(i + 1) * S5_TILE_ST])
                for j in range(SUBLANES):
                    x_scr[pl.ds(j, tc, stride=SUBLANES), :] = res[:, j * LANES:(j + 1) * LANES]

    a_re = [are_ref[i] for i in range(S5_TILES)]
    a_im = [aim_ref[i] for i in range(S5_TILES)]
    if not has_h0:
        h_re = [jnp.where(ct == 0, 0.0, hc_re[i]) for i in range(S5_TILES)]
        h_im = [jnp.where(ct == 0, 0.0, hc_im[i]) for i in range(S5_TILES)]
    for c in range(n_sub):
        for s in range(n_seq):
            if has_h0:
                h_re = [h0re_ref[c * n_seq + s, i] for i in range(S5_TILES)]
                h_im = [h0im_ref[c * n_seq + s, i] for i in range(S5_TILES)]
            for tl in range(ls):
                r0 = (s * ls + tl) * SUBLANES
                for i in range(S5_TILES):
                    nr = a_re[i] * h_re[i] - a_im[i] * h_im[i] + xr[c][i][r0:r0 + SUBLANES, :]
                    ni = a_re[i] * h_im[i] + a_im[i] * h_re[i] + xi[c][i][r0:r0 + SUBLANES, :]
                    xr[c][i][r0:r0 + SUBLANES, :] = nr
                    xi[c][i][r0:r0 + SUBLANES, :] = ni
                    h_re[i], h_im[i] = nr, ni
            if has_h0:
                for i in range(S5_TILES):
                    fre_ref[c * n_seq + s, i] = h_re[i]
                    fim_ref[c * n_seq + s, i] = h_im[i]
    if not has_h0:
        for i in range(S5_TILES):
            hc_re[i] = h_re[i]
            hc_im[i] = h_im[i]
            fre_ref[0, i] = h_re[i]
            fim_ref[0, i] = h_im[i]

    for c in range(n_sub):
        ys = []
        for i in range(S5_TILES):
            parts = [x_scr[pl.ds(j, tc, stride=SUBLANES), :]
                     for x_scr in (xr[c][i], xi[c][i]) for j in range(SUBLANES)]
            ys.append(_dot(jnp.concatenate(parts, axis=1).astype(BF16), wc_ref[i]))
        y = jnp.concatenate(ys, axis=1) + dskip_ref[...] * us[c]
        g = jax.nn.gelu(y)
        a_y = g * jax.nn.sigmoid(_dot(g.astype(BF16), wglu_ref[...]) + bglu_ref[...])
        gated_ref[c * tc:(c + 1) * tc, :] = (a_y * jax.nn.silu(za_ref[c * tc:(c + 1) * tc, :])).astype(BF16)


def _s5_call(proj, prep, d_skip, w_glu, b_glu, h0, *, n_batch, seq_len, tc, n_sub):
    a_re, a_im, wb_re, wb_im, wc = prep
    tokens = proj.shape[0]
    has_h0 = h0 is not None
    tt = tc * n_sub
    if has_h0:
        ls = seq_len
        grid = (tokens // tt, 1)
        tok = lambda b, c: b
        seq_block = tt // ls
    else:
        ls = tc
        nct = seq_len // tt
        grid = (n_batch, nct)
        tok = lambda b, c: b * nct + c
        seq_block = 1
    const = lambda shape: _const_spec(shape, 2)
    st_spec = pl.BlockSpec((seq_block, S5_TILES, SUBLANES, LANES), lambda b, c: (b, 0, 0, 0))
    in_specs = [pl.BlockSpec((tt, W_A), lambda b, c: (tok(b, c), COL_UA // W_A)),
                pl.BlockSpec((tt, W_A), lambda b, c: (tok(b, c), COL_ZA // W_A))]
    args = [proj] * 2
    if has_h0:
        in_specs += [st_spec, st_spec]
        args += [h0[0], h0[1]]
    in_specs += [const((S5_TILE_IN, N_STATE)), const((S5_TILE_IN, N_STATE)),
                 const((S5_TILES, 2 * S5_TILE_ST, S5_TILE_IN)),
                 const((S5_TILES, SUBLANES, LANES)), const((S5_TILES, SUBLANES, LANES)),
                 const((1, W_A)), const((W_A, W_A)), const((1, W_A))]
    args += [wb_re, wb_im, wc, a_re, a_im, d_skip, w_glu, b_glu]
    n_seq_total = tokens // seq_len
    st_shape = jax.ShapeDtypeStruct((n_seq_total, S5_TILES, SUBLANES, LANES), F32)
    return pl.pallas_call(
        functools.partial(_s5_kernel, tc=tc, n_sub=n_sub, ls=ls, has_h0=has_h0),
        grid=grid,
        in_specs=in_specs,
        out_specs=[pl.BlockSpec((tt, W_A), lambda b, c: (tok(b, c), 0)), st_spec, st_spec],
        out_shape=[jax.ShapeDtypeStruct((tokens, W_A), BF16), st_shape, st_shape],
        scratch_shapes=([pltpu.VMEM((tc * SUBLANES, LANES), F32)] * (2 * n_sub * S5_TILES)
                        + [pltpu.VMEM((S5_TILES, SUBLANES, LANES), F32)] * 2),
        compiler_params=_cparams(2),
        name="s5_branch",
    )(*args)


def _gla_levels(t, ls):
    ms = []
    m = ls
    while m >= 1:
        ms.append(m)
        m //= 2
    idx = np.arange(t)
    mats = []
    for m in ms:
        if m == 1:
            continue
        same = (idx[:, None] // m) == (idx[None, :] // m)
        mats.append(same & (idx[None, :] <= idx[:, None]))
        mats.append(same)
    return ms, np.concatenate(mats, axis=0).astype(np.float32)


def _gla_kernel(*refs, t, n_sub, ls, has_s0, ms):
    if has_s0:
        (q_ref, k_ref, v_ref, zb_ref, ma0_ref, ma1_ref, ma2_ref, mb0_ref, mb1_ref, mb2_ref, glr_ref, ya_ref,
         s0_ref, lv_ref, wg_ref, bg_ref, gg_ref, waout_ref, wbout_ref, merged_ref, sfin_ref, s_scr) = refs
        assert n_sub == 1
    else:
        (q_ref, k_ref, v_ref, zb_ref, ma0_ref, ma1_ref, ma2_ref, mb0_ref, mb1_ref, mb2_ref, glr_ref, ya_ref,
         lv_ref, wg_ref, bg_ref, gg_ref, waout_ref, wbout_ref, merged_ref, sfin_ref, s_scr) = refs
        assert ls == t
    ct = pl.program_id(1)
    n_seq = t // ls

    ti = lax.broadcasted_iota(jnp.int32, (t, t), 0)
    si = lax.broadcasted_iota(jnp.int32, (t, t), 1)
    masks = {}
    for li in range(1, len(ms)):
        m = ms[li]
        masks[li] = ((_shr(ti, 2 * m) == _shr(si, 2 * m)) & ((_shr(ti, m) & 1) == 1) & ((_shr(si, m) & 1) == 0))
    rid = lax.broadcasted_iota(jnp.int32, (t, 1), 0)

    if not has_s0:
        states = [jnp.where(ct == 0, 0.0, s_scr[h]) for h in range(N_HEADS_B)]
    ga = jax.nn.log_sigmoid(_dot(glr_ref[...].astype(BF16), wg_ref[...]) + bg_ref[...]) * (1.0 / GATE_TAU)
    ga_b = ga.astype(BF16)
    ga_r = ga_b.astype(F32)
    e_all = _dot(lv_ref[...], jnp.concatenate([ga_b[c * t:(c + 1) * t] for c in range(n_sub)], axis=1))
    gated_rows = []
    for c in range(n_sub):
        rows = slice(c * t, (c + 1) * t)
        cols = slice(c * QK_W, (c + 1) * QK_W)
        cum = lambda li: e_all[2 * li * t:(2 * li + 1) * t, cols]
        tot = lambda li: e_all[(2 * li + 1) * t:(2 * li + 2) * t, cols]

        q, k, v = q_ref[rows, :], k_ref[rows, :], v_ref[rows, :]
        heads = []
        for h in range(N_HEADS_B):
            hs = slice(h * DK_B, (h + 1) * DK_B)
            qs = q[:, hs] * (DK_B ** -0.5)
            kh = k[:, hs]
            vh = v[:, h * DV_B:(h + 1) * DV_B]
            vb = vh.astype(BF16)
            b = cum(0)[:, hs]
            bend = tot(0)[:, hs]
            q_dec = (qs * jnp.exp(b)).astype(BF16)
            k_dec = kh * jnp.exp(bend - b)
            scores = jnp.zeros((t, t), F32)
            for li in range(1, len(ms)):
                if ms[li] == 1:
                    qm = (qs * jnp.exp(ga_r[rows, hs])).astype(BF16)
                    km = kh.astype(BF16)
                else:
                    eq = cum(li)[:, hs]
                    qm = (qs * jnp.exp(eq)).astype(BF16)
                    km = (kh * jnp.exp(tot(li)[:, hs] - eq)).astype(BF16)
                scores = jnp.where(masks[li], _dot_nt(qm, km), scores)
            o = _dot(scores.astype(BF16), vb) + jnp.sum(qs * kh, axis=-1, keepdims=True) * vh
            o_inter = []
            for i in range(n_seq):
                r0 = i * ls
                s_old = s0_ref[i, h] if has_s0 else states[h]
                o_inter.append(_dot(q_dec[r0:r0 + ls], s_old.astype(BF16)))
                kd = k_dec if n_seq == 1 else jnp.where((rid >= r0) & (rid < r0 + ls), k_dec, 0.0)
                d_end = jnp.exp(bend[r0:r0 + 1, :])
                d_col = jnp.broadcast_to(d_end, (DK_B, DK_B)).T
                s_new = jnp.concatenate([d_col] * (DV_B // DK_B), axis=1) * s_old + _dot_tn(kd.astype(BF16), vb)
                if has_s0:
                    sfin_ref[i, h] = s_new
                else:
                    states[h] = s_new
            o = o + (o_inter[0] if n_seq == 1 else jnp.concatenate(o_inter, axis=0))
            heads.append(o * lax.rsqrt(jnp.mean(o * o, axis=-1, keepdims=True) + EPS))
        b_y = jnp.concatenate(heads, axis=1) * gg_ref[...]
        gated_rows.append((b_y * jax.nn.silu(zb_ref[rows, :])).astype(BF16))
    if not has_s0:
        for h in range(N_HEADS_B):
            s_scr[h] = states[h]
            sfin_ref[0, h] = states[h]
    gated = gated_rows[0] if n_sub == 1 else jnp.concatenate(gated_rows, axis=0)
    merged = (_merge_gate(ma0_ref, ma1_ref, ma2_ref) * _dot(ya_ref[...], waout_ref[...])
              + _merge_gate(mb0_ref, mb1_ref, mb2_ref) * _dot(gated, wbout_ref[...]))
    merged_ref[...] = merged.astype(BF16)


def _gla_call(proj, proj_tail, y_a, w_gate, b_gate, g_gain, w_a_out, w_b_out, s0, *, n_batch, seq_len, t, n_sub):
    tokens = proj.shape[0]
    has_s0 = s0 is not None
    tt = t * n_sub
    if has_s0:
        ls = seq_len
        grid = (tokens // tt, 1)
        tok = lambda b, c: b
        seq_block = tt // ls
    else:
        ls = t
        nct = seq_len // tt
        grid = (n_batch, nct)
        tok = lambda b, c: b * nct + c
        seq_block = 1
    ms, lv = _gla_levels(t, ls)
    lv = jnp.asarray(lv, dtype=BF16)
    const = lambda shape: _const_spec(shape, 2)
    st_spec = pl.BlockSpec((seq_block, N_HEADS_B, DK_B, DV_B), lambda b, c: (b, 0, 0, 0))
    in_specs = [pl.BlockSpec((tt, QK_W), lambda b, c: (tok(b, c), COL_Q // QK_W)),
                pl.BlockSpec((tt, QK_W), lambda b, c: (tok(b, c), COL_K // QK_W)),
                pl.BlockSpec((tt, W_B), lambda b, c: (tok(b, c), COL_V // W_B)),
                pl.BlockSpec((tt, W_B), lambda b, c: (tok(b, c), COL_ZB // W_B)),
                pl.BlockSpec((tt, W_A), lambda b, c: (tok(b, c), COL_MA // W_A)),
                pl.BlockSpec((tt, W_A), lambda b, c: (tok(b, c), COL_MA // W_A + 1)),
                pl.BlockSpec((tt, LANES), lambda b, c: (tok(b, c), COL_MB // LANES)),
                pl.BlockSpec((tt, W_B), lambda b, c: (tok(b, c), COL_MB // W_B)),
                pl.BlockSpec((tt, W_B), lambda b, c: (tok(b, c), COL_MB // W_B + 1)),
                pl.BlockSpec((tt, GATE_OFF), lambda b, c: (tok(b, c), 0)),
                pl.BlockSpec((tt, LANES), lambda b, c: (tok(b, c), COL_G // LANES)),
                pl.BlockSpec((tt, W_A), lambda b, c: (tok(b, c), 0))]
    args = [proj, proj, proj, proj, proj, proj, proj, proj, proj, proj_tail, proj, y_a]
    if has_s0:
        in_specs.append(st_spec)
        args.append(s0)
    in_specs += [const(lv.shape), const((LANES, QK_W)), const((1, QK_W)), const((1, W_B)),
                 const((W_A, D_MODEL)), const((W_B, D_MODEL))]
    args += [lv, w_gate, b_gate, g_gain, w_a_out, w_b_out]
    n_seq_total = tokens // seq_len
    return pl.pallas_call(
        functools.partial(_gla_kernel, t=t, n_sub=n_sub, ls=ls, has_s0=has_s0, ms=tuple(ms)),
        grid=grid,
        in_specs=in_specs,
        out_specs=[pl.BlockSpec((tt, D_MODEL), lambda b, c: (tok(b, c), 0)), st_spec],
        out_shape=[jax.ShapeDtypeStruct((tokens, D_MODEL), BF16),
                   jax.ShapeDtypeStruct((n_seq_total, N_HEADS_B, DK_B, DV_B), F32)],
        scratch_shapes=[pltpu.VMEM((N_HEADS_B, DK_B, DV_B), F32)],
        compiler_params=_cparams(2),
        name="gla_branch",
    )(*args)


def _out_kernel(m_ref, x_ref, gate_ref, w_ref, fg_ref, y_ref, acc, *, tm, rc, rows_per_seq, seq_base):
    mt = pl.program_id(0)
    acc[...] = _dot(m_ref[...].astype(BF16), w_ref[...])
    fg = fg_ref[...]

    def body(i, carry):
        r0 = pl.multiple_of(i * rc, rc)
        gate = _seq_rows(gate_ref, mt * tm + r0, rc, rows_per_seq, seq_base)
        yy = x_ref[pl.ds(r0, rc), :] + gate * acc[pl.ds(r0, rc), :]
        y_ref[pl.ds(r0, rc), :] = yy * lax.rsqrt(jnp.mean(yy * yy, axis=-1, keepdims=True) + EPS) * fg
        return carry

    lax.fori_loop(0, tm // rc, body, 0)


def _out_call(merged, x2d, mod, w_out, fgain, *, rows_per_seq, seq_base, tm):
    tokens = x2d.shape[0]
    ns = mod.shape[0]
    rc = 128 if rows_per_seq >= 128 else 2 * rows_per_seq
    kern = functools.partial(_out_kernel, tm=tm, rc=rc, rows_per_seq=rows_per_seq, seq_base=seq_base)
    return pl.pallas_call(
        kern,
        grid=(tokens // tm,),
        in_specs=[pl.BlockSpec((tm, D_MODEL), lambda m: (m, 0)),
                  pl.BlockSpec((tm, D_MODEL), lambda m: (m, 0)),
                  pl.BlockSpec((ns, D_MODEL), lambda m: (0, 2), pipeline_mode=pl.Buffered(1)),
                  _const_spec((D_MODEL, D_MODEL), 1),
                  _const_spec((1, D_MODEL), 1)],
        out_specs=pl.BlockSpec((tm, D_MODEL), lambda m: (m, 0)),
        out_shape=jax.ShapeDtypeStruct((tokens, D_MODEL), F32),
        scratch_shapes=[pltpu.VMEM((tm, D_MODEL), F32)],
        compiler_params=_cparams(1),
        name="out_proj",
    )(merged, x2d, mod, w_out, fgain)


def _tile(n, target):
    t = min(n, target)
    assert n % t == 0
    return t


def _layer(x, n_seq_before, mod, weights, prep, s0_ssm, s0_gla):
    (gain, w_in_t, d_skip, w_glu, b_glu, w_a_out, w_gate, b_gate, g_gain, w_b_out, w_out, fgain) = weights
    n_batch, seq_len, _ = x.shape
    tokens = n_batch * seq_len
    x2d = x.reshape(tokens, D_MODEL)
    long_seq = s0_ssm is None
    span = seq_len if long_seq else tokens
    proj, proj_tail = _inproj_call(x2d, mod, gain, w_in_t, rows_per_seq=seq_len, seq_base=n_seq_before,
                                   tm=_tile(span, 2048), tn=512)
    tc = _tile(span, 256)
    y_a, f_re, f_im = _s5_call(proj, prep, d_skip, w_glu, b_glu, s0_ssm, n_batch=n_batch, seq_len=seq_len,
                               tc=tc, n_sub=2 if span % (2 * tc) == 0 else 1)
    if long_seq:
        t = _tile(seq_len, 128)
        n_sub = 2 if seq_len % (2 * t) == 0 else 1
    else:
        t, n_sub = _tile(tokens, 128), 1
    merged, s_fin = _gla_call(proj, proj_tail, y_a, w_gate, b_gate, g_gain, w_a_out, w_b_out, s0_gla,
                              n_batch=n_batch, seq_len=seq_len, t=t, n_sub=n_sub)
    y = _out_call(merged, x2d, mod, w_out, fgain, rows_per_seq=seq_len, seq_base=n_seq_before,
                  tm=_tile(span, 512))
    st = lambda f: f.reshape(1, n_batch, N_GROUPS_A, P_STATE)
    return (y.reshape(n_batch, seq_len, D_MODEL), st(f_re), st(f_im),
            s_fin.reshape(1, n_batch, N_HEADS_B, DK_B, DV_B))


def kernel(x_prompt, x_sample, c_prompt, c_sample, state_ssm_re, state_ssm_im, state_gla, w_ada, b_ada, norm_gain, w_in, lambda_re, lambda_im, log_dt, ssm_b_re, ssm_b_im, ssm_c_re, ssm_c_im, d_skip, w_glu, b_glu, w_gate_up, b_gate, gla_norm_gain, w_a_out, w_b_out, w_out, final_norm_gain):
    assert w_ada.shape[0] == 1, "single-layer step"
    assert w_in.shape[2] == IN_COLS
    n_prompt, n_sample = x_prompt.shape[0], x_sample.shape[0]

    mod = _mod_call(jnp.concatenate([c_prompt, c_sample], axis=0), w_ada[0], b_ada)
    a_re, a_im, wb_re, wb_im, wc = _s5prep_call(lambda_re[0], lambda_im[0], log_dt[0], ssm_b_re[0], ssm_b_im[0],
                                                ssm_c_re[0], ssm_c_im[0])
    state_tiles = lambda a: a.reshape(-1, S5_TILES, SUBLANES, LANES)
    prep = (state_tiles(a_re)[0], state_tiles(a_im)[0], wb_re, wb_im, wc)

    w_in_t = jnp.swapaxes(w_in, 1, 2).reshape(IN_COLS, D_MODEL)
    w_gate = jnp.pad(w_gate_up[0], ((0, LANES - GATE_RANK), (0, 0))).astype(BF16)
    weights = (norm_gain, w_in_t, d_skip, w_glu[0].astype(BF16), b_glu, w_a_out[0].astype(BF16),
               w_gate, b_gate, gla_norm_gain, w_b_out[0].astype(BF16), w_out[0].astype(BF16),
               final_norm_gain.reshape(1, D_MODEL))

    y_p, pre, pim, pgla = _layer(x_prompt, 0, mod, weights, prep, None, None)
    y_s, sre, sim, sgla = _layer(x_sample, n_prompt, mod, weights, prep,
                                 (state_tiles(state_ssm_re[0]), state_tiles(state_ssm_im[0])), state_gla[0])
    return (y_p, y_s, pre, pim, pgla, sre, sim, sgla)
```

```python
import functools

import numpy as np
import jax
import jax.numpy as jnp
from jax import lax
from jax.experimental import pallas as pl
from jax.experimental.pallas import tpu as pltpu

F32 = jnp.float32
BF16 = jnp.bfloat16

D_MODEL = 2048
W_A = D_MODEL // 2
GROUP_A = 16
N_GROUPS_A = W_A // GROUP_A
P_STATE = 64
N_STATE = N_GROUPS_A * P_STATE
W_B = D_MODEL // 2
N_HEADS_B = 4
DK_B = W_B // 2 // N_HEADS_B
DV_B = W_B // N_HEADS_B
QK_W = N_HEADS_B * DK_B
GATE_RANK = 16
GATE_TAU = 16.0
EPS = 1e-6

LANES = 128
SUBLANES = 8
V7X_VMEM_BYTES = 64 * 1024 * 1024
VMEM_LIMIT_BYTES = 56 * 1024 * 1024

S5_TILE_GROUPS = 16
S5_TILES = N_GROUPS_A // S5_TILE_GROUPS
S5_TILE_IN = S5_TILE_GROUPS * GROUP_A
S5_TILE_ST = S5_TILE_GROUPS * P_STATE
assert S5_TILE_ST == SUBLANES * LANES

COL_UA = 0
COL_ZA = COL_UA + W_A
COL_Q = COL_ZA + W_A
COL_K = COL_Q + QK_W
COL_V = COL_K + QK_W
COL_ZB = COL_V + W_B
COL_G = COL_ZB + W_B
COL_MA = COL_G + GATE_RANK
COL_MB = COL_MA + D_MODEL
IN_COLS = COL_MB + D_MODEL
PROJ_COLS = (IN_COLS // LANES) * LANES
GATE_OFF = COL_MA % LANES
assert COL_MB % LANES == GATE_OFF and IN_COLS - PROJ_COLS == GATE_OFF


def _dot(a, b):
    return jnp.dot(a, b, preferred_element_type=F32)


def _dot_nt(a, b):
    return lax.dot_general(a, b, (((1,), (1,)), ((), ())), preferred_element_type=F32)


def _dot_tn(a, b):
    return lax.dot_general(a, b, (((0,), (0,)), ((), ())), preferred_element_type=F32)


def _shr(x, pow2):
    sh = int(pow2).bit_length() - 1
    assert 1 << sh == pow2
    return jnp.right_shift(x, sh)


def _cparams(n_axes):
    return pltpu.CompilerParams(dimension_semantics=("arbitrary",) * n_axes,
                                vmem_limit_bytes=VMEM_LIMIT_BYTES)


def _const_spec(shape, n_axes):
    zeros = (0,) * len(shape)
    index_map = (lambda a: zeros) if n_axes == 1 else (lambda a, b: zeros)
    return pl.BlockSpec(shape, index_map, pipeline_mode=pl.Buffered(1))


def _seq_rows(ref, row0, rc, rows_per_seq, seq_base):
    s = seq_base + lax.div(row0, jnp.int32(rows_per_seq))
    if rows_per_seq >= rc:
        return ref[pl.ds(s, 1), :]
    assert rc == 2 * rows_per_seq
    rid = lax.broadcasted_iota(jnp.int32, (rc, 1), 0)
    return jnp.where(rid < rows_per_seq, ref[pl.ds(s, 1), :], ref[pl.ds(s + 1, 1), :])


def _merge_gate(lo_ref, mid_ref, hi_ref):
    m = jnp.concatenate([lo_ref[...], mid_ref[...], hi_ref[...]], axis=1)
    return jax.nn.sigmoid(m[:, GATE_OFF:GATE_OFF + D_MODEL])


def _mod_kernel(c_ref, w_ref, b_ref, o_ref):
    s = jax.nn.silu(c_ref[...]).astype(BF16)
    o_ref[...] = _dot(s, w_ref[...].astype(BF16)) + b_ref[...]


def _mod_call(c_all, w_ada, b_ada):
    ns = c_all.shape[0]
    tn = 1024
    return pl.pallas_call(
        _mod_kernel,
        grid=(3 * D_MODEL // tn,),
        in_specs=[_const_spec((ns, D_MODEL), 1),
                  pl.BlockSpec((D_MODEL, tn), lambda n: (0, n)),
                  pl.BlockSpec((1, tn), lambda n: (0, n))],
        out_specs=pl.BlockSpec((ns, tn), lambda n: (0, n)),
        out_shape=jax.ShapeDtypeStruct((ns, 3 * D_MODEL), F32),
        compiler_params=_cparams(1),
        name="adaln_mod",
    )(c_all, w_ada, b_ada)


def _s5prep_kernel(lr_ref, li_ref, ldt_ref, bre_ref, bim_ref, cre_ref, cim_ref,
                   are_ref, aim_ref, wbre_ref, wbim_ref, wc_ref):
    dt = jnp.exp(ldt_ref[...])
    lr, li = lr_ref[...], li_ref[...]
    mag = jnp.exp(lr * dt)
    ab_re, ab_im = mag * jnp.cos(li * dt), mag * jnp.sin(li * dt)
    nr, ni = ab_re - 1.0, ab_im
    den = lr * lr + li * li
    cf_re = (nr * lr + ni * li) / den
    cf_im = (ni * lr - nr * li) / den
    are_ref[...] = ab_re
    aim_ref[...] = ab_im
    r = lax.broadcasted_iota(jnp.int32, (S5_TILE_IN, S5_TILE_ST), 0)
    c = lax.broadcasted_iota(jnp.int32, (S5_TILE_IN, S5_TILE_ST), 1)
    on_b = _shr(r, GROUP_A) == _shr(c, P_STATE)
    bre, bim = bre_ref[...], bim_ref[...]
    rep = lambda a: jnp.concatenate([a] * S5_TILE_GROUPS, axis=0)
    wbre_ref[...] = jnp.where(on_b, rep(cf_re * bre - cf_im * bim), 0.0).astype(BF16)
    wbim_ref[...] = jnp.where(on_b, rep(cf_re * bim + cf_im * bre), 0.0).astype(BF16)
    h = lax.broadcasted_iota(jnp.int32, (GROUP_A, S5_TILE_IN), 0)
    c = lax.broadcasted_iota(jnp.int32, (GROUP_A, S5_TILE_IN), 1)
    spread = jnp.where((c & (GROUP_A - 1)) == h, 1.0, 0.0).astype(BF16)
    r = lax.broadcasted_iota(jnp.int32, (S5_TILE_ST, S5_TILE_IN), 0)
    c = lax.broadcasted_iota(jnp.int32, (S5_TILE_ST, S5_TILE_IN), 1)
    on_c = _shr(r, P_STATE) == _shr(c, GROUP_A)
    wc_ref[0:S5_TILE_ST, :] = jnp.where(on_c, _dot(cre_ref[...].astype(BF16), spread), 0.0).astype(BF16)
    wc_ref[S5_TILE_ST:2 * S5_TILE_ST, :] = jnp.where(on_c, -_dot(cim_ref[...].astype(BF16), spread), 0.0).astype(BF16)


def _s5prep_call(lam_re, lam_im, log_dt, b_re, b_im, c_re, c_im):
    row = lambda a: a.reshape(1, N_STATE)
    ldt = jnp.broadcast_to(log_dt[:, None], (N_GROUPS_A, P_STATE))
    bt = lambda b: b.transpose(2, 0, 1).reshape(GROUP_A, N_STATE)
    ct = lambda c: c.transpose(0, 2, 1).reshape(N_STATE, GROUP_A)
    rspec = pl.BlockSpec((1, S5_TILE_ST), lambda i: (0, i))
    bspec = pl.BlockSpec((S5_TILE_IN, S5_TILE_ST), lambda i: (0, i))
    bin_spec = pl.BlockSpec((GROUP_A, S5_TILE_ST), lambda i: (0, i))
    cin_spec = pl.BlockSpec((S5_TILE_ST, GROUP_A), lambda i: (i, 0))
    return pl.pallas_call(
        _s5prep_kernel,
        grid=(S5_TILES,),
        in_specs=[rspec, rspec, rspec, bin_spec, bin_spec, cin_spec, cin_spec],
        out_specs=[rspec, rspec, bspec, bspec,
                   pl.BlockSpec((None, 2 * S5_TILE_ST, S5_TILE_IN), lambda i: (i, 0, 0))],
        out_shape=[jax.ShapeDtypeStruct((1, N_STATE), F32), jax.ShapeDtypeStruct((1, N_STATE), F32),
                   jax.ShapeDtypeStruct((S5_TILE_IN, N_STATE), BF16),
                   jax.ShapeDtypeStruct((S5_TILE_IN, N_STATE), BF16),
                   jax.ShapeDtypeStruct((S5_TILES, 2 * S5_TILE_ST, S5_TILE_IN), BF16)],
        compiler_params=_cparams(1),
        name="s5_prep",
    )(row(lam_re), row(lam_im), row(ldt), bt(b_re), bt(b_im), ct(c_re), ct(c_im))


def _inproj_kernel(x_ref, shift_ref, scale_ref, gain_ref, w_ref, wt_ref, p_ref, pt_ref, h_scr,
                   *, tm, rc, rows_per_seq, seq_base):
    mt, n = pl.program_id(0), pl.program_id(1)

    @pl.when(n == 0)
    def _():
        gain = gain_ref[...]

        def body(i, carry):
            r0 = pl.multiple_of(i * rc, rc)
            row0 = mt * tm + r0
            xx = x_ref[pl.ds(r0, rc), :]
            y = xx * lax.rsqrt(jnp.mean(xx * xx, axis=-1, keepdims=True) + EPS) * gain
            sc = _seq_rows(scale_ref, row0, rc, rows_per_seq, seq_base)
            sh = _seq_rows(shift_ref, row0, rc, rows_per_seq, seq_base)
            h_scr[pl.ds(r0, rc), :] = (y * (1.0 + sc) + sh).astype(BF16)
            return carry

        lax.fori_loop(0, tm // rc, body, 0)
        pt_ref[...] = _dot_nt(h_scr[...], wt_ref[...].astype(BF16))

    p_ref[...] = _dot_nt(h_scr[...], w_ref[...].astype(BF16))


def _inproj_call(x2d, mod, gain, w_in_t, *, rows_per_seq, seq_base, tm, tn):
    tokens = x2d.shape[0]
    ns = mod.shape[0]
    rc = 128 if rows_per_seq >= 128 else 2 * rows_per_seq
    kern = functools.partial(_inproj_kernel, tm=tm, rc=rc, rows_per_seq=rows_per_seq, seq_base=seq_base)
    x_buffers = 1 if tokens // tm > 1 and tm * D_MODEL * 4 > 8 * 1024 * 1024 else 2
    return pl.pallas_call(
        kern,
        grid=(tokens // tm, PROJ_COLS // tn),
        in_specs=[pl.BlockSpec((tm, D_MODEL), lambda m, n: (m, 0), pipeline_mode=pl.Buffered(x_buffers)),
                  pl.BlockSpec((ns, D_MODEL), lambda m, n: (0, 0), pipeline_mode=pl.Buffered(1)),
                  pl.BlockSpec((ns, D_MODEL), lambda m, n: (0, 1), pipeline_mode=pl.Buffered(1)),
                  _const_spec((1, D_MODEL), 2),
                  pl.BlockSpec((tn, D_MODEL), lambda m, n: (n, 0)),
                  pl.BlockSpec((GATE_OFF, D_MODEL), lambda m, n: (PROJ_COLS // GATE_OFF, 0),
                               pipeline_mode=pl.Buffered(1))],
        out_specs=[pl.BlockSpec((tm, tn), lambda m, n: (m, n)),
                   pl.BlockSpec((tm, GATE_OFF), lambda m, n: (m, 0))],
        out_shape=[jax.ShapeDtypeStruct((tokens, PROJ_COLS), F32),
                   jax.ShapeDtypeStruct((tokens, GATE_OFF), F32)],
        scratch_shapes=[pltpu.VMEM((tm, D_MODEL), BF16)],
        compiler_params=_cparams(2),
        name="in_proj",
    )(x2d, mod, mod, gain, w_in_t, w_in_t)


def _s5_kernel(*refs, tc, n_sub, ls, has_h0):
    n_in = 12 if has_h0 else 10
    if has_h0:
        (u_ref, za_ref, h0re_ref, h0im_ref, wbre_ref, wbim_ref, wc_ref,
         are_ref, aim_ref, dskip_ref, wglu_ref, bglu_ref) = refs[:n_in]
    else:
        (u_ref, za_ref, wbre_ref, wbim_ref, wc_ref,
         are_ref, aim_ref, dskip_ref, wglu_ref, bglu_ref) = refs[:n_in]
    gated_ref, fre_ref, fim_ref = refs[n_in:n_in + 3]
    scr = refs[n_in + 3:n_in + 3 + 2 * n_sub * S5_TILES]
    xr = [scr[2 * c * S5_TILES:(2 * c + 1) * S5_TILES] for c in range(n_sub)]
    xi = [scr[(2 * c + 1) * S5_TILES:(2 * c + 2) * S5_TILES] for c in range(n_sub)]
    hc_re, hc_im = refs[n_in + 3 + 2 * n_sub * S5_TILES:]
    ct = pl.program_id(1)
    n_seq = tc // ls

    us = []
    for c in range(n_sub):
        u = u_ref[c * tc:(c + 1) * tc, :]
        us.append(u)
        ub = u.astype(BF16)
        for i in range(S5_TILES):
            ui = ub[:, i * S5_TILE_IN:(i + 1) * S5_TILE_IN]
            for w_ref, x_scr in ((wbre_ref, xr[c][i]), (wbim_ref, xi[c][i])):
                res = _dot(ui, w_ref[:, i * S5_TILE_ST:(i + 1) * S5_TILE_ST])
                for j in range(SUBLANES):
                    x_scr[pl.ds(j, tc, stride=SUBLANES), :] = res[:, j * LANES:(j + 1) * LANES]

    a_re = [are_ref[i] for i in range(S5_TILES)]
    a_im = [aim_ref[i] for i in range(S5_TILES)]
    if not has_h0:
        h_re = [jnp.where(ct == 0, 0.0, hc_re[i]) for i in range(S5_TILES)]
        h_im = [jnp.where(ct == 0, 0.0, hc_im[i]) for i in range(S5_TILES)]
    for c in range(n_sub):
        for s in range(n_seq):
            if has_h0:
                h_re = [h0re_ref[c * n_seq + s, i] for i in range(S5_TILES)]
                h_im = [h0im_ref[c * n_seq + s, i] for i in range(S5_TILES)]
            for tl in range(ls):
                r0 = (s * ls + tl) * SUBLANES
                for i in range(S5_TILES):
                    nr = a_re[i] * h_re[i] - a_im[i] * h_im[i] + xr[c][i][r0:r0 + SUBLANES, :]
                    ni = a_re[i] * h_im[i] + a_im[i] * h_re[i] + xi[c][i][r0:r0 + SUBLANES, :]
                    xr[c][i][r0:r0 + SUBLANES, :] = nr
                    xi[c][i][r0:r0 + SUBLANES, :] = ni
                    h_re[i], h_im[i] = nr, ni
            if has_h0:
                for i in range(S5_TILES):
                    fre_ref[c * n_seq + s, i] = h_re[i]
                    fim_ref[c * n_seq + s, i] = h_im[i]
    if not has_h0:
        for i in range(S5_TILES):
            hc_re[i] = h_re[i]
            hc_im[i] = h_im[i]
            fre_ref[0, i] = h_re[i]
            fim_ref[0, i] = h_im[i]

    for c in range(n_sub):
        ys = []
        for i in range(S5_TILES):
            parts = [x_scr[pl.ds(j, tc, stride=SUBLANES), :]
                     for x_scr in (xr[c][i], xi[c][i]) for j in range(SUBLANES)]
            ys.append(_dot(jnp.concatenate(parts, axis=1).astype(BF16), wc_ref[i]))
        y = jnp.concatenate(ys, axis=1) + dskip_ref[...] * us[c]
        g = jax.nn.gelu(y)
        a_y = g * jax.nn.sigmoid(_dot(g.astype(BF16), wglu_ref[...]) + bglu_ref[...])
        gated_ref[c * tc:(c + 1) * tc, :] = (a_y * jax.nn.silu(za_ref[c * tc:(c + 1) * tc, :])).astype(BF16)


def _s5_call(proj, prep, d_skip, w_glu, b_glu, h0, *, n_batch, seq_len, tc, n_sub):
    a_re, a_im, wb_re, wb_im, wc = prep
    tokens = proj.shape[0]
    has_h0 = h0 is not None
    tt = tc * n_sub
    if has_h0:
        ls = seq_len
        grid = (tokens // tt, 1)
        tok = lambda b, c: b
        seq_block = tt // ls
    else:
        ls = tc
        nct = seq_len // tt
        grid = (n_batch, nct)
        tok = lambda b, c: b * nct + c
        seq_block = 1
    const = lambda shape: _const_spec(shape, 2)
    st_spec = pl.BlockSpec((seq_block, S5_TILES, SUBLANES, LANES), lambda b, c: (b, 0, 0, 0))
    in_specs = [pl.BlockSpec((tt, W_A), lambda b, c: (tok(b, c), COL_UA // W_A)),
                pl.BlockSpec((tt, W_A), lambda b, c: (tok(b, c), COL_ZA // W_A))]
    args = [proj] * 2
    if has_h0:
        in_specs += [st_spec, st_spec]
        args += [h0[0], h0[1]]
    in_specs += [const((S5_TILE_IN, N_STATE)), const((S5_TILE_IN, N_STATE)),
                 const((S5_TILES, 2 * S5_TILE_ST, S5_TILE_IN)),
                 const((S5_TILES, SUBLANES, LANES)), const((S5_TILES, SUBLANES, LANES)),
                 const((1, W_A)), const((W_A, W_A)), const((1, W_A))]
    args += [wb_re, wb_im, wc, a_re, a_im, d_skip, w_glu, b_glu]
    n_seq_total = tokens // seq_len
    st_shape = jax.ShapeDtypeStruct((n_seq_total, S5_TILES, SUBLANES, LANES), F32)
    return pl.pallas_call(
        functools.partial(_s5_kernel, tc=tc, n_sub=n_sub, ls=ls, has_h0=has_h0),
        grid=grid,
        in_specs=in_specs,
        out_specs=[pl.BlockSpec((tt, W_A), lambda b, c: (tok(b, c), 0)), st_spec, st_spec],
        out_shape=[jax.ShapeDtypeStruct((tokens, W_A), BF16), st_shape, st_shape],
        scratch_shapes=([pltpu.VMEM((tc * SUBLANES, LANES), F32)] * (2 * n_sub * S5_TILES)
                        + [pltpu.VMEM((S5_TILES, SUBLANES, LANES), F32)] * 2),
        compiler_params=_cparams(2),
        name="s5_branch",
    )(*args)


def _gla_levels(t, ls):
    ms = []
    m = ls
    while m >= 1:
        ms.append(m)
        m //= 2
    idx = np.arange(t)
    mats = []
    for m in ms:
        if m == 1:
            continue
        same = (idx[:, None] // m) == (idx[None, :] // m)
        mats.append(same & (idx[None, :] <= idx[:, None]))
        mats.append(same)
    return ms, np.concatenate(mats, axis=0).astype(np.float32)


def _gla_kernel(*refs, t, n_sub, ls, has_s0, ms):
    if has_s0:
        (q_ref, k_ref, v_ref, zb_ref, ma0_ref, ma1_ref, ma2_ref, mb0_ref, mb1_ref, mb2_ref, glr_ref, ya_ref,
         s0_ref, lv_ref, wg_ref, bg_ref, gg_ref, waout_ref, wbout_ref, merged_ref, sfin_ref, s_scr) = refs
        assert n_sub == 1
    else:
        (q_ref, k_ref, v_ref, zb_ref, ma0_ref, ma1_ref, ma2_ref, mb0_ref, mb1_ref, mb2_ref, glr_ref, ya_ref,
         lv_ref, wg_ref, bg_ref, gg_ref, waout_ref, wbout_ref, merged_ref, sfin_ref, s_scr) = refs
        assert ls == t
    ct = pl.program_id(1)
    n_seq = t // ls

    ti = lax.broadcasted_iota(jnp.int32, (t, t), 0)
    si = lax.broadcasted_iota(jnp.int32, (t, t), 1)
    masks = {}
    for li in range(1, len(ms)):
        m = ms[li]
        masks[li] = ((_shr(ti, 2 * m) == _shr(si, 2 * m)) & ((_shr(ti, m) & 1) == 1) & ((_shr(si, m) & 1) == 0))
    rid = lax.broadcasted_iota(jnp.int32, (t, 1), 0)

    if not has_s0:
        states = [jnp.where(ct == 0, 0.0, s_scr[h]) for h in range(N_HEADS_B)]
    ga = jax.nn.log_sigmoid(_dot(glr_ref[...].astype(BF16), wg_ref[...]) + bg_ref[...]) * (1.0 / GATE_TAU)
    ga_b = ga.astype(BF16)
    ga_r = ga_b.astype(F32)
    e_all = _dot(lv_ref[...], jnp.concatenate([ga_b[c * t:(c + 1) * t] for c in range(n_sub)], axis=1))
    gated_rows = []
    for c in range(n_sub):
        rows = slice(c * t, (c + 1) * t)
        cols = slice(c * QK_W, (c + 1) * QK_W)
        cum = lambda li: e_all[2 * li * t:(2 * li + 1) * t, cols]
        tot = lambda li: e_all[(2 * li + 1) * t:(2 * li + 2) * t, cols]

        q, k, v = q_ref[rows, :], k_ref[rows, :], v_ref[rows, :]
        heads = []
        for h in range(N_HEADS_B):
            hs = slice(h * DK_B, (h + 1) * DK_B)
            qs = q[:, hs] * (DK_B ** -0.5)
            kh = k[:, hs]
            vh = v[:, h * DV_B:(h + 1) * DV_B]
            vb = vh.astype(BF16)
            b = cum(0)[:, hs]
            bend = tot(0)[:, hs]
            q_dec = (qs * jnp.exp(b)).astype(BF16)
            k_dec = kh * jnp.exp(bend - b)
            scores = jnp.zeros((t, t), F32)
            for li in range(1, len(ms)):
                if ms[li] == 1:
                    qm = (qs * jnp.exp(ga_r[rows, hs])).astype(BF16)
                    km = kh.astype(BF16)
                else:
                    eq = cum(li)[:, hs]
                    qm = (qs * jnp.exp(eq)).astype(BF16)
                    km = (kh * jnp.exp(tot(li)[:, hs] - eq)).astype(BF16)
                scores = jnp.where(masks[li], _dot_nt(qm, km), scores)
            o = _dot(scores.astype(BF16), vb) + jnp.sum(qs * kh, axis=-1, keepdims=True) * vh
            o_inter = []
            for i in range(n_seq):
                r0 = i * ls
                s_old = s0_ref[i, h] if has_s0 else states[h]
                o_inter.append(_dot(q_dec[r0:r0 + ls], s_old.astype(BF16)))
                kd = k_dec if n_seq == 1 else jnp.where((rid >= r0) & (rid < r0 + ls), k_dec, 0.0)
                d_end = jnp.exp(bend[r0:r0 + 1, :])
                d_col = jnp.broadcast_to(d_end, (DK_B, DK_B)).T
                s_new = jnp.concatenate([d_col] * (DV_B // DK_B), axis=1) * s_old + _dot_tn(kd.astype(BF16), vb)
                if has_s0:
                    sfin_ref[i, h] = s_new
                else:
                    states[h] = s_new
            o = o + (o_inter[0] if n_seq == 1 else jnp.concatenate(o_inter, axis=0))
            heads.append(o * lax.rsqrt(jnp.mean(o * o, axis=-1, keepdims=True) + EPS))
        b_y = jnp.concatenate(heads, axis=1) * gg_ref[...]
        gated_rows.append((b_y * jax.nn.silu(zb_ref[rows, :])).astype(BF16))
    if not has_s0:
        for h in range(N_HEADS_B):
            s_scr[h] = states[h]
            sfin_ref[0, h] = states[h]
    gated = gated_rows[0] if n_sub == 1 else jnp.concatenate(gated_rows, axis=0)
    merged = (_merge_gate(ma0_ref, ma1_ref, ma2_ref) * _dot(ya_ref[...], waout_ref[...])
              + _merge_gate(mb0_ref, mb1_ref, mb2_ref) * _dot(gated, wbout_ref[...]))
    merged_ref[...] = merged.astype(BF16)


def _gla_call(proj, proj_tail, y_a, w_gate, b_gate, g_gain, w_a_out, w_b_out, s0, *, n_batch, seq_len, t, n_sub):
    tokens = proj.shape[0]
    has_s0 = s0 is not None
    tt = t * n_sub
    if has_s0:
        ls = seq_len
        grid = (tokens // tt, 1)
        tok = lambda b, c: b
        seq_block = tt // ls
    else:
        ls = t
        nct = seq_len // tt
        grid = (n_batch, nct)
        tok = lambda b, c: b * nct + c
        seq_block = 1
    ms, lv = _gla_levels(t, ls)
    lv = jnp.asarray(lv, dtype=BF16)
    const = lambda shape: _const_spec(shape, 2)
    st_spec = pl.BlockSpec((seq_block, N_HEADS_B, DK_B, DV_B), lambda b, c: (b, 0, 0, 0))
    in_specs = [pl.BlockSpec((tt, QK_W), lambda b, c: (tok(b, c), COL_Q // QK_W)),
                pl.BlockSpec((tt, QK_W), lambda b, c: (tok(b, c), COL_K // QK_W)),
                pl.BlockSpec((tt, W_B), lambda b, c: (tok(b, c), COL_V // W_B)),
                pl.BlockSpec((tt, W_B), lambda b, c: (tok(b, c), COL_ZB // W_B)),
                pl.BlockSpec((tt, W_A), lambda b, c: (tok(b, c), COL_MA // W_A)),
                pl.BlockSpec((tt, W_A), lambda b, c: (tok(b, c), COL_MA // W_A + 1)),
                pl.BlockSpec((tt, LANES), lambda b, c: (tok(b, c), COL_MB // LANES)),
                pl.BlockSpec((tt, W_B), lambda b, c: (tok(b, c), COL_MB // W_B)),
                pl.BlockSpec((tt, W_B), lambda b, c: (tok(b, c), COL_MB // W_B + 1)),
                pl.BlockSpec((tt, GATE_OFF), lambda b, c: (tok(b, c), 0)),
                pl.BlockSpec((tt, LANES), lambda b, c: (tok(b, c), COL_G // LANES)),
                pl.BlockSpec((tt, W_A), lambda b, c: (tok(b, c), 0))]
    args = [proj, proj, proj, proj, proj, proj, proj, proj, proj, proj_tail, proj, y_a]
    if has_s0:
        in_specs.append(st_spec)
        args.append(s0)
    in_specs += [const(lv.shape), const((LANES, QK_W)), const((1, QK_W)), const((1, W_B)),
                 const((W_A, D_MODEL)), const((W_B, D_MODEL))]
    args += [lv, w_gate, b_gate, g_gain, w_a_out, w_b_out]
    n_seq_total = tokens // seq_len
    return pl.pallas_call(
        functools.partial(_gla_kernel, t=t, n_sub=n_sub, ls=ls, has_s0=has_s0, ms=tuple(ms)),
        grid=grid,
        in_specs=in_specs,
        out_specs=[pl.BlockSpec((tt, D_MODEL), lambda b, c: (tok(b, c), 0)), st_spec],
        out_shape=[jax.ShapeDtypeStruct((tokens, D_MODEL), BF16),
                   jax.ShapeDtypeStruct((n_seq_total, N_HEADS_B, DK_B, DV_B), F32)],
        scratch_shapes=[pltpu.VMEM((N_HEADS_B, DK_B, DV_B), F32)],
        compiler_params=_cparams(2),
        name="gla_branch",
    )(*args)


def _out_kernel(m_ref, x_ref, gate_ref, w_ref, fg_ref, y_ref, acc0, acc1,
                *, tm, rc, rows_per_seq, seq_base, n_tiles):
    s = pl.program_id(0)

    def matmul(acc):
        acc[...] = _dot(m_ref[...], w_ref[...])

    def epilogue(acc):
        fg = fg_ref[...]
        for i in range(tm // rc):
            r0 = i * rc
            gate = _seq_rows(gate_ref, (s - 1) * tm + r0, rc, rows_per_seq, seq_base)
            yy = x_ref[r0:r0 + rc, :] + gate * acc[r0:r0 + rc, :]
            y_ref[r0:r0 + rc, :] = yy * lax.rsqrt(jnp.mean(yy * yy, axis=-1, keepdims=True) + EPS) * fg

    accs = (acc0, acc1)

    pl.when(s == 0)(lambda: matmul(acc0))
    for par in (0, 1):
        @pl.when((s > 0) & (s < n_tiles) & (lax.rem(s, 2) == par))
        def _(par=par):
            matmul(accs[par])
            epilogue(accs[1 - par])

    pl.when(s == n_tiles)(lambda: epilogue(accs[(n_tiles - 1) % 2]))


def _out_call(merged, x2d, mod, w_out, fgain, *, rows_per_seq, seq_base, tm):
    tokens = x2d.shape[0]
    ns = mod.shape[0]
    n_tiles = tokens // tm
    rc = 128 if rows_per_seq >= 128 else 2 * rows_per_seq
    kern = functools.partial(_out_kernel, tm=tm, rc=rc, rows_per_seq=rows_per_seq, seq_base=seq_base,
                             n_tiles=n_tiles)
    prev = lambda m: (jnp.maximum(m - 1, 0), 0)
    return pl.pallas_call(
        kern,
        grid=(n_tiles + 1,),
        in_specs=[pl.BlockSpec((tm, D_MODEL), lambda m: (jnp.minimum(m, n_tiles - 1), 0)),
                  pl.BlockSpec((tm, D_MODEL), prev),
                  pl.BlockSpec((ns, D_MODEL), lambda m: (0, 2), pipeline_mode=pl.Buffered(1)),
                  _const_spec((D_MODEL, D_MODEL), 1),
                  _const_spec((1, D_MODEL), 1)],
        out_specs=pl.BlockSpec((tm, D_MODEL), prev),
        out_shape=jax.ShapeDtypeStruct((tokens, D_MODEL), F32),
        scratch_shapes=[pltpu.VMEM((tm, D_MODEL), F32), pltpu.VMEM((tm, D_MODEL), F32)],
        compiler_params=_cparams(1),
        name="out_proj",
    )(merged, x2d, mod, w_out, fgain)


def _tile(n, target):
    t = min(n, target)
    assert n % t == 0
    return t


def _layer(x, n_seq_before, mod, weights, prep, s0_ssm, s0_gla):
    (gain, w_in_t, d_skip, w_glu, b_glu, w_a_out, w_gate, b_gate, g_gain, w_b_out, w_out, fgain) = weights
    n_batch, seq_len, _ = x.shape
    tokens = n_batch * seq_len
    x2d = x.reshape(tokens, D_MODEL)
    long_seq = s0_ssm is None
    span = seq_len if long_seq else tokens
    proj, proj_tail = _inproj_call(x2d, mod, gain, w_in_t, rows_per_seq=seq_len, seq_base=n_seq_before,
                                   tm=_tile(span, 2048), tn=512)
    tc = _tile(span, 256)
    y_a, f_re, f_im = _s5_call(proj, prep, d_skip, w_glu, b_glu, s0_ssm, n_batch=n_batch, seq_len=seq_len,
                               tc=tc, n_sub=2 if span % (2 * tc) == 0 else 1)
    if long_seq:
        t = _tile(seq_len, 128)
        n_sub = 2 if seq_len % (2 * t) == 0 else 1
    else:
        t, n_sub = _tile(tokens, 128), 1
    merged, s_fin = _gla_call(proj, proj_tail, y_a, w_gate, b_gate, g_gain, w_a_out, w_b_out, s0_gla,
                              n_batch=n_batch, seq_len=seq_len, t=t, n_sub=n_sub)
    y = _out_call(merged, x2d, mod, w_out, fgain, rows_per_seq=seq_len, seq_base=n_seq_before,
                  tm=_tile(span, 512))
    st = lambda f: f.reshape(1, n_batch, N_GROUPS_A, P_STATE)
    return (y.reshape(n_batch, seq_len, D_MODEL), st(f_re), st(f_im),
            s_fin.reshape(1, n_batch, N_HEADS_B, DK_B, DV_B))


def kernel(x_prompt, x_sample, c_prompt, c_sample, state_ssm_re, state_ssm_im, state_gla, w_ada, b_ada, norm_gain, w_in, lambda_re, lambda_im, log_dt, ssm_b_re, ssm_b_im, ssm_c_re, ssm_c_im, d_skip, w_glu, b_glu, w_gate_up, b_gate, gla_norm_gain, w_a_out, w_b_out, w_out, final_norm_gain):
    assert w_ada.shape[0] == 1, "single-layer step"
    assert w_in.shape[2] == IN_COLS
    n_prompt, n_sample = x_prompt.shape[0], x_sample.shape[0]

    mod = _mod_call(jnp.concatenate([c_prompt, c_sample], axis=0), w_ada[0], b_ada)
    a_re, a_im, wb_re, wb_im, wc = _s5prep_call(lambda_re[0], lambda_im[0], log_dt[0], ssm_b_re[0], ssm_b_im[0],
                                                ssm_c_re[0], ssm_c_im[0])
    state_tiles = lambda a: a.reshape(-1, S5_TILES, SUBLANES, LANES)
    prep = (state_tiles(a_re)[0], state_tiles(a_im)[0], wb_re, wb_im, wc)

    w_in_t = jnp.swapaxes(w_in, 1, 2).reshape(IN_COLS, D_MODEL)
    w_gate = jnp.pad(w_gate_up[0], ((0, LANES - GATE_RANK), (0, 0))).astype(BF16)
    weights = (norm_gain, w_in_t, d_skip, w_glu[0].astype(BF16), b_glu, w_a_out[0].astype(BF16),
               w_gate, b_gate, gla_norm_gain, w_b_out[0].astype(BF16), w_out[0].astype(BF16),
               final_norm_gain.reshape(1, D_MODEL))

    y_p, pre, pim, pgla = _layer(x_prompt, 0, mod, weights, prep, None, None)
    y_s, sre, sim, sgla = _layer(x_sample, n_prompt, mod, weights, prep,
                                 (state_tiles(state_ssm_re[0]), state_tiles(state_ssm_im[0])), state_gla[0])
    return (y_p, y_s, pre, pim, pgla, sre, sim, sgla)
```

```python
import functools

import numpy as np
import jax
import jax.numpy as jnp
from jax import lax
from jax.experimental import pallas as pl
from jax.experimental.pallas import tpu as pltpu

F32 = jnp.float32
BF16 = jnp.bfloat16

D_MODEL = 2048
W_A = D_MODEL // 2
GROUP_A = 16
N_GROUPS_A = W_A // GROUP_A
P_STATE = 64
N_STATE = N_GROUPS_A * P_STATE
W_B = D_MODEL // 2
N_HEADS_B = 4
DK_B = W_B // 2 // N_HEADS_B
DV_B = W_B // N_HEADS_B
QK_W = N_HEADS_B * DK_B
GATE_RANK = 16
GATE_TAU = 16.0
EPS = 1e-6

LANES = 128
SUBLANES = 8
V7X_VMEM_BYTES = 64 * 1024 * 1024
VMEM_LIMIT_BYTES = 56 * 1024 * 1024

S5_TILE_GROUPS = 16
S5_TILES = N_GROUPS_A // S5_TILE_GROUPS
S5_TILE_IN = S5_TILE_GROUPS * GROUP_A
S5_TILE_ST = S5_TILE_GROUPS * P_STATE
assert S5_TILE_ST == SUBLANES * LANES

COL_UA = 0
COL_ZA = COL_UA + W_A
COL_Q = COL_ZA + W_A
COL_K = COL_Q + QK_W
COL_V = COL_K + QK_W
COL_ZB = COL_V + W_B
COL_G = COL_ZB + W_B
COL_MA = COL_G + GATE_RANK
COL_MB = COL_MA + D_MODEL
IN_COLS = COL_MB + D_MODEL
PROJ_COLS = (IN_COLS // LANES) * LANES
GATE_OFF = COL_MA % LANES
assert COL_MB % LANES == GATE_OFF and IN_COLS - PROJ_COLS == GATE_OFF


def _dot(a, b):
    return jnp.dot(a, b, preferred_element_type=F32)


def _dot_nt(a, b):
    return lax.dot_general(a, b, (((1,), (1,)), ((), ())), preferred_element_type=F32)


def _dot_tn(a, b):
    return lax.dot_general(a, b, (((0,), (0,)), ((), ())), preferred_element_type=F32)


def _shr(x, pow2):
    sh = int(pow2).bit_length() - 1
    assert 1 << sh == pow2
    return jnp.right_shift(x, sh)


def _cparams(n_axes):
    return pltpu.CompilerParams(dimension_semantics=("arbitrary",) * n_axes,
                                vmem_limit_bytes=VMEM_LIMIT_BYTES)


def _const_spec(shape, n_axes):
    zeros = (0,) * len(shape)
    index_map = (lambda a: zeros) if n_axes == 1 else (lambda a, b: zeros)
    return pl.BlockSpec(shape, index_map, pipeline_mode=pl.Buffered(1))


def _seq_rows(ref, row0, rc, rows_per_seq, seq_base):
    s = seq_base + lax.div(row0, jnp.int32(rows_per_seq))
    if rows_per_seq >= rc:
        return ref[pl.ds(s, 1), :]
    assert rc % rows_per_seq == 0 and rows_per_seq % SUBLANES == 0
    width = ref.shape[1]
    return jnp.concatenate([jnp.broadcast_to(ref[pl.ds(s + j, 1), :], (rows_per_seq, width))
                            for j in range(rc // rows_per_seq)], axis=0)


def _merge_gate(lo_ref, mid_ref, hi_ref):
    m = jnp.concatenate([lo_ref[...], mid_ref[...], hi_ref[...]], axis=1)
    return jax.nn.sigmoid(m[:, GATE_OFF:GATE_OFF + D_MODEL])


def _mod_kernel(c_ref, w_ref, b_ref, o_ref):
    s = jax.nn.silu(c_ref[...]).astype(BF16)
    o_ref[...] = _dot(s, w_ref[...].astype(BF16)) + b_ref[...]


def _mod_call(c_all, w_ada, b_ada):
    ns = c_all.shape[0]
    tn = 1024
    return pl.pallas_call(
        _mod_kernel,
        grid=(3 * D_MODEL // tn,),
        in_specs=[_const_spec((ns, D_MODEL), 1),
                  pl.BlockSpec((D_MODEL, tn), lambda n: (0, n)),
                  pl.BlockSpec((1, tn), lambda n: (0, n))],
        out_specs=pl.BlockSpec((ns, tn), lambda n: (0, n)),
        out_shape=jax.ShapeDtypeStruct((ns, 3 * D_MODEL), F32),
        compiler_params=_cparams(1),
        name="adaln_mod",
    )(c_all, w_ada, b_ada)


def _s5prep_kernel(lr_ref, li_ref, ldt_ref, bre_ref, bim_ref, cre_ref, cim_ref,
                   are_ref, aim_ref, wbre_ref, wbim_ref, wc_ref):
    dt = jnp.exp(ldt_ref[...])
    lr, li = lr_ref[...], li_ref[...]
    mag = jnp.exp(lr * dt)
    ab_re, ab_im = mag * jnp.cos(li * dt), mag * jnp.sin(li * dt)
    nr, ni = ab_re - 1.0, ab_im
    den = lr * lr + li * li
    cf_re = (nr * lr + ni * li) / den
    cf_im = (ni * lr - nr * li) / den
    are_ref[...] = ab_re
    aim_ref[...] = ab_im
    r = lax.broadcasted_iota(jnp.int32, (S5_TILE_IN, S5_TILE_ST), 0)
    c = lax.broadcasted_iota(jnp.int32, (S5_TILE_IN, S5_TILE_ST), 1)
    on_b = _shr(r, GROUP_A) == _shr(c, P_STATE)
    bre, bim = bre_ref[...], bim_ref[...]
    rep = lambda a: jnp.concatenate([a] * S5_TILE_GROUPS, axis=0)
    wbre_ref[...] = jnp.where(on_b, rep(cf_re * bre - cf_im * bim), 0.0).astype(BF16)
    wbim_ref[...] = jnp.where(on_b, rep(cf_re * bim + cf_im * bre), 0.0).astype(BF16)
    h = lax.broadcasted_iota(jnp.int32, (GROUP_A, S5_TILE_IN), 0)
    c = lax.broadcasted_iota(jnp.int32, (GROUP_A, S5_TILE_IN), 1)
    spread = jnp.where((c & (GROUP_A - 1)) == h, 1.0, 0.0).astype(BF16)
    r = lax.broadcasted_iota(jnp.int32, (S5_TILE_ST, S5_TILE_IN), 0)
    c = lax.broadcasted_iota(jnp.int32, (S5_TILE_ST, S5_TILE_IN), 1)
    on_c = _shr(r, P_STATE) == _shr(c, GROUP_A)
    wc_ref[0:S5_TILE_ST, :] = jnp.where(on_c, _dot(cre_ref[...].astype(BF16), spread), 0.0).astype(BF16)
    wc_ref[S5_TILE_ST:2 * S5_TILE_ST, :] = jnp.where(on_c, -_dot(cim_ref[...].astype(BF16), spread), 0.0).astype(BF16)


def _s5prep_call(lam_re, lam_im, log_dt, b_re, b_im, c_re, c_im):
    row = lambda a: a.reshape(1, N_STATE)
    ldt = jnp.broadcast_to(log_dt[:, None], (N_GROUPS_A, P_STATE))
    bt = lambda b: b.transpose(2, 0, 1).reshape(GROUP_A, N_STATE)
    ct = lambda c: c.transpose(0, 2, 1).reshape(N_STATE, GROUP_A)
    rspec = pl.BlockSpec((1, S5_TILE_ST), lambda i: (0, i))
    bspec = pl.BlockSpec((S5_TILE_IN, S5_TILE_ST), lambda i: (0, i))
    bin_spec = pl.BlockSpec((GROUP_A, S5_TILE_ST), lambda i: (0, i))
    cin_spec = pl.BlockSpec((S5_TILE_ST, GROUP_A), lambda i: (i, 0))
    return pl.pallas_call(
        _s5prep_kernel,
        grid=(S5_TILES,),
        in_specs=[rspec, rspec, rspec, bin_spec, bin_spec, cin_spec, cin_spec],
        out_specs=[rspec, rspec, bspec, bspec,
                   pl.BlockSpec((None, 2 * S5_TILE_ST, S5_TILE_IN), lambda i: (i, 0, 0))],
        out_shape=[jax.ShapeDtypeStruct((1, N_STATE), F32), jax.ShapeDtypeStruct((1, N_STATE), F32),
                   jax.ShapeDtypeStruct((S5_TILE_IN, N_STATE), BF16),
                   jax.ShapeDtypeStruct((S5_TILE_IN, N_STATE), BF16),
                   jax.ShapeDtypeStruct((S5_TILES, 2 * S5_TILE_ST, S5_TILE_IN), BF16)],
        compiler_params=_cparams(1),
        name="s5_prep",
    )(row(lam_re), row(lam_im), row(ldt), bt(b_re), bt(b_im), ct(c_re), ct(c_im))


def _inproj_kernel(x_ref, shift_ref, scale_ref, gain_ref, w_ref, wt_ref, p_ref, pt_ref, h_scr,
                   *, tm, rc, rows_per_seq, seq_base):
    mt, n = pl.program_id(0), pl.program_id(1)

    @pl.when(n == 0)
    def _():
        gain = gain_ref[...]

        def body(i, carry):
            r0 = pl.multiple_of(i * rc, rc)
            row0 = mt * tm + r0
            xx = x_ref[pl.ds(r0, rc), :]
            y = xx * lax.rsqrt(jnp.mean(xx * xx, axis=-1, keepdims=True) + EPS)
            sc = _seq_rows(scale_ref, row0, rc, rows_per_seq, seq_base)
            sh = _seq_rows(shift_ref, row0, rc, rows_per_seq, seq_base)
            h_scr[pl.ds(r0, rc), :] = (y * (gain * (1.0 + sc)) + sh).astype(BF16)
            return carry

        lax.fori_loop(0, tm // rc, body, 0)
        pt_ref[...] = _dot_nt(h_scr[...], wt_ref[...].astype(BF16))

    p_ref[...] = _dot_nt(h_scr[...], w_ref[...].astype(BF16))


def _inproj_call(x2d, mod, gain, w_in_t, *, rows_per_seq, seq_base, tm, tn):
    tokens = x2d.shape[0]
    ns = mod.shape[0]
    rc = 128 if rows_per_seq >= 128 else 8 * rows_per_seq
    kern = functools.partial(_inproj_kernel, tm=tm, rc=rc, rows_per_seq=rows_per_seq, seq_base=seq_base)
    x_buffers = 1 if tokens // tm > 1 and tm * D_MODEL * 4 > 8 * 1024 * 1024 else 2
    return pl.pallas_call(
        kern,
        grid=(tokens // tm, PROJ_COLS // tn),
        in_specs=[pl.BlockSpec((tm, D_MODEL), lambda m, n: (m, 0), pipeline_mode=pl.Buffered(x_buffers)),
                  pl.BlockSpec((ns, D_MODEL), lambda m, n: (0, 0), pipeline_mode=pl.Buffered(1)),
                  pl.BlockSpec((ns, D_MODEL), lambda m, n: (0, 1), pipeline_mode=pl.Buffered(1)),
                  _const_spec((1, D_MODEL), 2),
                  pl.BlockSpec((tn, D_MODEL), lambda m, n: (n, 0)),
                  pl.BlockSpec((GATE_OFF, D_MODEL), lambda m, n: (PROJ_COLS // GATE_OFF, 0),
                               pipeline_mode=pl.Buffered(1))],
        out_specs=[pl.BlockSpec((tm, tn), lambda m, n: (m, n)),
                   pl.BlockSpec((tm, GATE_OFF), lambda m, n: (m, 0))],
        out_shape=[jax.ShapeDtypeStruct((tokens, PROJ_COLS), F32),
                   jax.ShapeDtypeStruct((tokens, GATE_OFF), F32)],
        scratch_shapes=[pltpu.VMEM((tm, D_MODEL), BF16)],
        compiler_params=_cparams(2),
        name="in_proj",
    )(x2d, mod, mod, gain, w_in_t, w_in_t)


def _s5_kernel(*refs, tc, n_sub, ls, has_h0):
    n_in = 12 if has_h0 else 10
    if has_h0:
        (u_ref, za_ref, h0re_ref, h0im_ref, wbre_ref, wbim_ref, wc_ref,
         are_ref, aim_ref, dskip_ref, wglu_ref, bglu_ref) = refs[:n_in]
    else:
        (u_ref, za_ref, wbre_ref, wbim_ref, wc_ref,
         are_ref, aim_ref, dskip_ref, wglu_ref, bglu_ref) = refs[:n_in]
    gated_ref, fre_ref, fim_ref = refs[n_in:n_in + 3]
    scr = refs[n_in + 3:n_in + 3 + 2 * n_sub * S5_TILES]
    xr = [scr[2 * c * S5_TILES:(2 * c + 1) * S5_TILES] for c in range(n_sub)]
    xi = [scr[(2 * c + 1) * S5_TILES:(2 * c + 2) * S5_TILES] for c in range(n_sub)]
    hc_re, hc_im = refs[n_in + 3 + 2 * n_sub * S5_TILES:]
    ct = pl.program_id(1)
    n_seq = tc // ls

    us = []
    for c in range(n_sub):
        u = u_ref[c * tc:(c + 1) * tc, :]
        us.append(u)
        ub = u.astype(BF16)
        for i in range(S5_TILES):
            ui = ub[:, i * S5_TILE_IN:(i + 1) * S5_TILE_IN]
            for w_ref, x_scr in ((wbre_ref, xr[c][i]), (wbim_ref, xi[c][i])):
                res = _dot(ui, w_ref[:, i * S5_TILE_ST:(i + 1) * S5_TILE_ST])
                for j in range(SUBLANES):
                    x_scr[pl.ds(j, tc, stride=SUBLANES), :] = res[:, j * LANES:(j + 1) * LANES]

    a_re = [are_ref[i] for i in range(S5_TILES)]
    a_im = [aim_ref[i] for i in range(S5_TILES)]
    if not has_h0:
        h_re = [jnp.where(ct == 0, 0.0, hc_re[i]) for i in range(S5_TILES)]
        h_im = [jnp.where(ct == 0, 0.0, hc_im[i]) for i in range(S5_TILES)]
    for c in range(n_sub):
        for s in range(n_seq):
            if has_h0:
                h_re = [h0re_ref[c * n_seq + s, i] for i in range(S5_TILES)]
                h_im = [h0im_ref[c * n_seq + s, i] for i in range(S5_TILES)]
            for tl in range(ls):
                r0 = (s * ls + tl) * SUBLANES
                for i in range(S5_TILES):
                    nr = a_re[i] * h_re[i] - a_im[i] * h_im[i] + xr[c][i][r0:r0 + SUBLANES, :]
                    ni = a_re[i] * h_im[i] + a_im[i] * h_re[i] + xi[c][i][r0:r0 + SUBLANES, :]
                    xr[c][i][r0:r0 + SUBLANES, :] = nr
                    xi[c][i][r0:r0 + SUBLANES, :] = ni
                    h_re[i], h_im[i] = nr, ni
            if has_h0:
                for i in range(S5_TILES):
                    fre_ref[c * n_seq + s, i] = h_re[i]
                    fim_ref[c * n_seq + s, i] = h_im[i]
    if not has_h0:
        for i in range(S5_TILES):
            hc_re[i] = h_re[i]
            hc_im[i] = h_im[i]
            fre_ref[0, i] = h_re[i]
            fim_ref[0, i] = h_im[i]

    for c in range(n_sub):
        ys = []
        for i in range(S5_TILES):
            parts = [x_scr[pl.ds(j, tc, stride=SUBLANES), :]
                     for x_scr in (xr[c][i], xi[c][i]) for j in range(SUBLANES)]
            ys.append(_dot(jnp.concatenate(parts, axis=1).astype(BF16), wc_ref[i]))
        y = jnp.concatenate(ys, axis=1) + dskip_ref[...] * us[c]
        g = jax.nn.gelu(y)
        a_y = g * jax.nn.sigmoid(_dot(g.astype(BF16), wglu_ref[...]) + bglu_ref[...])
        gated_ref[c * tc:(c + 1) * tc, :] = (a_y * jax.nn.silu(za_ref[c * tc:(c + 1) * tc, :])).astype(BF16)


def _s5_call(proj, prep, d_skip, w_glu, b_glu, h0, *, n_batch, seq_len, tc, n_sub):
    a_re, a_im, wb_re, wb_im, wc = prep
    tokens = proj.shape[0]
    has_h0 = h0 is not None
    tt = tc * n_sub
    if has_h0:
        ls = seq_len
        grid = (tokens // tt, 1)
        tok = lambda b, c: b
        seq_block = tt // ls
    else:
        ls = tc
        nct = seq_len // tt
        grid = (n_batch, nct)
        tok = lambda b, c: b * nct + c
        seq_block = 1
    const = lambda shape: _const_spec(shape, 2)
    st_spec = pl.BlockSpec((seq_block, S5_TILES, SUBLANES, LANES), lambda b, c: (b, 0, 0, 0))
    in_specs = [pl.BlockSpec((tt, W_A), lambda b, c: (tok(b, c), COL_UA // W_A)),
                pl.BlockSpec((tt, W_A), lambda b, c: (tok(b, c), COL_ZA // W_A))]
    args = [proj] * 2
    if has_h0:
        in_specs += [st_spec, st_spec]
        args += [h0[0], h0[1]]
    in_specs += [const((S5_TILE_IN, N_STATE)), const((S5_TILE_IN, N_STATE)),
                 const((S5_TILES, 2 * S5_TILE_ST, S5_TILE_IN)),
                 const((S5_TILES, SUBLANES, LANES)), const((S5_TILES, SUBLANES, LANES)),
                 const((1, W_A)), const((W_A, W_A)), const((1, W_A))]
    args += [wb_re, wb_im, wc, a_re, a_im, d_skip, w_glu, b_glu]
    n_seq_total = tokens // seq_len
    st_shape = jax.ShapeDtypeStruct((n_seq_total, S5_TILES, SUBLANES, LANES), F32)
    return pl.pallas_call(
        functools.partial(_s5_kernel, tc=tc, n_sub=n_sub, ls=ls, has_h0=has_h0),
        grid=grid,
        in_specs=in_specs,
        out_specs=[pl.BlockSpec((tt, W_A), lambda b, c: (tok(b, c), 0)), st_spec, st_spec],
        out_shape=[jax.ShapeDtypeStruct((tokens, W_A), BF16), st_shape, st_shape],
        scratch_shapes=([pltpu.VMEM((tc * SUBLANES, LANES), F32)] * (2 * n_sub * S5_TILES)
                        + [pltpu.VMEM((S5_TILES, SUBLANES, LANES), F32)] * 2),
        compiler_params=_cparams(2),
        name="s5_branch",
    )(*args)


def _gla_levels(t, ls):
    ms = []
    m = ls
    while m >= 1:
        ms.append(m)
        m //= 2
    idx = np.arange(t)
    mats = []
    for m in ms:
        if m == 1:
            continue
        same = (idx[:, None] // m) == (idx[None, :] // m)
        mats.append(same & (idx[None, :] <= idx[:, None]))
        mats.append(same)
    return ms, np.concatenate(mats, axis=0).astype(np.float32)


def _gla_kernel(*refs, t, n_sub, ls, has_s0, ms):
    if has_s0:
        (q_ref, k_ref, v_ref, zb_ref, ma0_ref, ma1_ref, ma2_ref, mb0_ref, mb1_ref, mb2_ref, glr_ref, ya_ref,
         s0_ref, lv_ref, wg_ref, bg_ref, gg_ref, waout_ref, wbout_ref, merged_ref, sfin_ref, s_scr) = refs
        assert n_sub == 1
    else:
        (q_ref, k_ref, v_ref, zb_ref, ma0_ref, ma1_ref, ma2_ref, mb0_ref, mb1_ref, mb2_ref, glr_ref, ya_ref,
         lv_ref, wg_ref, bg_ref, gg_ref, waout_ref, wbout_ref, merged_ref, sfin_ref, s_scr) = refs
        assert ls == t
    ct = pl.program_id(1)
    n_seq = t // ls

    ti = lax.broadcasted_iota(jnp.int32, (t, t), 0)
    si = lax.broadcasted_iota(jnp.int32, (t, t), 1)
    masks = {}
    for li in range(1, len(ms)):
        m = ms[li]
        masks[li] = ((_shr(ti, 2 * m) == _shr(si, 2 * m)) & ((_shr(ti, m) & 1) == 1) & ((_shr(si, m) & 1) == 0))
    rid = lax.broadcasted_iota(jnp.int32, (t, 1), 0)

    if not has_s0:
        states = [jnp.where(ct == 0, 0.0, s_scr[h]) for h in range(N_HEADS_B)]
    ga = jax.nn.log_sigmoid(_dot(glr_ref[...].astype(BF16), wg_ref[...]) + bg_ref[...]) * (1.0 / GATE_TAU)
    ga_b = ga.astype(BF16)
    ga_r = ga_b.astype(F32)
    e_all = _dot(lv_ref[...], jnp.concatenate([ga_b[c * t:(c + 1) * t] for c in range(n_sub)], axis=1))
    gated_rows = []
    for c in range(n_sub):
        rows = slice(c * t, (c + 1) * t)
        cols = slice(c * QK_W, (c + 1) * QK_W)
        cum = lambda li: e_all[2 * li * t:(2 * li + 1) * t, cols]
        tot = lambda li: e_all[(2 * li + 1) * t:(2 * li + 2) * t, cols]

        q, k, v = q_ref[rows, :], k_ref[rows, :], v_ref[rows, :]
        heads = []
        for h in range(N_HEADS_B):
            hs = slice(h * DK_B, (h + 1) * DK_B)
            qs = q[:, hs] * (DK_B ** -0.5)
            kh = k[:, hs]
            vh = v[:, h * DV_B:(h + 1) * DV_B]
            vb = vh.astype(BF16)
            b = cum(0)[:, hs]
            bend = tot(0)[:, hs]
            q_dec = (qs * jnp.exp(b)).astype(BF16)
            k_dec = kh * jnp.exp(bend - b)
            scores = jnp.zeros((t, t), F32)
            for li in range(1, len(ms)):
                if ms[li] == 1:
                    qm = (qs * jnp.exp(ga_r[rows, hs])).astype(BF16)
                    km = kh.astype(BF16)
                else:
                    eq = cum(li)[:, hs]
                    qm = (qs * jnp.exp(eq)).astype(BF16)
                    km = (kh * jnp.exp(tot(li)[:, hs] - eq)).astype(BF16)
                scores = jnp.where(masks[li], _dot_nt(qm, km), scores)
            o = _dot(scores.astype(BF16), vb) + jnp.sum(qs * kh, axis=-1, keepdims=True) * vh
            o_inter = []
            for i in range(n_seq):
                r0 = i * ls
                s_old = s0_ref[i, h] if has_s0 else states[h]
                o_inter.append(_dot(q_dec[r0:r0 + ls], s_old.astype(BF16)))
                kd = k_dec if n_seq == 1 else jnp.where((rid >= r0) & (rid < r0 + ls), k_dec, 0.0)
                d_end = jnp.exp(bend[r0:r0 + 1, :])
                d_col = jnp.broadcast_to(d_end, (DK_B, DK_B)).T
                s_new = jnp.concatenate([d_col] * (DV_B // DK_B), axis=1) * s_old + _dot_tn(kd.astype(BF16), vb)
                if has_s0:
                    sfin_ref[i, h] = s_new
                else:
                    states[h] = s_new
            o = o + (o_inter[0] if n_seq == 1 else jnp.concatenate(o_inter, axis=0))
            heads.append(o * lax.rsqrt(jnp.mean(o * o, axis=-1, keepdims=True) + EPS))
        b_y = jnp.concatenate(heads, axis=1) * gg_ref[...]
        gated_rows.append((b_y * jax.nn.silu(zb_ref[rows, :])).astype(BF16))
    if not has_s0:
        for h in range(N_HEADS_B):
            s_scr[h] = states[h]
            sfin_ref[0, h] = states[h]
    gated = gated_rows[0] if n_sub == 1 else jnp.concatenate(gated_rows, axis=0)
    merged = (_merge_gate(ma0_ref, ma1_ref, ma2_ref) * _dot(ya_ref[...], waout_ref[...])
              + _merge_gate(mb0_ref, mb1_ref, mb2_ref) * _dot(gated, wbout_ref[...]))
    merged_ref[...] = merged.astype(BF16)


def _gla_call(proj, proj_tail, y_a, w_gate, b_gate, g_gain, w_a_out, w_b_out, s0, *, n_batch, seq_len, t, n_sub):
    tokens = proj.shape[0]
    has_s0 = s0 is not None
    tt = t * n_sub
    if has_s0:
        ls = seq_len
        grid = (tokens // tt, 1)
        tok = lambda b, c: b
        seq_block = tt // ls
    else:
        ls = t
        nct = seq_len // tt
        grid = (n_batch, nct)
        tok = lambda b, c: b * nct + c
        seq_block = 1
    ms, lv = _gla_levels(t, ls)
    lv = jnp.asarray(lv, dtype=BF16)
    const = lambda shape: _const_spec(shape, 2)
    st_spec = pl.BlockSpec((seq_block, N_HEADS_B, DK_B, DV_B), lambda b, c: (b, 0, 0, 0))
    in_specs = [pl.BlockSpec((tt, QK_W), lambda b, c: (tok(b, c), COL_Q // QK_W)),
                pl.BlockSpec((tt, QK_W), lambda b, c: (tok(b, c), COL_K // QK_W)),
                pl.BlockSpec((tt, W_B), lambda b, c: (tok(b, c), COL_V // W_B)),
                pl.BlockSpec((tt, W_B), lambda b, c: (tok(b, c), COL_ZB // W_B)),
                pl.BlockSpec((tt, W_A), lambda b, c: (tok(b, c), COL_MA // W_A)),
                pl.BlockSpec((tt, W_A), lambda b, c: (tok(b, c), COL_MA // W_A + 1)),
                pl.BlockSpec((tt, LANES), lambda b, c: (tok(b, c), COL_MB // LANES)),
                pl.BlockSpec((tt, W_B), lambda b, c: (tok(b, c), COL_MB // W_B)),
                pl.BlockSpec((tt, W_B), lambda b, c: (tok(b, c), COL_MB // W_B + 1)),
                pl.BlockSpec((tt, GATE_OFF), lambda b, c: (tok(b, c), 0)),
                pl.BlockSpec((tt, LANES), lambda b, c: (tok(b, c), COL_G // LANES)),
                pl.BlockSpec((tt, W_A), lambda b, c: (tok(b, c), 0))]
    args = [proj, proj, proj, proj, proj, proj, proj, proj, proj, proj_tail, proj, y_a]
    if has_s0:
        in_specs.append(st_spec)
        args.append(s0)
    in_specs += [const(lv.shape), const((LANES, QK_W)), const((1, QK_W)), const((1, W_B)),
                 const((W_A, D_MODEL)), const((W_B, D_MODEL))]
    args += [lv, w_gate, b_gate, g_gain, w_a_out, w_b_out]
    n_seq_total = tokens // seq_len
    return pl.pallas_call(
        functools.partial(_gla_kernel, t=t, n_sub=n_sub, ls=ls, has_s0=has_s0, ms=tuple(ms)),
        grid=grid,
        in_specs=in_specs,
        out_specs=[pl.BlockSpec((tt, D_MODEL), lambda b, c: (tok(b, c), 0)), st_spec],
        out_shape=[jax.ShapeDtypeStruct((tokens, D_MODEL), BF16),
                   jax.ShapeDtypeStruct((n_seq_total, N_HEADS_B, DK_B, DV_B), F32)],
        scratch_shapes=[pltpu.VMEM((N_HEADS_B, DK_B, DV_B), F32)],
        compiler_params=_cparams(2),
        name="gla_branch",
    )(*args)


def _out_kernel(m_ref, x_ref, gate_ref, w_ref, fg_ref, y_ref, acc0, acc1,
                *, tm, rc, rows_per_seq, seq_base, n_tiles):
    s = pl.program_id(0)

    def matmul(acc):
        acc[...] = _dot(m_ref[...], w_ref[...])

    def epilogue(acc):
        fg = fg_ref[...]
        for i in range(tm // rc):
            r0 = i * rc
            gate = _seq_rows(gate_ref, (s - 1) * tm + r0, rc, rows_per_seq, seq_base)
            yy = x_ref[r0:r0 + rc, :] + gate * acc[r0:r0 + rc, :]
            y_ref[r0:r0 + rc, :] = yy * lax.rsqrt(jnp.mean(yy * yy, axis=-1, keepdims=True) + EPS) * fg

    accs = (acc0, acc1)

    pl.when(s == 0)(lambda: matmul(acc0))
    for par in (0, 1):
        @pl.when((s > 0) & (s < n_tiles) & (lax.rem(s, 2) == par))
        def _(par=par):
            matmul(accs[par])
            epilogue(accs[1 - par])

    pl.when(s == n_tiles)(lambda: epilogue(accs[(n_tiles - 1) % 2]))


def _out_call(merged, x2d, mod, w_out, fgain, *, rows_per_seq, seq_base, tm):
    tokens = x2d.shape[0]
    ns = mod.shape[0]
    n_tiles = tokens // tm
    rc = 128 if rows_per_seq >= 128 else 8 * rows_per_seq
    kern = functools.partial(_out_kernel, tm=tm, rc=rc, rows_per_seq=rows_per_seq, seq_base=seq_base,
                             n_tiles=n_tiles)
    prev = lambda m: (jnp.maximum(m - 1, 0), 0)
    return pl.pallas_call(
        kern,
        grid=(n_tiles + 1,),
        in_specs=[pl.BlockSpec((tm, D_MODEL), lambda m: (jnp.minimum(m, n_tiles - 1), 0)),
                  pl.BlockSpec((tm, D_MODEL), prev),
                  pl.BlockSpec((ns, D_MODEL), lambda m: (0, 2), pipeline_mode=pl.Buffered(1)),
                  _const_spec((D_MODEL, D_MODEL), 1),
                  _const_spec((1, D_MODEL), 1)],
        out_specs=pl.BlockSpec((tm, D_MODEL), prev),
        out_shape=jax.ShapeDtypeStruct((tokens, D_MODEL), F32),
        scratch_shapes=[pltpu.VMEM((tm, D_MODEL), F32), pltpu.VMEM((tm, D_MODEL), F32)],
        compiler_params=_cparams(1),
        name="out_proj",
    )(merged, x2d, mod, w_out, fgain)


def _tile(n, target):
    t = min(n, target)
    assert n % t == 0
    return t


def _layer(x, n_seq_before, mod, weights, prep, s0_ssm, s0_gla):
    (gain, w_in_t, d_skip, w_glu, b_glu, w_a_out, w_gate, b_gate, g_gain, w_b_out, w_out, fgain) = weights
    n_batch, seq_len, _ = x.shape
    tokens = n_batch * seq_len
    x2d = x.reshape(tokens, D_MODEL)
    long_seq = s0_ssm is None
    span = seq_len if long_seq else tokens
    proj, proj_tail = _inproj_call(x2d, mod, gain, w_in_t, rows_per_seq=seq_len, seq_base=n_seq_before,
                                   tm=_tile(span, 2048), tn=512 if long_seq else 1024)
    tc = _tile(span, 256)
    y_a, f_re, f_im = _s5_call(proj, prep, d_skip, w_glu, b_glu, s0_ssm, n_batch=n_batch, seq_len=seq_len,
                               tc=tc, n_sub=2 if span % (2 * tc) == 0 else 1)
    if long_seq:
        t = _tile(seq_len, 128)
        n_sub = 2 if seq_len % (2 * t) == 0 else 1
    else:
        t, n_sub = _tile(tokens, 128), 1
    merged, s_fin = _gla_call(proj, proj_tail, y_a, w_gate, b_gate, g_gain, w_a_out, w_b_out, s0_gla,
                              n_batch=n_batch, seq_len=seq_len, t=t, n_sub=n_sub)
    y = _out_call(merged, x2d, mod, w_out, fgain, rows_per_seq=seq_len, seq_base=n_seq_before,
                  tm=_tile(span, 512))
    st = lambda f: f.reshape(1, n_batch, N_GROUPS_A, P_STATE)
    return (y.reshape(n_batch, seq_len, D_MODEL), st(f_re), st(f_im),
            s_fin.reshape(1, n_batch, N_HEADS_B, DK_B, DV_B))


def kernel(x_prompt, x_sample, c_prompt, c_sample, state_ssm_re, state_ssm_im, state_gla, w_ada, b_ada, norm_gain, w_in, lambda_re, lambda_im, log_dt, ssm_b_re, ssm_b_im, ssm_c_re, ssm_c_im, d_skip, w_glu, b_glu, w_gate_up, b_gate, gla_norm_gain, w_a_out, w_b_out, w_out, final_norm_gain):
    assert w_ada.shape[0] == 1, "single-layer step"
    assert w_in.shape[2] == IN_COLS
    n_prompt, n_sample = x_prompt.shape[0], x_sample.shape[0]

    mod = _mod_call(jnp.concatenate([c_prompt, c_sample], axis=0), w_ada[0], b_ada)
    a_re, a_im, wb_re, wb_im, wc = _s5prep_call(lambda_re[0], lambda_im[0], log_dt[0], ssm_b_re[0], ssm_b_im[0],
                                                ssm_c_re[0], ssm_c_im[0])
    state_tiles = lambda a: a.reshape(-1, S5_TILES, SUBLANES, LANES)
    prep = (state_tiles(a_re)[0], state_tiles(a_im)[0], wb_re, wb_im, wc)

    w_in_t = jnp.swapaxes(w_in, 1, 2).reshape(IN_COLS, D_MODEL)
    w_gate = jnp.pad(w_gate_up[0], ((0, LANES - GATE_RANK), (0, 0))).astype(BF16)
    weights = (norm_gain, w_in_t, d_skip, w_glu[0].astype(BF16), b_glu, w_a_out[0].astype(BF16),
               w_gate, b_gate, gla_norm_gain, w_b_out[0].astype(BF16), w_out[0].astype(BF16),
               final_norm_gain.reshape(1, D_MODEL))

    y_p, pre, pim, pgla = _layer(x_prompt, 0, mod, weights, prep, None, None)
    y_s, sre, sim, sgla = _layer(x_sample, n_prompt, mod, weights, prep,
                                 (state_tiles(state_ssm_re[0]), state_tiles(state_ssm_im[0])), state_gla[0])
    return (y_p, y_s, pre, pim, pgla, sre, sim, sgla)
```

```python
import functools

import numpy as np
import jax
import jax.numpy as jnp
from jax import lax
from jax.experimental import pallas as pl
from jax.experimental.pallas import tpu as pltpu

F32 = jnp.float32
BF16 = jnp.bfloat16

D_MODEL = 2048
W_A = D_MODEL // 2
GROUP_A = 16
N_GROUPS_A = W_A // GROUP_A
P_STATE = 64
N_STATE = N_GROUPS_A * P_STATE
W_B = D_MODEL // 2
N_HEADS_B = 4
DK_B = W_B // 2 // N_HEADS_B
DV_B = W_B // N_HEADS_B
QK_W = N_HEADS_B * DK_B
GATE_RANK = 16
GATE_TAU = 16.0
EPS = 1e-6
LOG2_E = 1.4426950408889634

LANES = 128
SUBLANES = 8
V7X_VMEM_BYTES = 64 * 1024 * 1024
VMEM_LIMIT_BYTES = 56 * 1024 * 1024

S5_TILE_GROUPS = 16
S5_TILES = N_GROUPS_A // S5_TILE_GROUPS
S5_TILE_IN = S5_TILE_GROUPS * GROUP_A
S5_TILE_ST = S5_TILE_GROUPS * P_STATE
assert S5_TILE_ST == SUBLANES * LANES

COL_UA = 0
COL_ZA = COL_UA + W_A
COL_Q = COL_ZA + W_A
COL_K = COL_Q + QK_W
COL_V = COL_K + QK_W
COL_ZB = COL_V + W_B
COL_G = COL_ZB + W_B
COL_MA = COL_G + GATE_RANK
COL_MB = COL_MA + D_MODEL
IN_COLS = COL_MB + D_MODEL
PROJ_COLS = IN_COLS - GATE_RANK
PCOL_MA = COL_G
PCOL_MB = PCOL_MA + D_MODEL
assert COL_MA % SUBLANES == 0 and COL_G % GATE_RANK == 0


def _dot(a, b):
    return jnp.dot(a, b, preferred_element_type=F32)


def _dot_nt(a, b):
    return lax.dot_general(a, b, (((1,), (1,)), ((), ())), preferred_element_type=F32)


def _dot_tn(a, b):
    return lax.dot_general(a, b, (((0,), (0,)), ((), ())), preferred_element_type=F32)


def _shr(x, pow2):
    sh = int(pow2).bit_length() - 1
    assert 1 << sh == pow2
    return jnp.right_shift(x, sh)


def _cparams(n_axes):
    return pltpu.CompilerParams(dimension_semantics=("arbitrary",) * n_axes,
                                vmem_limit_bytes=VMEM_LIMIT_BYTES)


def _const_spec(shape, n_axes):
    zeros = (0,) * len(shape)
    index_map = (lambda a: zeros) if n_axes == 1 else (lambda a, b: zeros)
    return pl.BlockSpec(shape, index_map, pipeline_mode=pl.Buffered(1))


def _seq_rows(ref, row0, rc, rows_per_seq, seq_base):
    s = seq_base + lax.div(row0, jnp.int32(rows_per_seq))
    if rows_per_seq >= rc:
        return ref[pl.ds(s, 1), :]
    assert rc % rows_per_seq == 0 and rows_per_seq % SUBLANES == 0
    width = ref.shape[1]
    return jnp.concatenate([jnp.broadcast_to(ref[pl.ds(s + j, 1), :], (rows_per_seq, width))
                            for j in range(rc // rows_per_seq)], axis=0)


def _merge_gate(lo_ref, hi_ref):
    return jax.nn.sigmoid(jnp.concatenate([lo_ref[...], hi_ref[...]], axis=1))


def _mod_kernel(c_ref, w_ref, b_ref, o_ref):
    s = jax.nn.silu(c_ref[...]).astype(BF16)
    o_ref[...] = _dot(s, w_ref[...].astype(BF16)) + b_ref[...]


def _mod_call(c_all, w_ada, b_ada):
    ns = c_all.shape[0]
    tn = 1024
    return pl.pallas_call(
        _mod_kernel,
        grid=(3 * D_MODEL // tn,),
        in_specs=[_const_spec((ns, D_MODEL), 1),
                  pl.BlockSpec((D_MODEL, tn), lambda n: (0, n)),
                  pl.BlockSpec((1, tn), lambda n: (0, n))],
        out_specs=pl.BlockSpec((ns, tn), lambda n: (0, n)),
        out_shape=jax.ShapeDtypeStruct((ns, 3 * D_MODEL), F32),
        compiler_params=_cparams(1),
        name="adaln_mod",
    )(c_all, w_ada, b_ada)


def _s5prep_kernel(lr_ref, li_ref, ldt_ref, bre_ref, bim_ref, cre_ref, cim_ref,
                   are_ref, aim_ref, wbre_ref, wbim_ref, wc_ref):
    dt = jnp.exp(ldt_ref[...])
    lr, li = lr_ref[...], li_ref[...]
    mag = jnp.exp(lr * dt)
    ab_re, ab_im = mag * jnp.cos(li * dt), mag * jnp.sin(li * dt)
    nr, ni = ab_re - 1.0, ab_im
    den = lr * lr + li * li
    cf_re = (nr * lr + ni * li) / den
    cf_im = (ni * lr - nr * li) / den
    are_ref[...] = ab_re
    aim_ref[...] = ab_im
    r = lax.broadcasted_iota(jnp.int32, (S5_TILE_IN, S5_TILE_ST), 0)
    c = lax.broadcasted_iota(jnp.int32, (S5_TILE_IN, S5_TILE_ST), 1)
    on_b = _shr(r, GROUP_A) == _shr(c, P_STATE)
    bre, bim = bre_ref[...], bim_ref[...]
    rep = lambda a: jnp.concatenate([a] * S5_TILE_GROUPS, axis=0)
    wbre_ref[...] = jnp.where(on_b, rep(cf_re * bre - cf_im * bim), 0.0).astype(BF16)
    wbim_ref[...] = jnp.where(on_b, rep(cf_re * bim + cf_im * bre), 0.0).astype(BF16)
    h = lax.broadcasted_iota(jnp.int32, (GROUP_A, S5_TILE_IN), 0)
    c = lax.broadcasted_iota(jnp.int32, (GROUP_A, S5_TILE_IN), 1)
    spread = jnp.where((c & (GROUP_A - 1)) == h, 1.0, 0.0).astype(BF16)
    r = lax.broadcasted_iota(jnp.int32, (S5_TILE_ST, S5_TILE_IN), 0)
    c = lax.broadcasted_iota(jnp.int32, (S5_TILE_ST, S5_TILE_IN), 1)
    on_c = _shr(r, P_STATE) == _shr(c, GROUP_A)
    wc_ref[0:S5_TILE_ST, :] = jnp.where(on_c, _dot(cre_ref[...].astype(BF16), spread), 0.0).astype(BF16)
    wc_ref[S5_TILE_ST:2 * S5_TILE_ST, :] = jnp.where(on_c, -_dot(cim_ref[...].astype(BF16), spread), 0.0).astype(BF16)


def _s5prep_call(lam_re, lam_im, log_dt, b_re, b_im, c_re, c_im):
    row = lambda a: a.reshape(1, N_STATE)
    ldt = jnp.broadcast_to(log_dt[:, None], (N_GROUPS_A, P_STATE))
    bt = lambda b: b.transpose(2, 0, 1).reshape(GROUP_A, N_STATE)
    ct = lambda c: c.transpose(0, 2, 1).reshape(N_STATE, GROUP_A)
    rspec = pl.BlockSpec((1, S5_TILE_ST), lambda i: (0, i))
    bspec = pl.BlockSpec((S5_TILE_IN, S5_TILE_ST), lambda i: (0, i))
    bin_spec = pl.BlockSpec((GROUP_A, S5_TILE_ST), lambda i: (0, i))
    cin_spec = pl.BlockSpec((S5_TILE_ST, GROUP_A), lambda i: (i, 0))
    return pl.pallas_call(
        _s5prep_kernel,
        grid=(S5_TILES,),
        in_specs=[rspec, rspec, rspec, bin_spec, bin_spec, cin_spec, cin_spec],
        out_specs=[rspec, rspec, bspec, bspec,
                   pl.BlockSpec((None, 2 * S5_TILE_ST, S5_TILE_IN), lambda i: (i, 0, 0))],
        out_shape=[jax.ShapeDtypeStruct((1, N_STATE), F32), jax.ShapeDtypeStruct((1, N_STATE), F32),
                   jax.ShapeDtypeStruct((S5_TILE_IN, N_STATE), BF16),
                   jax.ShapeDtypeStruct((S5_TILE_IN, N_STATE), BF16),
                   jax.ShapeDtypeStruct((S5_TILES, 2 * S5_TILE_ST, S5_TILE_IN), BF16)],
        compiler_params=_cparams(1),
        name="s5_prep",
    )(row(lam_re), row(lam_im), row(ldt), bt(b_re), bt(b_im), ct(c_re), ct(c_im))


def _inproj_kernel(x_ref, shift_ref, scale_ref, gain_ref, w_ref, wt_ref, p_ref, pt_ref, h_scr,
                   *, tm, rc, rows_per_seq, seq_base):
    mt, n = pl.program_id(0), pl.program_id(1)

    @pl.when(n == 0)
    def _():
        gain = gain_ref[...]

        def body(i, carry):
            r0 = pl.multiple_of(i * rc, rc)
            row0 = mt * tm + r0
            xx = x_ref[pl.ds(r0, rc), :]
            y = xx * lax.rsqrt(jnp.mean(xx * xx, axis=-1, keepdims=True) + EPS)
            sc = _seq_rows(scale_ref, row0, rc, rows_per_seq, seq_base)
            sh = _seq_rows(shift_ref, row0, rc, rows_per_seq, seq_base)
            h_scr[pl.ds(r0, rc), :] = (y * (gain * (1.0 + sc)) + sh).astype(BF16)
            return carry

        lax.fori_loop(0, tm // rc, body, 0)
        pt_ref[...] = _dot_nt(h_scr[...], wt_ref[...].astype(BF16))

    p_ref[...] = _dot_nt(h_scr[...], w_ref[...].astype(BF16))


def _inproj_call(x2d, mod, gain, w_in_t, *, rows_per_seq, seq_base, tm, tn):
    tokens = x2d.shape[0]
    ns = mod.shape[0]
    rc = 128 if rows_per_seq >= 128 else 8 * rows_per_seq
    kern = functools.partial(_inproj_kernel, tm=tm, rc=rc, rows_per_seq=rows_per_seq, seq_base=seq_base)
    x_buffers = 1 if tokens // tm > 1 and tm * D_MODEL * 4 > 8 * 1024 * 1024 else 2
    assert COL_G % tn == 0
    n_before = COL_G // tn

    def w_rows(m, n):
        tiles = jnp.where(n < n_before, n * (tn // SUBLANES), COL_MA // SUBLANES + (n - n_before) * (tn // SUBLANES))
        return (tiles * SUBLANES, 0)

    return pl.pallas_call(
        kern,
        grid=(tokens // tm, PROJ_COLS // tn),
        in_specs=[pl.BlockSpec((tm, D_MODEL), lambda m, n: (m, 0), pipeline_mode=pl.Buffered(x_buffers)),
                  pl.BlockSpec((ns, D_MODEL), lambda m, n: (0, 0), pipeline_mode=pl.Buffered(1)),
                  pl.BlockSpec((ns, D_MODEL), lambda m, n: (0, 1), pipeline_mode=pl.Buffered(1)),
                  _const_spec((1, D_MODEL), 2),
                  pl.BlockSpec((pl.Element(tn), pl.Element(D_MODEL)), w_rows),
                  pl.BlockSpec((GATE_RANK, D_MODEL), lambda m, n: (COL_G // GATE_RANK, 0),
                               pipeline_mode=pl.Buffered(1))],
        out_specs=[pl.BlockSpec((tm, tn), lambda m, n: (m, n)),
                   pl.BlockSpec((tm, GATE_RANK), lambda m, n: (m, 0))],
        out_shape=[jax.ShapeDtypeStruct((tokens, PROJ_COLS), F32),
                   jax.ShapeDtypeStruct((tokens, GATE_RANK), F32)],
        scratch_shapes=[pltpu.VMEM((tm, D_MODEL), BF16)],
        compiler_params=_cparams(2),
        name="in_proj",
    )(x2d, mod, mod, gain, w_in_t, w_in_t)


def _s5_kernel(*refs, tc, n_sub, ls, has_h0):
    n_in = 12 if has_h0 else 10
    if has_h0:
        (u_ref, za_ref, h0re_ref, h0im_ref, wbre_ref, wbim_ref, wc_ref,
         are_ref, aim_ref, dskip_ref, wglu_ref, bglu_ref) = refs[:n_in]
    else:
        (u_ref, za_ref, wbre_ref, wbim_ref, wc_ref,
         are_ref, aim_ref, dskip_ref, wglu_ref, bglu_ref) = refs[:n_in]
    gated_ref, fre_ref, fim_ref = refs[n_in:n_in + 3]
    scr = refs[n_in + 3:n_in + 3 + 2 * n_sub * S5_TILES]
    xr = [scr[2 * c * S5_TILES:(2 * c + 1) * S5_TILES] for c in range(n_sub)]
    xi = [scr[(2 * c + 1) * S5_TILES:(2 * c + 2) * S5_TILES] for c in range(n_sub)]
    hc_re, hc_im = refs[n_in + 3 + 2 * n_sub * S5_TILES:]
    ct = pl.program_id(1)
    n_seq = tc // ls

    us = []
    for c in range(n_sub):
        u = u_ref[c * tc:(c + 1) * tc, :]
        us.append(u)
        ub = u.astype(BF16)
        for i in range(S5_TILES):
            ui = ub[:, i * S5_TILE_IN:(i + 1) * S5_TILE_IN]
            for w_ref, x_scr in ((wbre_ref, xr[c][i]), (wbim_ref, xi[c][i])):
                res = _dot(ui, w_ref[:, i * S5_TILE_ST:(i + 1) * S5_TILE_ST])
                for j in range(SUBLANES):
                    x_scr[pl.ds(j, tc, stride=SUBLANES), :] = res[:, j * LANES:(j + 1) * LANES]

    a_re = [are_ref[i] for i in range(S5_TILES)]
    a_im = [aim_ref[i] for i in range(S5_TILES)]
    if not has_h0:
        h_re = [jnp.where(ct == 0, 0.0, hc_re[i]) for i in range(S5_TILES)]
        h_im = [jnp.where(ct == 0, 0.0, hc_im[i]) for i in range(S5_TILES)]
    for c in range(n_sub):
        for s in range(n_seq):
            if has_h0:
                h_re = [h0re_ref[c * n_seq + s, i] for i in range(S5_TILES)]
                h_im = [h0im_ref[c * n_seq + s, i] for i in range(S5_TILES)]
            for tl in range(ls):
                r0 = (s * ls + tl) * SUBLANES
                for i in range(S5_TILES):
                    nr = a_re[i] * h_re[i] - a_im[i] * h_im[i] + xr[c][i][r0:r0 + SUBLANES, :]
                    ni = a_re[i] * h_im[i] + a_im[i] * h_re[i] + xi[c][i][r0:r0 + SUBLANES, :]
                    xr[c][i][r0:r0 + SUBLANES, :] = nr
                    xi[c][i][r0:r0 + SUBLANES, :] = ni
                    h_re[i], h_im[i] = nr, ni
            if has_h0:
                for i in range(S5_TILES):
                    fre_ref[c * n_seq + s, i] = h_re[i]
                    fim_ref[c * n_seq + s, i] = h_im[i]
    if not has_h0:
        for i in range(S5_TILES):
            hc_re[i] = h_re[i]
            hc_im[i] = h_im[i]
            fre_ref[0, i] = h_re[i]
            fim_ref[0, i] = h_im[i]

    for c in range(n_sub):
        ys = []
        for i in range(S5_TILES):
            parts = [x_scr[pl.ds(j, tc, stride=SUBLANES), :]
                     for x_scr in (xr[c][i], xi[c][i]) for j in range(SUBLANES)]
            ys.append(_dot(jnp.concatenate(parts, axis=1).astype(BF16), wc_ref[i]))
        y = jnp.concatenate(ys, axis=1) + dskip_ref[...] * us[c]
        g = jax.nn.gelu(y)
        a_y = g * jax.nn.sigmoid(_dot(g.astype(BF16), wglu_ref[...]) + bglu_ref[...])
        gated_ref[c * tc:(c + 1) * tc, :] = (a_y * jax.nn.silu(za_ref[c * tc:(c + 1) * tc, :])).astype(BF16)


def _s5_call(proj, prep, d_skip, w_glu, b_glu, h0, *, n_batch, seq_len, tc, n_sub):
    a_re, a_im, wb_re, wb_im, wc = prep
    tokens = proj.shape[0]
    has_h0 = h0 is not None
    tt = tc * n_sub
    if has_h0:
        ls = seq_len
        grid = (tokens // tt, 1)
        tok = lambda b, c: b
        seq_block = tt // ls
    else:
        ls = tc
        nct = seq_len // tt
        grid = (n_batch, nct)
        tok = lambda b, c: b * nct + c
        seq_block = 1
    const = lambda shape: _const_spec(shape, 2)
    st_spec = pl.BlockSpec((seq_block, S5_TILES, SUBLANES, LANES), lambda b, c: (b, 0, 0, 0))
    in_specs = [pl.BlockSpec((tt, W_A), lambda b, c: (tok(b, c), COL_UA // W_A)),
                pl.BlockSpec((tt, W_A), lambda b, c: (tok(b, c), COL_ZA // W_A))]
    args = [proj] * 2
    if has_h0:
        in_specs += [st_spec, st_spec]
        args += [h0[0], h0[1]]
    in_specs += [const((S5_TILE_IN, N_STATE)), const((S5_TILE_IN, N_STATE)),
                 const((S5_TILES, 2 * S5_TILE_ST, S5_TILE_IN)),
                 const((S5_TILES, SUBLANES, LANES)), const((S5_TILES, SUBLANES, LANES)),
                 const((1, W_A)), const((W_A, W_A)), const((1, W_A))]
    args += [wb_re, wb_im, wc, a_re, a_im, d_skip, w_glu, b_glu]
    n_seq_total = tokens // seq_len
    st_shape = jax.ShapeDtypeStruct((n_seq_total, S5_TILES, SUBLANES, LANES), F32)
    return pl.pallas_call(
        functools.partial(_s5_kernel, tc=tc, n_sub=n_sub, ls=ls, has_h0=has_h0),
        grid=grid,
        in_specs=in_specs,
        out_specs=[pl.BlockSpec((tt, W_A), lambda b, c: (tok(b, c), 0)), st_spec, st_spec],
        out_shape=[jax.ShapeDtypeStruct((tokens, W_A), BF16), st_shape, st_shape],
        scratch_shapes=([pltpu.VMEM((tc * SUBLANES, LANES), F32)] * (2 * n_sub * S5_TILES)
                        + [pltpu.VMEM((S5_TILES, SUBLANES, LANES), F32)] * 2),
        compiler_params=_cparams(2),
        name="s5_branch",
    )(*args)


def _gla_levels(t, ls):
    ms = []
    m = ls
    while m >= 1:
        ms.append(m)
        m //= 2
    idx = np.arange(t)
    mats = []
    for m in ms:
        if m == 1:
            continue
        same = (idx[:, None] // m) == (idx[None, :] // m)
        mats.append(same & (idx[None, :] <= idx[:, None]))
        mats.append(same)
    return ms, np.concatenate(mats, axis=0).astype(np.float32)


def _gla_kernel(*refs, t, n_sub, ls, has_s0, ms):
    if has_s0:
        (q_ref, k_ref, v_ref, zb_ref, ma0_ref, ma1_ref, mb0_ref, mb1_ref, glr_ref, ya_ref,
         s0_ref, lv_ref, wg_ref, bg_ref, gg_ref, waout_ref, wbout_ref, merged_ref, sfin_ref, s_scr) = refs
        assert n_sub == 1
    else:
        (q_ref, k_ref, v_ref, zb_ref, ma0_ref, ma1_ref, mb0_ref, mb1_ref, glr_ref, ya_ref,
         lv_ref, wg_ref, bg_ref, gg_ref, waout_ref, wbout_ref, merged_ref, sfin_ref, s_scr) = refs
        assert ls == t
    ct = pl.program_id(1)
    n_seq = t // ls

    ti = lax.broadcasted_iota(jnp.int32, (t, t), 0)
    si = lax.broadcasted_iota(jnp.int32, (t, t), 1)
    masks = {}
    for li in range(1, len(ms)):
        m = ms[li]
        masks[li] = ((_shr(ti, 2 * m) == _shr(si, 2 * m)) & ((_shr(ti, m) & 1) == 1) & ((_shr(si, m) & 1) == 0))
    rid = lax.broadcasted_iota(jnp.int32, (t, 1), 0)

    if not has_s0:
        states = [jnp.where(ct == 0, 0.0, s_scr[h]) for h in range(N_HEADS_B)]
    ga = jax.nn.log_sigmoid(_dot(glr_ref[...].astype(BF16), wg_ref[...]) + bg_ref[...]) * (LOG2_E / GATE_TAU)
    ga_b = ga.astype(BF16)
    ga_r = ga_b.astype(F32)
    e_all = _dot(lv_ref[...], jnp.concatenate([ga_b[c * t:(c + 1) * t] for c in range(n_sub)], axis=1))
    gated_rows = []
    for c in range(n_sub):
        rows = slice(c * t, (c + 1) * t)
        cols = slice(c * QK_W, (c + 1) * QK_W)
        cum = lambda li: e_all[2 * li * t:(2 * li + 1) * t, cols]
        tot = lambda li: e_all[(2 * li + 1) * t:(2 * li + 2) * t, cols]

        q, k, v = q_ref[rows, :], k_ref[rows, :], v_ref[rows, :]
        heads = []
        for h in range(N_HEADS_B):
            hs = slice(h * DK_B, (h + 1) * DK_B)
            qs = q[:, hs] * (DK_B ** -0.5)
            kh = k[:, hs]
            vh = v[:, h * DV_B:(h + 1) * DV_B]
            vb = vh.astype(BF16)
            b = cum(0)[:, hs]
            bend = tot(0)[:, hs]
            q_dec = (qs * jnp.exp2(b)).astype(BF16)
            k_dec = kh * jnp.exp2(bend - b)
            scores = jnp.zeros((t, t), F32)
            for li in range(1, len(ms)):
                if ms[li] == 1:
                    qm = (qs * jnp.exp2(ga_r[rows, hs])).astype(BF16)
                    km = kh.astype(BF16)
                else:
                    eq = cum(li)[:, hs]
                    qm = (qs * jnp.exp2(eq)).astype(BF16)
                    km = (kh * jnp.exp2(tot(li)[:, hs] - eq)).astype(BF16)
                scores = jnp.where(masks[li], _dot_nt(qm, km), scores)
            o = _dot(scores.astype(BF16), vb) + jnp.sum(qs * kh, axis=-1, keepdims=True) * vh
            o_inter = []
            for i in range(n_seq):
                r0 = i * ls
                s_old = s0_ref[i, h] if has_s0 else states[h]
                o_inter.append(_dot(q_dec[r0:r0 + ls], s_old.astype(BF16)))
                kd = k_dec if n_seq == 1 else jnp.where((rid >= r0) & (rid < r0 + ls), k_dec, 0.0)
                d_end = jnp.exp2(bend[r0:r0 + 1, :])
                d_col = jnp.broadcast_to(d_end, (DK_B, DK_B)).T
                s_new = jnp.concatenate([d_col] * (DV_B // DK_B), axis=1) * s_old + _dot_tn(kd.astype(BF16), vb)
                if has_s0:
                    sfin_ref[i, h] = s_new
                else:
                    states[h] = s_new
            o = o + (o_inter[0] if n_seq == 1 else jnp.concatenate(o_inter, axis=0))
            heads.append(o * lax.rsqrt(jnp.mean(o * o, axis=-1, keepdims=True) + EPS))
        b_y = jnp.concatenate(heads, axis=1) * gg_ref[...]
        gated_rows.append((b_y * jax.nn.silu(zb_ref[rows, :])).astype(BF16))
    if not has_s0:
        for h in range(N_HEADS_B):
            s_scr[h] = states[h]
            sfin_ref[0, h] = states[h]
    gated = gated_rows[0] if n_sub == 1 else jnp.concatenate(gated_rows, axis=0)
    merged = (_merge_gate(ma0_ref, ma1_ref) * _dot(ya_ref[...], waout_ref[...])
              + _merge_gate(mb0_ref, mb1_ref) * _dot(gated, wbout_ref[...]))
    merged_ref[...] = merged.astype(BF16)


def _gla_call(proj, proj_g, y_a, w_gate, b_gate, g_gain, w_a_out, w_b_out, s0, *, n_batch, seq_len, t, n_sub):
    tokens = proj.shape[0]
    has_s0 = s0 is not None
    tt = t * n_sub
    if has_s0:
        ls = seq_len
        grid = (tokens // tt, 1)
        tok = lambda b, c: b
        seq_block = tt // ls
    else:
        ls = t
        nct = seq_len // tt
        grid = (n_batch, nct)
        tok = lambda b, c: b * nct + c
        seq_block = 1
    ms, lv = _gla_levels(t, ls)
    lv = jnp.asarray(lv, dtype=BF16)
    const = lambda shape: _const_spec(shape, 2)
    st_spec = pl.BlockSpec((seq_block, N_HEADS_B, DK_B, DV_B), lambda b, c: (b, 0, 0, 0))
    in_specs = [pl.BlockSpec((tt, QK_W), lambda b, c: (tok(b, c), COL_Q // QK_W)),
                pl.BlockSpec((tt, QK_W), lambda b, c: (tok(b, c), COL_K // QK_W)),
                pl.BlockSpec((tt, W_B), lambda b, c: (tok(b, c), COL_V // W_B)),
                pl.BlockSpec((tt, W_B), lambda b, c: (tok(b, c), COL_ZB // W_B)),
                pl.BlockSpec((tt, W_A), lambda b, c: (tok(b, c), PCOL_MA // W_A)),
                pl.BlockSpec((tt, W_A), lambda b, c: (tok(b, c), PCOL_MA // W_A + 1)),
                pl.BlockSpec((tt, W_B), lambda b, c: (tok(b, c), PCOL_MB // W_B)),
                pl.BlockSpec((tt, W_B), lambda b, c: (tok(b, c), PCOL_MB // W_B + 1)),
                pl.BlockSpec((tt, GATE_RANK), lambda b, c: (tok(b, c), 0)),
                pl.BlockSpec((tt, W_A), lambda b, c: (tok(b, c), 0))]
    args = [proj, proj, proj, proj, proj, proj, proj, proj, proj_g, y_a]
    if has_s0:
        in_specs.append(st_spec)
        args.append(s0)
    in_specs += [const(lv.shape), const((GATE_RANK, QK_W)), const((1, QK_W)), const((1, W_B)),
                 const((W_A, D_MODEL)), const((W_B, D_MODEL))]
    args += [lv, w_gate, b_gate, g_gain, w_a_out, w_b_out]
    n_seq_total = tokens // seq_len
    return pl.pallas_call(
        functools.partial(_gla_kernel, t=t, n_sub=n_sub, ls=ls, has_s0=has_s0, ms=tuple(ms)),
        grid=grid,
        in_specs=in_specs,
        out_specs=[pl.BlockSpec((tt, D_MODEL), lambda b, c: (tok(b, c), 0)), st_spec],
        out_shape=[jax.ShapeDtypeStruct((tokens, D_MODEL), BF16),
                   jax.ShapeDtypeStruct((n_seq_total, N_HEADS_B, DK_B, DV_B), F32)],
        scratch_shapes=[pltpu.VMEM((N_HEADS_B, DK_B, DV_B), F32)],
        compiler_params=_cparams(2),
        name="gla_branch",
    )(*args)


def _out_kernel(m_ref, x_ref, gate_ref, w_ref, fg_ref, y_ref, acc0, acc1,
                *, tm, rc, rows_per_seq, seq_base, n_tiles):
    s = pl.program_id(0)

    def matmul(acc):
        acc[...] = _dot(m_ref[...], w_ref[...])

    def epilogue(acc):
        fg = fg_ref[...]
        for i in range(tm // rc):
            r0 = i * rc
            gate = _seq_rows(gate_ref, (s - 1) * tm + r0, rc, rows_per_seq, seq_base)
            yy = x_ref[r0:r0 + rc, :] + gate * acc[r0:r0 + rc, :]
            y_ref[r0:r0 + rc, :] = yy * lax.rsqrt(jnp.mean(yy * yy, axis=-1, keepdims=True) + EPS) * fg

    accs = (acc0, acc1)

    pl.when(s == 0)(lambda: matmul(acc0))
    for par in (0, 1):
        @pl.when((s > 0) & (s < n_tiles) & (lax.rem(s, 2) == par))
        def _(par=par):
            matmul(accs[par])
            epilogue(accs[1 - par])

    pl.when(s == n_tiles)(lambda: epilogue(accs[(n_tiles - 1) % 2]))


def _out_call(merged, x2d, mod, w_out, fgain, *, rows_per_seq, seq_base, tm):
    tokens = x2d.shape[0]
    ns = mod.shape[0]
    n_tiles = tokens // tm
    rc = 128 if rows_per_seq >= 128 else 8 * rows_per_seq
    kern = functools.partial(_out_kernel, tm=tm, rc=rc, rows_per_seq=rows_per_seq, seq_base=seq_base,
                             n_tiles=n_tiles)
    prev = lambda m: (jnp.maximum(m - 1, 0), 0)
    return pl.pallas_call(
        kern,
        grid=(n_tiles + 1,),
        in_specs=[pl.BlockSpec((tm, D_MODEL), lambda m: (jnp.minimum(m, n_tiles - 1), 0)),
                  pl.BlockSpec((tm, D_MODEL), prev),
                  pl.BlockSpec((ns, D_MODEL), lambda m: (0, 2), pipeline_mode=pl.Buffered(1)),
                  _const_spec((D_MODEL, D_MODEL), 1),
                  _const_spec((1, D_MODEL), 1)],
        out_specs=pl.BlockSpec((tm, D_MODEL), prev),
        out_shape=jax.ShapeDtypeStruct((tokens, D_MODEL), F32),
        scratch_shapes=[pltpu.VMEM((tm, D_MODEL), F32), pltpu.VMEM((tm, D_MODEL), F32)],
        compiler_params=_cparams(1),
        name="out_proj",
    )(merged, x2d, mod, w_out, fgain)


def _tile(n, target):
    t = min(n, target)
    assert n % t == 0
    return t


def _layer(x, n_seq_before, mod, weights, prep, s0_ssm, s0_gla):
    (gain, w_in_t, d_skip, w_glu, b_glu, w_a_out, w_gate, b_gate, g_gain, w_b_out, w_out, fgain) = weights
    n_batch, seq_len, _ = x.shape
    tokens = n_batch * seq_len
    x2d = x.reshape(tokens, D_MODEL)
    long_seq = s0_ssm is None
    span = seq_len if long_seq else tokens
    proj, proj_g = _inproj_call(x2d, mod, gain, w_in_t, rows_per_seq=seq_len, seq_base=n_seq_before,
                                   tm=_tile(span, 2048), tn=512 if long_seq else 1024)
    tc = _tile(span, 256)
    y_a, f_re, f_im = _s5_call(proj, prep, d_skip, w_glu, b_glu, s0_ssm, n_batch=n_batch, seq_len=seq_len,
                               tc=tc, n_sub=2 if span % (2 * tc) == 0 else 1)
    if long_seq:
        t = _tile(seq_len, 128)
        n_sub = 2 if seq_len % (2 * t) == 0 else 1
    else:
        t, n_sub = _tile(tokens, 128), 1
    merged, s_fin = _gla_call(proj, proj_g, y_a, w_gate, b_gate, g_gain, w_a_out, w_b_out, s0_gla,
                              n_batch=n_batch, seq_len=seq_len, t=t, n_sub=n_sub)
    y = _out_call(merged, x2d, mod, w_out, fgain, rows_per_seq=seq_len, seq_base=n_seq_before,
                  tm=_tile(span, 512))
    st = lambda f: f.reshape(1, n_batch, N_GROUPS_A, P_STATE)
    return (y.reshape(n_batch, seq_len, D_MODEL), st(f_re), st(f_im),
            s_fin.reshape(1, n_batch, N_HEADS_B, DK_B, DV_B))


def kernel(x_prompt, x_sample, c_prompt, c_sample, state_ssm_re, state_ssm_im, state_gla, w_ada, b_ada, norm_gain, w_in, lambda_re, lambda_im, log_dt, ssm_b_re, ssm_b_im, ssm_c_re, ssm_c_im, d_skip, w_glu, b_glu, w_gate_up, b_gate, gla_norm_gain, w_a_out, w_b_out, w_out, final_norm_gain):
    assert w_ada.shape[0] == 1, "single-layer step"
    assert w_in.shape[2] == IN_COLS
    n_prompt, n_sample = x_prompt.shape[0], x_sample.shape[0]

    mod = _mod_call(jnp.concatenate([c_prompt, c_sample], axis=0), w_ada[0], b_ada)
    a_re, a_im, wb_re, wb_im, wc = _s5prep_call(lambda_re[0], lambda_im[0], log_dt[0], ssm_b_re[0], ssm_b_im[0],
                                                ssm_c_re[0], ssm_c_im[0])
    state_tiles = lambda a: a.reshape(-1, S5_TILES, SUBLANES, LANES)
    prep = (state_tiles(a_re)[0], state_tiles(a_im)[0], wb_re, wb_im, wc)

    w_in_t = jnp.swapaxes(w_in, 1, 2).reshape(IN_COLS, D_MODEL)
    w_gate = w_gate_up[0].astype(BF16)
    weights = (norm_gain, w_in_t, d_skip, w_glu[0].astype(BF16), b_glu, w_a_out[0].astype(BF16),
               w_gate, b_gate, gla_norm_gain, w_b_out[0].astype(BF16), w_out[0].astype(BF16),
               final_norm_gain.reshape(1, D_MODEL))

    y_p, pre, pim, pgla = _layer(x_prompt, 0, mod, weights, prep, None, None)
    y_s, sre, sim, sgla = _layer(x_sample, n_prompt, mod, weights, prep,
                                 (state_tiles(state_ssm_re[0]), state_tiles(state_ssm_im[0])), state_gla[0])
    return (y_p, y_s, pre, pim, pgla, sre, sim, sgla)
```

```python
import functools

import numpy as np
import jax
import jax.numpy as jnp
from jax import lax
from jax.experimental import pallas as pl
from jax.experimental.pallas import tpu as pltpu

F32 = jnp.float32
BF16 = jnp.bfloat16

D_MODEL = 2048
W_A = D_MODEL // 2
GROUP_A = 16
N_GROUPS_A = W_A // GROUP_A
P_STATE = 64
N_STATE = N_GROUPS_A * P_STATE
W_B = D_MODEL // 2
N_HEADS_B = 4
DK_B = W_B // 2 // N_HEADS_B
DV_B = W_B // N_HEADS_B
QK_W = N_HEADS_B * DK_B
GATE_RANK = 16
GATE_TAU = 16.0
EPS = 1e-6
LOG2_E = 1.4426950408889634

LANES = 128
SUBLANES = 8
V7X_VMEM_BYTES = 64 * 1024 * 1024
VMEM_LIMIT_BYTES = 56 * 1024 * 1024

S5_TILE_GROUPS = 16
S5_TILES = N_GROUPS_A // S5_TILE_GROUPS
S5_TILE_IN = S5_TILE_GROUPS * GROUP_A
S5_TILE_ST = S5_TILE_GROUPS * P_STATE
assert S5_TILE_ST == SUBLANES * LANES

COL_UA = 0
COL_ZA = COL_UA + W_A
COL_Q = COL_ZA + W_A
COL_K = COL_Q + QK_W
COL_V = COL_K + QK_W
COL_ZB = COL_V + W_B
COL_G = COL_ZB + W_B
COL_MA = COL_G + GATE_RANK
COL_MB = COL_MA + D_MODEL
IN_COLS = COL_MB + D_MODEL
PROJ_COLS = IN_COLS - GATE_RANK
PCOL_MA = COL_G
PCOL_MB = PCOL_MA + D_MODEL
assert COL_MA % SUBLANES == 0 and COL_G % GATE_RANK == 0


def _dot(a, b):
    return jnp.dot(a, b, preferred_element_type=F32)


def _dot_nt(a, b):
    return lax.dot_general(a, b, (((1,), (1,)), ((), ())), preferred_element_type=F32)


def _dot_tn(a, b):
    return lax.dot_general(a, b, (((0,), (0,)), ((), ())), preferred_element_type=F32)


def _shr(x, pow2):
    sh = int(pow2).bit_length() - 1
    assert 1 << sh == pow2
    return jnp.right_shift(x, sh)


def _cparams(n_axes):
    return pltpu.CompilerParams(dimension_semantics=("arbitrary",) * n_axes,
                                vmem_limit_bytes=VMEM_LIMIT_BYTES)


def _const_spec(shape, n_axes):
    zeros = (0,) * len(shape)
    index_map = (lambda a: zeros) if n_axes == 1 else (lambda a, b: zeros)
    return pl.BlockSpec(shape, index_map, pipeline_mode=pl.Buffered(1))


def _seq_rows(ref, row0, rc, rows_per_seq, seq_base):
    s = seq_base + lax.div(row0, jnp.int32(rows_per_seq))
    if rows_per_seq >= rc:
        return ref[pl.ds(s, 1), :]
    assert rc % rows_per_seq == 0 and rows_per_seq % SUBLANES == 0
    width = ref.shape[1]
    return jnp.concatenate([jnp.broadcast_to(ref[pl.ds(s + j, 1), :], (rows_per_seq, width))
                            for j in range(rc // rows_per_seq)], axis=0)


def _merge_gate(lo_ref, hi_ref):
    return jax.nn.sigmoid(jnp.concatenate([lo_ref[...], hi_ref[...]], axis=1))


def _mod_kernel(c_ref, w_ref, b_ref, o_ref):
    s = jax.nn.silu(c_ref[...]).astype(BF16)
    o_ref[...] = _dot(s, w_ref[...].astype(BF16)) + b_ref[...]


def _mod_call(c_all, w_ada, b_ada):
    ns = c_all.shape[0]
    tn = 1024
    return pl.pallas_call(
        _mod_kernel,
        grid=(3 * D_MODEL // tn,),
        in_specs=[_const_spec((ns, D_MODEL), 1),
                  pl.BlockSpec((D_MODEL, tn), lambda n: (0, n)),
                  pl.BlockSpec((1, tn), lambda n: (0, n))],
        out_specs=pl.BlockSpec((ns, tn), lambda n: (0, n)),
        out_shape=jax.ShapeDtypeStruct((ns, 3 * D_MODEL), F32),
        compiler_params=_cparams(1),
        name="adaln_mod",
    )(c_all, w_ada, b_ada)


def _s5prep_kernel(lr_ref, li_ref, ldt_ref, bre_ref, bim_ref, cre_ref, cim_ref,
                   are_ref, aim_ref, wbre_ref, wbim_ref, wc_ref):
    dt = jnp.exp(ldt_ref[...])
    lr, li = lr_ref[...], li_ref[...]
    mag = jnp.exp(lr * dt)
    ab_re, ab_im = mag * jnp.cos(li * dt), mag * jnp.sin(li * dt)
    nr, ni = ab_re - 1.0, ab_im
    den = lr * lr + li * li
    cf_re = (nr * lr + ni * li) / den
    cf_im = (ni * lr - nr * li) / den
    are_ref[...] = ab_re
    aim_ref[...] = ab_im
    r = lax.broadcasted_iota(jnp.int32, (S5_TILE_IN, S5_TILE_ST), 0)
    c = lax.broadcasted_iota(jnp.int32, (S5_TILE_IN, S5_TILE_ST), 1)
    on_b = _shr(r, GROUP_A) == _shr(c, P_STATE)
    bre, bim = bre_ref[...], bim_ref[...]
    rep = lambda a: jnp.concatenate([a] * S5_TILE_GROUPS, axis=0)
    wbre_ref[...] = jnp.where(on_b, rep(cf_re * bre - cf_im * bim), 0.0).astype(BF16)
    wbim_ref[...] = jnp.where(on_b, rep(cf_re * bim + cf_im * bre), 0.0).astype(BF16)
    h = lax.broadcasted_iota(jnp.int32, (GROUP_A, S5_TILE_IN), 0)
    c = lax.broadcasted_iota(jnp.int32, (GROUP_A, S5_TILE_IN), 1)
    spread = jnp.where((c & (GROUP_A - 1)) == h, 1.0, 0.0).astype(BF16)
    r = lax.broadcasted_iota(jnp.int32, (S5_TILE_ST, S5_TILE_IN), 0)
    c = lax.broadcasted_iota(jnp.int32, (S5_TILE_ST, S5_TILE_IN), 1)
    on_c = _shr(r, P_STATE) == _shr(c, GROUP_A)
    wc_ref[0:S5_TILE_ST, :] = jnp.where(on_c, _dot(cre_ref[...].astype(BF16), spread), 0.0).astype(BF16)
    wc_ref[S5_TILE_ST:2 * S5_TILE_ST, :] = jnp.where(on_c, -_dot(cim_ref[...].astype(BF16), spread), 0.0).astype(BF16)


def _s5prep_call(lam_re, lam_im, log_dt, b_re, b_im, c_re, c_im):
    row = lambda a: a.reshape(1, N_STATE)
    ldt = jnp.broadcast_to(log_dt[:, None], (N_GROUPS_A, P_STATE))
    bt = lambda b: b.transpose(2, 0, 1).reshape(GROUP_A, N_STATE)
    ct = lambda c: c.transpose(0, 2, 1).reshape(N_STATE, GROUP_A)
    rspec = pl.BlockSpec((1, S5_TILE_ST), lambda i: (0, i))
    bspec = pl.BlockSpec((S5_TILE_IN, S5_TILE_ST), lambda i: (0, i))
    bin_spec = pl.BlockSpec((GROUP_A, S5_TILE_ST), lambda i: (0, i))
    cin_spec = pl.BlockSpec((S5_TILE_ST, GROUP_A), lambda i: (i, 0))
    return pl.pallas_call(
        _s5prep_kernel,
        grid=(S5_TILES,),
        in_specs=[rspec, rspec, rspec, bin_spec, bin_spec, cin_spec, cin_spec],
        out_specs=[rspec, rspec, bspec, bspec,
                   pl.BlockSpec((None, 2 * S5_TILE_ST, S5_TILE_IN), lambda i: (i, 0, 0))],
        out_shape=[jax.ShapeDtypeStruct((1, N_STATE), F32), jax.ShapeDtypeStruct((1, N_STATE), F32),
                   jax.ShapeDtypeStruct((S5_TILE_IN, N_STATE), BF16),
                   jax.ShapeDtypeStruct((S5_TILE_IN, N_STATE), BF16),
                   jax.ShapeDtypeStruct((S5_TILES, 2 * S5_TILE_ST, S5_TILE_IN), BF16)],
        compiler_params=_cparams(1),
        name="s5_prep",
    )(row(lam_re), row(lam_im), row(ldt), bt(b_re), bt(b_im), ct(c_re), ct(c_im))


def _inproj_kernel(x_hbm, shift_ref, scale_ref, gain_ref, w_ref, wt_ref, p_ref, pt_ref, h_scr, x_buf, x_sem,
                   *, tm, rc, rows_per_seq, seq_base, n_mtiles):
    mt, n = pl.program_id(0), pl.program_id(1)

    def x_copy(tile):
        return pltpu.make_async_copy(x_hbm.at[pl.ds(pl.multiple_of(tile * tm, tm), tm), :], x_buf, x_sem)

    @pl.when(n == 0)
    def _():
        pl.when(mt == 0)(lambda: x_copy(0).start())
        x_copy(mt).wait()
        gain = gain_ref[...]

        def body(i, carry):
            r0 = pl.multiple_of(i * rc, rc)
            row0 = mt * tm + r0
            xx = x_buf[pl.ds(r0, rc), :]
            y = xx * lax.rsqrt(jnp.mean(xx * xx, axis=-1, keepdims=True) + EPS)
            sc = _seq_rows(scale_ref, row0, rc, rows_per_seq, seq_base)
            sh = _seq_rows(shift_ref, row0, rc, rows_per_seq, seq_base)
            h_scr[pl.ds(r0, rc), :] = (y * (gain * (1.0 + sc)) + sh).astype(BF16)
            return carry

        lax.fori_loop(0, tm // rc, body, 0)
        pl.when(mt + 1 < n_mtiles)(lambda: x_copy(mt + 1).start())
        pt_ref[...] = _dot_nt(h_scr[...], wt_ref[...].astype(BF16))

    p_ref[...] = _dot_nt(h_scr[...], w_ref[...].astype(BF16))


def _inproj_call(x2d, mod, gain, w_in_t, *, rows_per_seq, seq_base, tm, tn):
    tokens = x2d.shape[0]
    ns = mod.shape[0]
    rc = 128 if rows_per_seq >= 128 else 8 * rows_per_seq
    kern = functools.partial(_inproj_kernel, tm=tm, rc=rc, rows_per_seq=rows_per_seq, seq_base=seq_base,
                             n_mtiles=tokens // tm)
    assert COL_G % tn == 0
    n_before = COL_G // tn

    def w_rows(m, n):
        tiles = jnp.where(n < n_before, n * (tn // SUBLANES), COL_MA // SUBLANES + (n - n_before) * (tn // SUBLANES))
        return (tiles * SUBLANES, 0)

    return pl.pallas_call(
        kern,
        grid=(tokens // tm, PROJ_COLS // tn),
        in_specs=[pl.BlockSpec(memory_space=pl.ANY),
                  pl.BlockSpec((ns, D_MODEL), lambda m, n: (0, 0), pipeline_mode=pl.Buffered(1)),
                  pl.BlockSpec((ns, D_MODEL), lambda m, n: (0, 1), pipeline_mode=pl.Buffered(1)),
                  _const_spec((1, D_MODEL), 2),
                  pl.BlockSpec((pl.Element(tn), pl.Element(D_MODEL)), w_rows),
                  pl.BlockSpec((GATE_RANK, D_MODEL), lambda m, n: (COL_G // GATE_RANK, 0),
                               pipeline_mode=pl.Buffered(1))],
        out_specs=[pl.BlockSpec((tm, tn), lambda m, n: (m, n)),
                   pl.BlockSpec((tm, GATE_RANK), lambda m, n: (m, 0))],
        out_shape=[jax.ShapeDtypeStruct((tokens, PROJ_COLS), F32),
                   jax.ShapeDtypeStruct((tokens, GATE_RANK), F32)],
        scratch_shapes=[pltpu.VMEM((tm, D_MODEL), BF16), pltpu.VMEM((tm, D_MODEL), F32),
                        pltpu.SemaphoreType.DMA(())],
        compiler_params=_cparams(2),
        name="in_proj",
    )(x2d, mod, mod, gain, w_in_t, w_in_t)


def _s5_kernel(*refs, tc, n_sub, ls, has_h0):
    n_in = 12 if has_h0 else 10
    if has_h0:
        (u_ref, za_ref, h0re_ref, h0im_ref, wbre_ref, wbim_ref, wc_ref,
         are_ref, aim_ref, dskip_ref, wglu_ref, bglu_ref) = refs[:n_in]
    else:
        (u_ref, za_ref, wbre_ref, wbim_ref, wc_ref,
         are_ref, aim_ref, dskip_ref, wglu_ref, bglu_ref) = refs[:n_in]
    gated_ref, fre_ref, fim_ref = refs[n_in:n_in + 3]
    scr = refs[n_in + 3:n_in + 3 + 2 * n_sub * S5_TILES]
    xr = [scr[2 * c * S5_TILES:(2 * c + 1) * S5_TILES] for c in range(n_sub)]
    xi = [scr[(2 * c + 1) * S5_TILES:(2 * c + 2) * S5_TILES] for c in range(n_sub)]
    hc_re, hc_im = refs[n_in + 3 + 2 * n_sub * S5_TILES:]
    ct = pl.program_id(1)
    n_seq = tc // ls

    us = []
    for c in range(n_sub):
        u = u_ref[c * tc:(c + 1) * tc, :]
        us.append(u)
        ub = u.astype(BF16)
        for i in range(S5_TILES):
            ui = ub[:, i * S5_TILE_IN:(i + 1) * S5_TILE_IN]
            for w_ref, x_scr in ((wbre_ref, xr[c][i]), (wbim_ref, xi[c][i])):
                res = _dot(ui, w_ref[:, i * S5_TILE_ST:(i + 1) * S5_TILE_ST])
                for j in range(SUBLANES):
                    x_scr[pl.ds(j, tc, stride=SUBLANES), :] = res[:, j * LANES:(j + 1) * LANES]

    a_re = [are_ref[i] for i in range(S5_TILES)]
    a_im = [aim_ref[i] for i in range(S5_TILES)]
    if not has_h0:
        h_re = [jnp.where(ct == 0, 0.0, hc_re[i]) for i in range(S5_TILES)]
        h_im = [jnp.where(ct == 0, 0.0, hc_im[i]) for i in range(S5_TILES)]
    for c in range(n_sub):
        for s in range(n_seq):
            if has_h0:
                h_re = [h0re_ref[c * n_seq + s, i] for i in range(S5_TILES)]
                h_im = [h0im_ref[c * n_seq + s, i] for i in range(S5_TILES)]
            for tl in range(ls):
                r0 = (s * ls + tl) * SUBLANES
                for i in range(S5_TILES):
                    nr = a_re[i] * h_re[i] - a_im[i] * h_im[i] + xr[c][i][r0:r0 + SUBLANES, :]
                    ni = a_re[i] * h_im[i] + a_im[i] * h_re[i] + xi[c][i][r0:r0 + SUBLANES, :]
                    xr[c][i][r0:r0 + SUBLANES, :] = nr
                    xi[c][i][r0:r0 + SUBLANES, :] = ni
                    h_re[i], h_im[i] = nr, ni
            if has_h0:
                for i in range(S5_TILES):
                    fre_ref[c * n_seq + s, i] = h_re[i]
                    fim_ref[c * n_seq + s, i] = h_im[i]
    if not has_h0:
        for i in range(S5_TILES):
            hc_re[i] = h_re[i]
            hc_im[i] = h_im[i]
            fre_ref[0, i] = h_re[i]
            fim_ref[0, i] = h_im[i]

    for c in range(n_sub):
        ys = []
        for i in range(S5_TILES):
            parts = [x_scr[pl.ds(j, tc, stride=SUBLANES), :]
                     for x_scr in (xr[c][i], xi[c][i]) for j in range(SUBLANES)]
            ys.append(_dot(jnp.concatenate(parts, axis=1).astype(BF16), wc_ref[i]))
        y = jnp.concatenate(ys, axis=1) + dskip_ref[...] * us[c]
        g = jax.nn.gelu(y)
        a_y = g * jax.nn.sigmoid(_dot(g.astype(BF16), wglu_ref[...]) + bglu_ref[...])
        gated_ref[c * tc:(c + 1) * tc, :] = (a_y * jax.nn.silu(za_ref[c * tc:(c + 1) * tc, :])).astype(BF16)


def _s5_call(proj, prep, d_skip, w_glu, b_glu, h0, *, n_batch, seq_len, tc, n_sub):
    a_re, a_im, wb_re, wb_im, wc = prep
    tokens = proj.shape[0]
    has_h0 = h0 is not None
    tt = tc * n_sub
    if has_h0:
        ls = seq_len
        grid = (tokens // tt, 1)
        tok = lambda b, c: b
        seq_block = tt // ls
    else:
        ls = tc
        nct = seq_len // tt
        grid = (n_batch, nct)
        tok = lambda b, c: b * nct + c
        seq_block = 1
    const = lambda shape: _const_spec(shape, 2)
    st_spec = pl.BlockSpec((seq_block, S5_TILES, SUBLANES, LANES), lambda b, c: (b, 0, 0, 0))
    in_specs = [pl.BlockSpec((tt, W_A), lambda b, c: (tok(b, c), COL_UA // W_A)),
                pl.BlockSpec((tt, W_A), lambda b, c: (tok(b, c), COL_ZA // W_A))]
    args = [proj] * 2
    if has_h0:
        in_specs += [st_spec, st_spec]
        args += [h0[0], h0[1]]
    in_specs += [const((S5_TILE_IN, N_STATE)), const((S5_TILE_IN, N_STATE)),
                 const((S5_TILES, 2 * S5_TILE_ST, S5_TILE_IN)),
                 const((S5_TILES, SUBLANES, LANES)), const((S5_TILES, SUBLANES, LANES)),
                 const((1, W_A)), const((W_A, W_A)), const((1, W_A))]
    args += [wb_re, wb_im, wc, a_re, a_im, d_skip, w_glu, b_glu]
    n_seq_total = tokens // seq_len
    st_shape = jax.ShapeDtypeStruct((n_seq_total, S5_TILES, SUBLANES, LANES), F32)
    return pl.pallas_call(
        functools.partial(_s5_kernel, tc=tc, n_sub=n_sub, ls=ls, has_h0=has_h0),
        grid=grid,
        in_specs=in_specs,
        out_specs=[pl.BlockSpec((tt, W_A), lambda b, c: (tok(b, c), 0)), st_spec, st_spec],
        out_shape=[jax.ShapeDtypeStruct((tokens, W_A), BF16), st_shape, st_shape],
        scratch_shapes=([pltpu.VMEM((tc * SUBLANES, LANES), F32)] * (2 * n_sub * S5_TILES)
                        + [pltpu.VMEM((S5_TILES, SUBLANES, LANES), F32)] * 2),
        compiler_params=_cparams(2),
        name="s5_branch",
    )(*args)


def _gla_levels(t, ls):
    ms = []
    m = ls
    while m >= 1:
        ms.append(m)
        m //= 2
    idx = np.arange(t)
    mats = []
    for m in ms:
        if m == 1:
            continue
        same = (idx[:, None] // m) == (idx[None, :] // m)
        mats.append(same & (idx[None, :] <= idx[:, None]))
        mats.append(same)
    return ms, np.concatenate(mats, axis=0).astype(np.float32)


def _gla_kernel(*refs, t, n_sub, ls, has_s0, ms):
    if has_s0:
        (q_ref, k_ref, v_ref, zb_ref, ma0_ref, ma1_ref, mb0_ref, mb1_ref, glr_ref, ya_ref,
         s0_ref, lv_ref, wg_ref, bg_ref, gg_ref, waout_ref, wbout_ref, merged_ref, sfin_ref, s_scr) = refs
        assert n_sub == 1
    else:
        (q_ref, k_ref, v_ref, zb_ref, ma0_ref, ma1_ref, mb0_ref, mb1_ref, glr_ref, ya_ref,
         lv_ref, wg_ref, bg_ref, gg_ref, waout_ref, wbout_ref, merged_ref, sfin_ref, s_scr) = refs
        assert ls == t
    ct = pl.program_id(1)
    n_seq = t // ls

    ti = lax.broadcasted_iota(jnp.int32, (t, t), 0)
    si = lax.broadcasted_iota(jnp.int32, (t, t), 1)
    masks = {}
    for li in range(1, len(ms)):
        m = ms[li]
        masks[li] = ((_shr(ti, 2 * m) == _shr(si, 2 * m)) & ((_shr(ti, m) & 1) == 1) & ((_shr(si, m) & 1) == 0))
    rid = lax.broadcasted_iota(jnp.int32, (t, 1), 0)

    if not has_s0:
        states = [jnp.where(ct == 0, 0.0, s_scr[h]) for h in range(N_HEADS_B)]
    ga = jax.nn.log_sigmoid(_dot(glr_ref[...].astype(BF16), wg_ref[...]) + bg_ref[...]) * (LOG2_E / GATE_TAU)
    ga_b = ga.astype(BF16)
    ga_r = ga_b.astype(F32)
    e_all = _dot(lv_ref[...], jnp.concatenate([ga_b[c * t:(c + 1) * t] for c in range(n_sub)], axis=1))
    gated_rows = []
    for c in range(n_sub):
        rows = slice(c * t, (c + 1) * t)
        cols = slice(c * QK_W, (c + 1) * QK_W)
        cum = lambda li: e_all[2 * li * t:(2 * li + 1) * t, cols]
        tot = lambda li: e_all[(2 * li + 1) * t:(2 * li + 2) * t, cols]

        q, k, v = q_ref[rows, :], k_ref[rows, :], v_ref[rows, :]
        heads = []
        for h in range(N_HEADS_B):
            hs = slice(h * DK_B, (h + 1) * DK_B)
            qs = q[:, hs] * (DK_B ** -0.5)
            kh = k[:, hs]
            vh = v[:, h * DV_B:(h + 1) * DV_B]
            vb = vh.astype(BF16)
            b = cum(0)[:, hs]
            bend = tot(0)[:, hs]
            q_dec = (qs * jnp.exp2(b)).astype(BF16)
            k_dec = kh * jnp.exp2(bend - b)
            scores = jnp.zeros((t, t), F32)
            for li in range(1, len(ms)):
                if ms[li] == 1:
                    qm = (qs * jnp.exp2(ga_r[rows, hs])).astype(BF16)
                    km = kh.astype(BF16)
                else:
                    eq = cum(li)[:, hs]
                    qm = (qs * jnp.exp2(eq)).astype(BF16)
                    km = (kh * jnp.exp2(tot(li)[:, hs] - eq)).astype(BF16)
                scores = jnp.where(masks[li], _dot_nt(qm, km), scores)
            o = _dot(scores.astype(BF16), vb) + jnp.sum(qs * kh, axis=-1, keepdims=True) * vh
            o_inter = []
            for i in range(n_seq):
                r0 = i * ls
                s_old = s0_ref[i, h] if has_s0 else states[h]
                o_inter.append(_dot(q_dec[r0:r0 + ls], s_old.astype(BF16)))
                kd = k_dec if n_seq == 1 else jnp.where((rid >= r0) & (rid < r0 + ls), k_dec, 0.0)
                d_end = jnp.exp2(bend[r0:r0 + 1, :])
                d_col = jnp.broadcast_to(d_end, (DK_B, DK_B)).T
                s_new = jnp.concatenate([d_col] * (DV_B // DK_B), axis=1) * s_old + _dot_tn(kd.astype(BF16), vb)
                if has_s0:
                    sfin_ref[i, h] = s_new
                else:
                    states[h] = s_new
            o = o + (o_inter[0] if n_seq == 1 else jnp.concatenate(o_inter, axis=0))
            heads.append(o * lax.rsqrt(jnp.mean(o * o, axis=-1, keepdims=True) + EPS))
        b_y = jnp.concatenate(heads, axis=1) * gg_ref[...]
        gated_rows.append((b_y * jax.nn.silu(zb_ref[rows, :])).astype(BF16))
    if not has_s0:
        for h in range(N_HEADS_B):
            s_scr[h] = states[h]
            sfin_ref[0, h] = states[h]
    gated = gated_rows[0] if n_sub == 1 else jnp.concatenate(gated_rows, axis=0)
    merged = (_merge_gate(ma0_ref, ma1_ref) * _dot(ya_ref[...], waout_ref[...])
              + _merge_gate(mb0_ref, mb1_ref) * _dot(gated, wbout_ref[...]))
    merged_ref[...] = merged.astype(BF16)


def _gla_call(proj, proj_g, y_a, w_gate, b_gate, g_gain, w_a_out, w_b_out, s0, *, n_batch, seq_len, t, n_sub):
    tokens = proj.shape[0]
    has_s0 = s0 is not None
    tt = t * n_sub
    if has_s0:
        ls = seq_len
        grid = (tokens // tt, 1)
        tok = lambda b, c: b
        seq_block = tt // ls
    else:
        ls = t
        nct = seq_len // tt
        grid = (n_batch, nct)
        tok = lambda b, c: b * nct + c
        seq_block = 1
    ms, lv = _gla_levels(t, ls)
    lv = jnp.asarray(lv, dtype=BF16)
    const = lambda shape: _const_spec(shape, 2)
    st_spec = pl.BlockSpec((seq_block, N_HEADS_B, DK_B, DV_B), lambda b, c: (b, 0, 0, 0))
    in_specs = [pl.BlockSpec((tt, QK_W), lambda b, c: (tok(b, c), COL_Q // QK_W)),
                pl.BlockSpec((tt, QK_W), lambda b, c: (tok(b, c), COL_K // QK_W)),
                pl.BlockSpec((tt, W_B), lambda b, c: (tok(b, c), COL_V // W_B)),
                pl.BlockSpec((tt, W_B), lambda b, c: (tok(b, c), COL_ZB // W_B)),
                pl.BlockSpec((tt, W_A), lambda b, c: (tok(b, c), PCOL_MA // W_A)),
                pl.BlockSpec((tt, W_A), lambda b, c: (tok(b, c), PCOL_MA // W_A + 1)),
                pl.BlockSpec((tt, W_B), lambda b, c: (tok(b, c), PCOL_MB // W_B)),
                pl.BlockSpec((tt, W_B), lambda b, c: (tok(b, c), PCOL_MB // W_B + 1)),
                pl.BlockSpec((tt, GATE_RANK), lambda b, c: (tok(b, c), 0)),
                pl.BlockSpec((tt, W_A), lambda b, c: (tok(b, c), 0))]
    args = [proj, proj, proj, proj, proj, proj, proj, proj, proj_g, y_a]
    if has_s0:
        in_specs.append(st_spec)
        args.append(s0)
    in_specs += [const(lv.shape), const((GATE_RANK, QK_W)), const((1, QK_W)), const((1, W_B)),
                 const((W_A, D_MODEL)), const((W_B, D_MODEL))]
    args += [lv, w_gate, b_gate, g_gain, w_a_out, w_b_out]
    n_seq_total = tokens // seq_len
    return pl.pallas_call(
        functools.partial(_gla_kernel, t=t, n_sub=n_sub, ls=ls, has_s0=has_s0, ms=tuple(ms)),
        grid=grid,
        in_specs=in_specs,
        out_specs=[pl.BlockSpec((tt, D_MODEL), lambda b, c: (tok(b, c), 0)), st_spec],
        out_shape=[jax.ShapeDtypeStruct((tokens, D_MODEL), BF16),
                   jax.ShapeDtypeStruct((n_seq_total, N_HEADS_B, DK_B, DV_B), F32)],
        scratch_shapes=[pltpu.VMEM((N_HEADS_B, DK_B, DV_B), F32)],
        compiler_params=_cparams(2),
        name="gla_branch",
    )(*args)


def _out_kernel(m_ref, x_ref, gate_ref, w_ref, fg_ref, y_ref, acc0, acc1,
                *, tm, rc, rows_per_seq, seq_base, n_tiles):
    s = pl.program_id(0)

    def matmul(acc):
        acc[...] = _dot(m_ref[...], w_ref[...])

    def epilogue(acc):
        fg = fg_ref[...]
        for i in range(tm // rc):
            r0 = i * rc
            gate = _seq_rows(gate_ref, (s - 1) * tm + r0, rc, rows_per_seq, seq_base)
            yy = x_ref[r0:r0 + rc, :] + gate * acc[r0:r0 + rc, :]
            y_ref[r0:r0 + rc, :] = yy * lax.rsqrt(jnp.mean(yy * yy, axis=-1, keepdims=True) + EPS) * fg

    accs = (acc0, acc1)

    pl.when(s == 0)(lambda: matmul(acc0))
    for par in (0, 1):
        @pl.when((s > 0) & (s < n_tiles) & (lax.rem(s, 2) == par))
        def _(par=par):
            matmul(accs[par])
            epilogue(accs[1 - par])

    pl.when(s == n_tiles)(lambda: epilogue(accs[(n_tiles - 1) % 2]))


def _out_call(merged, x2d, mod, w_out, fgain, *, rows_per_seq, seq_base, tm):
    tokens = x2d.shape[0]
    ns = mod.shape[0]
    n_tiles = tokens // tm
    rc = 128 if rows_per_seq >= 128 else 8 * rows_per_seq
    kern = functools.partial(_out_kernel, tm=tm, rc=rc, rows_per_seq=rows_per_seq, seq_base=seq_base,
                             n_tiles=n_tiles)
    prev = lambda m: (jnp.maximum(m - 1, 0), 0)
    return pl.pallas_call(
        kern,
        grid=(n_tiles + 1,),
        in_specs=[pl.BlockSpec((tm, D_MODEL), lambda m: (jnp.minimum(m, n_tiles - 1), 0)),
                  pl.BlockSpec((tm, D_MODEL), prev),
                  pl.BlockSpec((ns, D_MODEL), lambda m: (0, 2), pipeline_mode=pl.Buffered(1)),
                  _const_spec((D_MODEL, D_MODEL), 1),
                  _const_spec((1, D_MODEL), 1)],
        out_specs=pl.BlockSpec((tm, D_MODEL), prev),
        out_shape=jax.ShapeDtypeStruct((tokens, D_MODEL), F32),
        scratch_shapes=[pltpu.VMEM((tm, D_MODEL), F32), pltpu.VMEM((tm, D_MODEL), F32)],
        compiler_params=_cparams(1),
        name="out_proj",
    )(merged, x2d, mod, w_out, fgain)


def _tile(n, target):
    t = min(n, target)
    assert n % t == 0
    return t


def _layer(x, n_seq_before, mod, weights, prep, s0_ssm, s0_gla):
    (gain, w_in_t, d_skip, w_glu, b_glu, w_a_out, w_gate, b_gate, g_gain, w_b_out, w_out, fgain) = weights
    n_batch, seq_len, _ = x.shape
    tokens = n_batch * seq_len
    x2d = x.reshape(tokens, D_MODEL)
    long_seq = s0_ssm is None
    span = seq_len if long_seq else tokens
    proj, proj_g = _inproj_call(x2d, mod, gain, w_in_t, rows_per_seq=seq_len, seq_base=n_seq_before,
                                   tm=_tile(span, 2048), tn=512 if long_seq else 1024)
    tc = _tile(span, 256)
    y_a, f_re, f_im = _s5_call(proj, prep, d_skip, w_glu, b_glu, s0_ssm, n_batch=n_batch, seq_len=seq_len,
                               tc=tc, n_sub=2 if span % (2 * tc) == 0 else 1)
    if long_seq:
        t = _tile(seq_len, 128)
        n_sub = 2 if seq_len % (2 * t) == 0 else 1
    else:
        t, n_sub = _tile(tokens, 128), 1
    merged, s_fin = _gla_call(proj, proj_g, y_a, w_gate, b_gate, g_gain, w_a_out, w_b_out, s0_gla,
                              n_batch=n_batch, seq_len=seq_len, t=t, n_sub=n_sub)
    y = _out_call(merged, x2d, mod, w_out, fgain, rows_per_seq=seq_len, seq_base=n_seq_before,
                  tm=_tile(span, 512))
    st = lambda f: f.reshape(1, n_batch, N_GROUPS_A, P_STATE)
    return (y.reshape(n_batch, seq_len, D_MODEL), st(f_re), st(f_im),
            s_fin.reshape(1, n_batch, N_HEADS_B, DK_B, DV_B))


def kernel(x_prompt, x_sample, c_prompt, c_sample, state_ssm_re, state_ssm_im, state_gla, w_ada, b_ada, norm_gain, w_in, lambda_re, lambda_im, log_dt, ssm_b_re, ssm_b_im, ssm_c_re, ssm_c_im, d_skip, w_glu, b_glu, w_gate_up, b_gate, gla_norm_gain, w_a_out, w_b_out, w_out, final_norm_gain):
    assert w_ada.shape[0] == 1, "single-layer step"
    assert w_in.shape[2] == IN_COLS
    n_prompt, n_sample = x_prompt.shape[0], x_sample.shape[0]

    mod = _mod_call(jnp.concatenate([c_prompt, c_sample], axis=0), w_ada[0], b_ada)
    a_re, a_im, wb_re, wb_im, wc = _s5prep_call(lambda_re[0], lambda_im[0], log_dt[0], ssm_b_re[0], ssm_b_im[0],
                                                ssm_c_re[0], ssm_c_im[0])
    state_tiles = lambda a: a.reshape(-1, S5_TILES, SUBLANES, LANES)
    prep = (state_tiles(a_re)[0], state_tiles(a_im)[0], wb_re, wb_im, wc)

    w_in_t = jnp.swapaxes(w_in, 1, 2).reshape(IN_COLS, D_MODEL)
    w_gate = w_gate_up[0].astype(BF16)
    weights = (norm_gain, w_in_t, d_skip, w_glu[0].astype(BF16), b_glu, w_a_out[0].astype(BF16),
               w_gate, b_gate, gla_norm_gain, w_b_out[0].astype(BF16), w_out[0].astype(BF16),
               final_norm_gain.reshape(1, D_MODEL))

    y_p, pre, pim, pgla = _layer(x_prompt, 0, mod, weights, prep, None, None)
    y_s, sre, sim, sgla = _layer(x_sample, n_prompt, mod, weights, prep,
                                 (state_tiles(state_ssm_re[0]), state_tiles(state_ssm_im[0])), state_gla[0])
    return (y_p, y_s, pre, pim, pgla, sre, sim, sgla)
```

```python
import functools

import numpy as np
import jax
import jax.numpy as jnp
from jax import lax
from jax.experimental import pallas as pl
from jax.experimental.pallas import tpu as pltpu

F32 = jnp.float32
BF16 = jnp.bfloat16

D_MODEL = 2048
W_A = D_MODEL // 2
GROUP_A = 16
N_GROUPS_A = W_A // GROUP_A
P_STATE = 64
N_STATE = N_GROUPS_A * P_STATE
W_B = D_MODEL // 2
N_HEADS_B = 4
DK_B = W_B // 2 // N_HEADS_B
DV_B = W_B // N_HEADS_B
QK_W = N_HEADS_B * DK_B
GATE_RANK = 16
GATE_TAU = 16.0
EPS = 1e-6
LOG2_E = 1.4426950408889634

LANES = 128
SUBLANES = 8
V7X_VMEM_BYTES = 64 * 1024 * 1024
VMEM_LIMIT_BYTES = 56 * 1024 * 1024

S5_TILE_GROUPS = 16
S5_TILES = N_GROUPS_A // S5_TILE_GROUPS
S5_TILE_IN = S5_TILE_GROUPS * GROUP_A
S5_TILE_ST = S5_TILE_GROUPS * P_STATE
assert S5_TILE_ST == SUBLANES * LANES

COL_UA = 0
COL_ZA = COL_UA + W_A
COL_Q = COL_ZA + W_A
COL_K = COL_Q + QK_W
COL_V = COL_K + QK_W
COL_ZB = COL_V + W_B
COL_G = COL_ZB + W_B
COL_MA = COL_G + GATE_RANK
COL_MB = COL_MA + D_MODEL
IN_COLS = COL_MB + D_MODEL
PROJ_COLS = IN_COLS - GATE_RANK
PCOL_MA = COL_G
PCOL_MB = PCOL_MA + D_MODEL
assert COL_MA % SUBLANES == 0 and COL_G % GATE_RANK == 0


def _dot(a, b):
    return jnp.dot(a, b, preferred_element_type=F32)


def _dot_nt(a, b):
    return lax.dot_general(a, b, (((1,), (1,)), ((), ())), preferred_element_type=F32)


def _dot_tn(a, b):
    return lax.dot_general(a, b, (((0,), (0,)), ((), ())), preferred_element_type=F32)


def _shr(x, pow2):
    sh = int(pow2).bit_length() - 1
    assert 1 << sh == pow2
    return jnp.right_shift(x, sh)


def _cparams(n_axes):
    return pltpu.CompilerParams(dimension_semantics=("arbitrary",) * n_axes,
                                vmem_limit_bytes=VMEM_LIMIT_BYTES)


def _const_spec(shape, n_axes):
    zeros = (0,) * len(shape)
    index_map = (lambda a: zeros) if n_axes == 1 else (lambda a, b: zeros)
    return pl.BlockSpec(shape, index_map, pipeline_mode=pl.Buffered(1))


def _seq_rows(ref, row0, rc, rows_per_seq, seq_base):
    s = seq_base + lax.div(row0, jnp.int32(rows_per_seq))
    if rows_per_seq >= rc:
        return ref[pl.ds(s, 1), :]
    assert rc % rows_per_seq == 0 and rows_per_seq % SUBLANES == 0
    width = ref.shape[1]
    return jnp.concatenate([jnp.broadcast_to(ref[pl.ds(s + j, 1), :], (rows_per_seq, width))
                            for j in range(rc // rows_per_seq)], axis=0)


def _merge_gate(lo_ref, hi_ref):
    return jax.nn.sigmoid(jnp.concatenate([lo_ref[...], hi_ref[...]], axis=1))


def _mod_kernel(c_ref, w_ref, b_ref, o_ref):
    s = jax.nn.silu(c_ref[...]).astype(BF16)
    o_ref[...] = _dot(s, w_ref[...].astype(BF16)) + b_ref[...]


def _mod_call(c_all, w_ada, b_ada):
    ns = c_all.shape[0]
    tn = 1024
    return pl.pallas_call(
        _mod_kernel,
        grid=(3 * D_MODEL // tn,),
        in_specs=[_const_spec((ns, D_MODEL), 1),
                  pl.BlockSpec((D_MODEL, tn), lambda n: (0, n)),
                  pl.BlockSpec((1, tn), lambda n: (0, n))],
        out_specs=pl.BlockSpec((ns, tn), lambda n: (0, n)),
        out_shape=jax.ShapeDtypeStruct((ns, 3 * D_MODEL), F32),
        compiler_params=_cparams(1),
        name="adaln_mod",
    )(c_all, w_ada, b_ada)


def _s5prep_kernel(lr_ref, li_ref, ldt_ref, bre_ref, bim_ref, cre_ref, cim_ref,
                   are_ref, aim_ref, wbre_ref, wbim_ref, wc_ref):
    dt = jnp.exp(ldt_ref[...])
    lr, li = lr_ref[...], li_ref[...]
    mag = jnp.exp(lr * dt)
    ab_re, ab_im = mag * jnp.cos(li * dt), mag * jnp.sin(li * dt)
    nr, ni = ab_re - 1.0, ab_im
    den = lr * lr + li * li
    cf_re = (nr * lr + ni * li) / den
    cf_im = (ni * lr - nr * li) / den
    are_ref[...] = ab_re
    aim_ref[...] = ab_im
    r = lax.broadcasted_iota(jnp.int32, (S5_TILE_IN, S5_TILE_ST), 0)
    c = lax.broadcasted_iota(jnp.int32, (S5_TILE_IN, S5_TILE_ST), 1)
    on_b = _shr(r, GROUP_A) == _shr(c, P_STATE)
    bre, bim = bre_ref[...], bim_ref[...]
    rep = lambda a: jnp.concatenate([a] * S5_TILE_GROUPS, axis=0)
    wbre_ref[...] = jnp.where(on_b, rep(cf_re * bre - cf_im * bim), 0.0).astype(BF16)
    wbim_ref[...] = jnp.where(on_b, rep(cf_re * bim + cf_im * bre), 0.0).astype(BF16)
    h = lax.broadcasted_iota(jnp.int32, (GROUP_A, S5_TILE_IN), 0)
    c = lax.broadcasted_iota(jnp.int32, (GROUP_A, S5_TILE_IN), 1)
    spread = jnp.where((c & (GROUP_A - 1)) == h, 1.0, 0.0).astype(BF16)
    r = lax.broadcasted_iota(jnp.int32, (S5_TILE_ST, S5_TILE_IN), 0)
    c = lax.broadcasted_iota(jnp.int32, (S5_TILE_ST, S5_TILE_IN), 1)
    on_c = _shr(r, P_STATE) == _shr(c, GROUP_A)
    wc_ref[0:S5_TILE_ST, :] = jnp.where(on_c, _dot(cre_ref[...].astype(BF16), spread), 0.0).astype(BF16)
    wc_ref[S5_TILE_ST:2 * S5_TILE_ST, :] = jnp.where(on_c, -_dot(cim_ref[...].astype(BF16), spread), 0.0).astype(BF16)


def _s5prep_call(lam_re, lam_im, log_dt, b_re, b_im, c_re, c_im):
    row = lambda a: a.reshape(1, N_STATE)
    ldt = jnp.broadcast_to(log_dt[:, None], (N_GROUPS_A, P_STATE))
    bt = lambda b: b.transpose(2, 0, 1).reshape(GROUP_A, N_STATE)
    ct = lambda c: c.transpose(0, 2, 1).reshape(N_STATE, GROUP_A)
    rspec = pl.BlockSpec((1, S5_TILE_ST), lambda i: (0, i))
    bspec = pl.BlockSpec((S5_TILE_IN, S5_TILE_ST), lambda i: (0, i))
    bin_spec = pl.BlockSpec((GROUP_A, S5_TILE_ST), lambda i: (0, i))
    cin_spec = pl.BlockSpec((S5_TILE_ST, GROUP_A), lambda i: (i, 0))
    return pl.pallas_call(
        _s5prep_kernel,
        grid=(S5_TILES,),
        in_specs=[rspec, rspec, rspec, bin_spec, bin_spec, cin_spec, cin_spec],
        out_specs=[rspec, rspec, bspec, bspec,
                   pl.BlockSpec((None, 2 * S5_TILE_ST, S5_TILE_IN), lambda i: (i, 0, 0))],
        out_shape=[jax.ShapeDtypeStruct((1, N_STATE), F32), jax.ShapeDtypeStruct((1, N_STATE), F32),
                   jax.ShapeDtypeStruct((S5_TILE_IN, N_STATE), BF16),
                   jax.ShapeDtypeStruct((S5_TILE_IN, N_STATE), BF16),
                   jax.ShapeDtypeStruct((S5_TILES, 2 * S5_TILE_ST, S5_TILE_IN), BF16)],
        compiler_params=_cparams(1),
        name="s5_prep",
    )(row(lam_re), row(lam_im), row(ldt), bt(b_re), bt(b_im), ct(c_re), ct(c_im))


def _inproj_kernel(x_hbm, shift_ref, scale_ref, gain_ref, w_ref, wt_ref, p_ref, pt_ref, h_scr, x_buf, x_sem,
                   *, tm, rc, rows_per_seq, seq_base, n_mtiles):
    mt, n = pl.program_id(0), pl.program_id(1)

    def x_copy(tile):
        return pltpu.make_async_copy(x_hbm.at[pl.ds(pl.multiple_of(tile * tm, tm), tm), :], x_buf, x_sem)

    @pl.when(n == 0)
    def _():
        pl.when(mt == 0)(lambda: x_copy(0).start())
        x_copy(mt).wait()
        gain = gain_ref[...]

        def body(i, carry):
            r0 = pl.multiple_of(i * rc, rc)
            row0 = mt * tm + r0
            xx = x_buf[pl.ds(r0, rc), :]
            y = xx * lax.rsqrt(jnp.mean(xx * xx, axis=-1, keepdims=True) + EPS)
            sc = _seq_rows(scale_ref, row0, rc, rows_per_seq, seq_base)
            sh = _seq_rows(shift_ref, row0, rc, rows_per_seq, seq_base)
            h_scr[pl.ds(r0, rc), :] = (y * (gain * (1.0 + sc)) + sh).astype(BF16)
            return carry

        lax.fori_loop(0, tm // rc, body, 0)
        pl.when(mt + 1 < n_mtiles)(lambda: x_copy(mt + 1).start())
        pt_ref[...] = _dot_nt(h_scr[...], wt_ref[...].astype(BF16))

    p_ref[...] = _dot_nt(h_scr[...], w_ref[...].astype(BF16))


def _inproj_call(x2d, mod, gain, w_in_t, *, rows_per_seq, seq_base, tm, tn):
    tokens = x2d.shape[0]
    ns = mod.shape[0]
    rc = 128 if rows_per_seq >= 128 else 8 * rows_per_seq
    kern = functools.partial(_inproj_kernel, tm=tm, rc=rc, rows_per_seq=rows_per_seq, seq_base=seq_base,
                             n_mtiles=tokens // tm)
    assert COL_G % tn == 0
    n_before = COL_G // tn

    def w_rows(m, n):
        tiles = jnp.where(n < n_before, n * (tn // SUBLANES), COL_MA // SUBLANES + (n - n_before) * (tn // SUBLANES))
        return (tiles * SUBLANES, 0)

    return pl.pallas_call(
        kern,
        grid=(tokens // tm, PROJ_COLS // tn),
        in_specs=[pl.BlockSpec(memory_space=pl.ANY),
                  pl.BlockSpec((ns, D_MODEL), lambda m, n: (0, 0), pipeline_mode=pl.Buffered(1)),
                  pl.BlockSpec((ns, D_MODEL), lambda m, n: (0, 1), pipeline_mode=pl.Buffered(1)),
                  _const_spec((1, D_MODEL), 2),
                  pl.BlockSpec((pl.Element(tn), pl.Element(D_MODEL)), w_rows),
                  pl.BlockSpec((GATE_RANK, D_MODEL), lambda m, n: (COL_G // GATE_RANK, 0),
                               pipeline_mode=pl.Buffered(1))],
        out_specs=[pl.BlockSpec((tm, tn), lambda m, n: (m, n)),
                   pl.BlockSpec((tm, GATE_RANK), lambda m, n: (m, 0))],
        out_shape=[jax.ShapeDtypeStruct((tokens, PROJ_COLS), F32),
                   jax.ShapeDtypeStruct((tokens, GATE_RANK), F32)],
        scratch_shapes=[pltpu.VMEM((tm, D_MODEL), BF16), pltpu.VMEM((tm, D_MODEL), F32),
                        pltpu.SemaphoreType.DMA(())],
        compiler_params=_cparams(2),
        name="in_proj",
    )(x2d, mod, mod, gain, w_in_t, w_in_t)


def _s5_kernel(*refs, tc, n_sub, ls, has_h0):
    n_in = 12 if has_h0 else 10
    if has_h0:
        (u_ref, za_ref, h0re_ref, h0im_ref, wbre_ref, wbim_ref, wc_ref,
         are_ref, aim_ref, dskip_ref, wglu_ref, bglu_ref) = refs[:n_in]
    else:
        (u_ref, za_ref, wbre_ref, wbim_ref, wc_ref,
         are_ref, aim_ref, dskip_ref, wglu_ref, bglu_ref) = refs[:n_in]
    gated_ref, fre_ref, fim_ref = refs[n_in:n_in + 3]
    scr = refs[n_in + 3:n_in + 3 + 2 * n_sub * S5_TILES]
    xr = [scr[2 * c * S5_TILES:(2 * c + 1) * S5_TILES] for c in range(n_sub)]
    xi = [scr[(2 * c + 1) * S5_TILES:(2 * c + 2) * S5_TILES] for c in range(n_sub)]
    hc_re, hc_im = refs[n_in + 3 + 2 * n_sub * S5_TILES:]
    ct = pl.program_id(1)
    n_seq = tc // ls

    us = []
    for c in range(n_sub):
        u = u_ref[c * tc:(c + 1) * tc, :]
        us.append(u)
        ub = u.astype(BF16)
        for i in range(S5_TILES):
            ui = ub[:, i * S5_TILE_IN:(i + 1) * S5_TILE_IN]
            for w_ref, x_scr in ((wbre_ref, xr[c][i]), (wbim_ref, xi[c][i])):
                res = _dot(ui, w_ref[:, i * S5_TILE_ST:(i + 1) * S5_TILE_ST])
                for j in range(SUBLANES):
                    x_scr[pl.ds(j, tc, stride=SUBLANES), :] = res[:, j * LANES:(j + 1) * LANES]

    a_re = [are_ref[i] for i in range(S5_TILES)]
    a_im = [aim_ref[i] for i in range(S5_TILES)]
    if not has_h0:
        h_re = [jnp.where(ct == 0, 0.0, hc_re[i]) for i in range(S5_TILES)]
        h_im = [jnp.where(ct == 0, 0.0, hc_im[i]) for i in range(S5_TILES)]
    for c in range(n_sub):
        for s in range(n_seq):
            if has_h0:
                h_re = [h0re_ref[c * n_seq + s, i] for i in range(S5_TILES)]
                h_im = [h0im_ref[c * n_seq + s, i] for i in range(S5_TILES)]
            for tl in range(ls):
                r0 = (s * ls + tl) * SUBLANES
                for i in range(S5_TILES):
                    nr = a_re[i] * h_re[i] - a_im[i] * h_im[i] + xr[c][i][r0:r0 + SUBLANES, :]
                    ni = a_re[i] * h_im[i] + a_im[i] * h_re[i] + xi[c][i][r0:r0 + SUBLANES, :]
                    xr[c][i][r0:r0 + SUBLANES, :] = nr
                    xi[c][i][r0:r0 + SUBLANES, :] = ni
                    h_re[i], h_im[i] = nr, ni
            if has_h0:
                for i in range(S5_TILES):
                    fre_ref[c * n_seq + s, i] = h_re[i]
                    fim_ref[c * n_seq + s, i] = h_im[i]
    if not has_h0:
        for i in range(S5_TILES):
            hc_re[i] = h_re[i]
            hc_im[i] = h_im[i]
            fre_ref[0, i] = h_re[i]
            fim_ref[0, i] = h_im[i]

    for c in range(n_sub):
        ys = []
        for i in range(S5_TILES):
            parts = [x_scr[pl.ds(j, tc, stride=SUBLANES), :]
                     for x_scr in (xr[c][i], xi[c][i]) for j in range(SUBLANES)]
            ys.append(_dot(jnp.concatenate(parts, axis=1).astype(BF16), wc_ref[i]))
        y = jnp.concatenate(ys, axis=1) + dskip_ref[...] * us[c]
        g = jax.nn.gelu(y)
        a_y = g * jax.nn.sigmoid(_dot(g.astype(BF16), wglu_ref[...].astype(BF16)) + bglu_ref[...])
        gated_ref[c * tc:(c + 1) * tc, :] = (a_y * jax.nn.silu(za_ref[c * tc:(c + 1) * tc, :])).astype(BF16)


def _s5_call(proj, prep, d_skip, w_glu, b_glu, h0, *, n_batch, seq_len, tc, n_sub):
    a_re, a_im, wb_re, wb_im, wc = prep
    tokens = proj.shape[0]
    has_h0 = h0 is not None
    tt = tc * n_sub
    if has_h0:
        ls = seq_len
        grid = (tokens // tt, 1)
        tok = lambda b, c: b
        seq_block = tt // ls
    else:
        ls = tc
        nct = seq_len // tt
        grid = (n_batch, nct)
        tok = lambda b, c: b * nct + c
        seq_block = 1
    const = lambda shape: _const_spec(shape, 2)
    st_spec = pl.BlockSpec((seq_block, S5_TILES, SUBLANES, LANES), lambda b, c: (b, 0, 0, 0))
    in_specs = [pl.BlockSpec((tt, W_A), lambda b, c: (tok(b, c), COL_UA // W_A)),
                pl.BlockSpec((tt, W_A), lambda b, c: (tok(b, c), COL_ZA // W_A))]
    args = [proj] * 2
    if has_h0:
        in_specs += [st_spec, st_spec]
        args += [h0[0], h0[1]]
    in_specs += [const((S5_TILE_IN, N_STATE)), const((S5_TILE_IN, N_STATE)),
                 const((S5_TILES, 2 * S5_TILE_ST, S5_TILE_IN)),
                 const((S5_TILES, SUBLANES, LANES)), const((S5_TILES, SUBLANES, LANES)),
                 const((1, W_A)), const((W_A, W_A)), const((1, W_A))]
    args += [wb_re, wb_im, wc, a_re, a_im, d_skip, w_glu, b_glu]
    n_seq_total = tokens // seq_len
    st_shape = jax.ShapeDtypeStruct((n_seq_total, S5_TILES, SUBLANES, LANES), F32)
    return pl.pallas_call(
        functools.partial(_s5_kernel, tc=tc, n_sub=n_sub, ls=ls, has_h0=has_h0),
        grid=grid,
        in_specs=in_specs,
        out_specs=[pl.BlockSpec((tt, W_A), lambda b, c: (tok(b, c), 0)), st_spec, st_spec],
        out_shape=[jax.ShapeDtypeStruct((tokens, W_A), BF16), st_shape, st_shape],
        scratch_shapes=([pltpu.VMEM((tc * SUBLANES, LANES), F32)] * (2 * n_sub * S5_TILES)
                        + [pltpu.VMEM((S5_TILES, SUBLANES, LANES), F32)] * 2),
        compiler_params=_cparams(2),
        name="s5_branch",
    )(*args)


def _gla_levels(t, ls):
    ms = []
    m = ls
    while m >= 1:
        ms.append(m)
        m //= 2
    idx = np.arange(t)
    mats = []
    for m in ms:
        if m == 1:
            continue
        same = (idx[:, None] // m) == (idx[None, :] // m)
        mats.append(same & (idx[None, :] <= idx[:, None]))
        mats.append(same)
    return ms, np.concatenate(mats, axis=0).astype(np.float32)


def _gla_kernel(*refs, t, n_sub, ls, has_s0, ms):
    if has_s0:
        (q_ref, k_ref, v_ref, zb_ref, ma0_ref, ma1_ref, mb0_ref, mb1_ref, glr_ref, ya_ref,
         s0_ref, lv_ref, wg_ref, bg_ref, gg_ref, waout_ref, wbout_ref, merged_ref, sfin_ref, s_scr) = refs
        assert n_sub == 1
    else:
        (q_ref, k_ref, v_ref, zb_ref, ma0_ref, ma1_ref, mb0_ref, mb1_ref, glr_ref, ya_ref,
         lv_ref, wg_ref, bg_ref, gg_ref, waout_ref, wbout_ref, merged_ref, sfin_ref, s_scr) = refs
        assert ls == t
    ct = pl.program_id(1)
    n_seq = t // ls

    ti = lax.broadcasted_iota(jnp.int32, (t, t), 0)
    si = lax.broadcasted_iota(jnp.int32, (t, t), 1)
    masks = {}
    for li in range(1, len(ms)):
        m = ms[li]
        masks[li] = ((_shr(ti, 2 * m) == _shr(si, 2 * m)) & ((_shr(ti, m) & 1) == 1) & ((_shr(si, m) & 1) == 0))
    rid = lax.broadcasted_iota(jnp.int32, (t, 1), 0)

    if not has_s0:
        states = [jnp.where(ct == 0, 0.0, s_scr[h]) for h in range(N_HEADS_B)]
    ga = jax.nn.log_sigmoid(_dot(glr_ref[...].astype(BF16), wg_ref[...]) + bg_ref[...]) * (LOG2_E / GATE_TAU)
    ga_b = ga.astype(BF16)
    ga_r = ga_b.astype(F32)
    e_all = _dot(lv_ref[...], jnp.concatenate([ga_b[c * t:(c + 1) * t] for c in range(n_sub)], axis=1))
    gated_rows = []
    for c in range(n_sub):
        rows = slice(c * t, (c + 1) * t)
        cols = slice(c * QK_W, (c + 1) * QK_W)
        cum = lambda li: e_all[2 * li * t:(2 * li + 1) * t, cols]
        tot = lambda li: e_all[(2 * li + 1) * t:(2 * li + 2) * t, cols]

        q, k, v = q_ref[rows, :], k_ref[rows, :], v_ref[rows, :]
        heads = []
        for h in range(N_HEADS_B):
            hs = slice(h * DK_B, (h + 1) * DK_B)
            qs = q[:, hs] * (DK_B ** -0.5)
            kh = k[:, hs]
            vh = v[:, h * DV_B:(h + 1) * DV_B]
            vb = vh.astype(BF16)
            b = cum(0)[:, hs]
            bend = tot(0)[:, hs]
            q_dec = (qs * jnp.exp2(b)).astype(BF16)
            k_dec = kh * jnp.exp2(bend - b)
            scores = jnp.zeros((t, t), F32)
            for li in range(1, len(ms)):
                if ms[li] == 1:
                    qm = (qs * jnp.exp2(ga_r[rows, hs])).astype(BF16)
                    km = kh.astype(BF16)
                else:
                    eq = cum(li)[:, hs]
                    qm = (qs * jnp.exp2(eq)).astype(BF16)
                    km = (kh * jnp.exp2(tot(li)[:, hs] - eq)).astype(BF16)
                scores = jnp.where(masks[li], _dot_nt(qm, km), scores)
            o = _dot(scores.astype(BF16), vb) + jnp.sum(qs * kh, axis=-1, keepdims=True) * vh
            o_inter = []
            for i in range(n_seq):
                r0 = i * ls
                s_old = s0_ref[i, h] if has_s0 else states[h]
                o_inter.append(_dot(q_dec[r0:r0 + ls], s_old.astype(BF16)))
                kd = k_dec if n_seq == 1 else jnp.where((rid >= r0) & (rid < r0 + ls), k_dec, 0.0)
                d_end = jnp.exp2(bend[r0:r0 + 1, :])
                d_col = jnp.broadcast_to(d_end, (DK_B, DK_B)).T
                s_new = jnp.concatenate([d_col] * (DV_B // DK_B), axis=1) * s_old + _dot_tn(kd.astype(BF16), vb)
                if has_s0:
                    sfin_ref[i, h] = s_new
                else:
                    states[h] = s_new
            o = o + (o_inter[0] if n_seq == 1 else jnp.concatenate(o_inter, axis=0))
            heads.append(o * lax.rsqrt(jnp.mean(o * o, axis=-1, keepdims=True) + EPS))
        b_y = jnp.concatenate(heads, axis=1) * gg_ref[...]
        gated_rows.append((b_y * jax.nn.silu(zb_ref[rows, :])).astype(BF16))
    if not has_s0:
        for h in range(N_HEADS_B):
            s_scr[h] = states[h]
            sfin_ref[0, h] = states[h]
    gated = gated_rows[0] if n_sub == 1 else jnp.concatenate(gated_rows, axis=0)
    merged = (_merge_gate(ma0_ref, ma1_ref) * _dot(ya_ref[...], waout_ref[...])
              + _merge_gate(mb0_ref, mb1_ref) * _dot(gated, wbout_ref[...]))
    merged_ref[...] = merged.astype(BF16)


def _gla_call(proj, proj_g, y_a, w_gate, b_gate, g_gain, w_a_out, w_b_out, s0, *, n_batch, seq_len, t, n_sub):
    tokens = proj.shape[0]
    has_s0 = s0 is not None
    tt = t * n_sub
    if has_s0:
        ls = seq_len
        grid = (tokens // tt, 1)
        tok = lambda b, c: b
        seq_block = tt // ls
    else:
        ls = t
        nct = seq_len // tt
        grid = (n_batch, nct)
        tok = lambda b, c: b * nct + c
        seq_block = 1
    ms, lv = _gla_levels(t, ls)
    lv = jnp.asarray(lv, dtype=BF16)
    const = lambda shape: _const_spec(shape, 2)
    st_spec = pl.BlockSpec((seq_block, N_HEADS_B, DK_B, DV_B), lambda b, c: (b, 0, 0, 0))
    in_specs = [pl.BlockSpec((tt, QK_W), lambda b, c: (tok(b, c), COL_Q // QK_W)),
                pl.BlockSpec((tt, QK_W), lambda b, c: (tok(b, c), COL_K // QK_W)),
                pl.BlockSpec((tt, W_B), lambda b, c: (tok(b, c), COL_V // W_B)),
                pl.BlockSpec((tt, W_B), lambda b, c: (tok(b, c), COL_ZB // W_B)),
                pl.BlockSpec((tt, W_A), lambda b, c: (tok(b, c), PCOL_MA // W_A)),
                pl.BlockSpec((tt, W_A), lambda b, c: (tok(b, c), PCOL_MA // W_A + 1)),
                pl.BlockSpec((tt, W_B), lambda b, c: (tok(b, c), PCOL_MB // W_B)),
                pl.BlockSpec((tt, W_B), lambda b, c: (tok(b, c), PCOL_MB // W_B + 1)),
                pl.BlockSpec((tt, GATE_RANK), lambda b, c: (tok(b, c), 0)),
                pl.BlockSpec((tt, W_A), lambda b, c: (tok(b, c), 0))]
    args = [proj, proj, proj, proj, proj, proj, proj, proj, proj_g, y_a]
    if has_s0:
        in_specs.append(st_spec)
        args.append(s0)
    in_specs += [const(lv.shape), const((GATE_RANK, QK_W)), const((1, QK_W)), const((1, W_B)),
                 const((W_A, D_MODEL)), const((W_B, D_MODEL))]
    args += [lv, w_gate, b_gate, g_gain, w_a_out, w_b_out]
    n_seq_total = tokens // seq_len
    return pl.pallas_call(
        functools.partial(_gla_kernel, t=t, n_sub=n_sub, ls=ls, has_s0=has_s0, ms=tuple(ms)),
        grid=grid,
        in_specs=in_specs,
        out_specs=[pl.BlockSpec((tt, D_MODEL), lambda b, c: (tok(b, c), 0)), st_spec],
        out_shape=[jax.ShapeDtypeStruct((tokens, D_MODEL), BF16),
                   jax.ShapeDtypeStruct((n_seq_total, N_HEADS_B, DK_B, DV_B), F32)],
        scratch_shapes=[pltpu.VMEM((N_HEADS_B, DK_B, DV_B), F32)],
        compiler_params=_cparams(2),
        name="gla_branch",
    )(*args)


def _out_kernel(m_ref, x_ref, gate_ref, w_ref, fg_ref, y_ref, acc0, acc1,
                *, tm, rc, rows_per_seq, seq_base, n_tiles):
    s = pl.program_id(0)

    def matmul(acc):
        acc[...] = _dot(m_ref[...], w_ref[...].astype(BF16))

    def epilogue(acc):
        fg = fg_ref[...]
        for i in range(tm // rc):
            r0 = i * rc
            gate = _seq_rows(gate_ref, (s - 1) * tm + r0, rc, rows_per_seq, seq_base)
            yy = x_ref[r0:r0 + rc, :] + gate * acc[r0:r0 + rc, :]
            y_ref[r0:r0 + rc, :] = yy * lax.rsqrt(jnp.mean(yy * yy, axis=-1, keepdims=True) + EPS) * fg

    accs = (acc0, acc1)

    pl.when(s == 0)(lambda: matmul(acc0))
    for par in (0, 1):
        @pl.when((s > 0) & (s < n_tiles) & (lax.rem(s, 2) == par))
        def _(par=par):
            matmul(accs[par])
            epilogue(accs[1 - par])

    pl.when(s == n_tiles)(lambda: epilogue(accs[(n_tiles - 1) % 2]))


def _out_call(merged, x2d, mod, w_out, fgain, *, rows_per_seq, seq_base, tm):
    tokens = x2d.shape[0]
    ns = mod.shape[0]
    n_tiles = tokens // tm
    rc = 128 if rows_per_seq >= 128 else 8 * rows_per_seq
    kern = functools.partial(_out_kernel, tm=tm, rc=rc, rows_per_seq=rows_per_seq, seq_base=seq_base,
                             n_tiles=n_tiles)
    prev = lambda m: (jnp.maximum(m - 1, 0), 0)
    return pl.pallas_call(
        kern,
        grid=(n_tiles + 1,),
        in_specs=[pl.BlockSpec((tm, D_MODEL), lambda m: (jnp.minimum(m, n_tiles - 1), 0)),
                  pl.BlockSpec((tm, D_MODEL), prev),
                  pl.BlockSpec((ns, D_MODEL), lambda m: (0, 2), pipeline_mode=pl.Buffered(1)),
                  _const_spec((D_MODEL, D_MODEL), 1),
                  _const_spec((1, D_MODEL), 1)],
        out_specs=pl.BlockSpec((tm, D_MODEL), prev),
        out_shape=jax.ShapeDtypeStruct((tokens, D_MODEL), F32),
        scratch_shapes=[pltpu.VMEM((tm, D_MODEL), F32), pltpu.VMEM((tm, D_MODEL), F32)],
        compiler_params=_cparams(1),
        name="out_proj",
    )(merged, x2d, mod, w_out, fgain)


def _tile(n, target):
    t = min(n, target)
    assert n % t == 0
    return t


def _layer(x, n_seq_before, mod, weights, prep, s0_ssm, s0_gla):
    (gain, w_in_t, d_skip, w_glu, b_glu, w_a_out, w_gate, b_gate, g_gain, w_b_out, w_out, fgain) = weights
    n_batch, seq_len, _ = x.shape
    tokens = n_batch * seq_len
    x2d = x.reshape(tokens, D_MODEL)
    long_seq = s0_ssm is None
    span = seq_len if long_seq else tokens
    proj, proj_g = _inproj_call(x2d, mod, gain, w_in_t, rows_per_seq=seq_len, seq_base=n_seq_before,
                                   tm=_tile(span, 2048), tn=512 if long_seq else 1024)
    tc = _tile(span, 256)
    y_a, f_re, f_im = _s5_call(proj, prep, d_skip, w_glu, b_glu, s0_ssm, n_batch=n_batch, seq_len=seq_len,
                               tc=tc, n_sub=2 if span % (2 * tc) == 0 else 1)
    if long_seq:
        t = _tile(seq_len, 128)
        n_sub = 2 if seq_len % (2 * t) == 0 else 1
    else:
        t, n_sub = _tile(tokens, 128), 1
    merged, s_fin = _gla_call(proj, proj_g, y_a, w_gate, b_gate, g_gain, w_a_out, w_b_out, s0_gla,
                              n_batch=n_batch, seq_len=seq_len, t=t, n_sub=n_sub)
    y = _out_call(merged, x2d, mod, w_out, fgain, rows_per_seq=seq_len, seq_base=n_seq_before,
                  tm=_tile(span, 512))
    st = lambda f: f.reshape(1, n_batch, N_GROUPS_A, P_STATE)
    return (y.reshape(n_batch, seq_len, D_MODEL), st(f_re), st(f_im),
            s_fin.reshape(1, n_batch, N_HEADS_B, DK_B, DV_B))


def kernel(x_prompt, x_sample, c_prompt, c_sample, state_ssm_re, state_ssm_im, state_gla, w_ada, b_ada, norm_gain, w_in, lambda_re, lambda_im, log_dt, ssm_b_re, ssm_b_im, ssm_c_re, ssm_c_im, d_skip, w_glu, b_glu, w_gate_up, b_gate, gla_norm_gain, w_a_out, w_b_out, w_out, final_norm_gain):
    assert w_ada.shape[0] == 1, "single-layer step"
    assert w_in.shape[2] == IN_COLS
    n_prompt, n_sample = x_prompt.shape[0], x_sample.shape[0]

    mod = _mod_call(jnp.concatenate([c_prompt, c_sample], axis=0), w_ada[0], b_ada)
    a_re, a_im, wb_re, wb_im, wc = _s5prep_call(lambda_re[0], lambda_im[0], log_dt[0], ssm_b_re[0], ssm_b_im[0],
                                                ssm_c_re[0], ssm_c_im[0])
    state_tiles = lambda a: a.reshape(-1, S5_TILES, SUBLANES, LANES)
    prep = (state_tiles(a_re)[0], state_tiles(a_im)[0], wb_re, wb_im, wc)

    w_in_t = jnp.swapaxes(w_in, 1, 2).reshape(IN_COLS, D_MODEL)
    w_gate = w_gate_up[0].astype(BF16)
    weights = (norm_gain, w_in_t, d_skip, w_glu[0], b_glu, w_a_out[0].astype(BF16),
               w_gate, b_gate, gla_norm_gain, w_b_out[0].astype(BF16), w_out[0],
               final_norm_gain.reshape(1, D_MODEL))

    y_p, pre, pim, pgla = _layer(x_prompt, 0, mod, weights, prep, None, None)
    y_s, sre, sim, sgla = _layer(x_sample, n_prompt, mod, weights, prep,
                                 (state_tiles(state_ssm_re[0]), state_tiles(state_ssm_im[0])), state_gla[0])
    return (y_p, y_s, pre, pim, pgla, sre, sim, sgla)
```

```python
import functools

import numpy as np
import jax
import jax.numpy as jnp
from jax import lax
from jax.experimental import pallas as pl
from jax.experimental.pallas import tpu as pltpu

F32 = jnp.float32
BF16 = jnp.bfloat16

D_MODEL = 2048
W_A = D_MODEL // 2
GROUP_A = 16
N_GROUPS_A = W_A // GROUP_A
P_STATE = 64
N_STATE = N_GROUPS_A * P_STATE
W_B = D_MODEL // 2
N_HEADS_B = 4
DK_B = W_B // 2 // N_HEADS_B
DV_B = W_B // N_HEADS_B
QK_W = N_HEADS_B * DK_B
GATE_RANK = 16
GATE_TAU = 16.0
EPS = 1e-6
LOG2_E = 1.4426950408889634

LANES = 128
SUBLANES = 8
V7X_VMEM_BYTES = 64 * 1024 * 1024
VMEM_LIMIT_BYTES = V7X_VMEM_BYTES - 8 * 1024 * 1024

S5_TILE_GROUPS = 16
S5_TILES = N_GROUPS_A // S5_TILE_GROUPS
S5_TILE_IN = S5_TILE_GROUPS * GROUP_A
S5_TILE_ST = S5_TILE_GROUPS * P_STATE
assert S5_TILE_ST == SUBLANES * LANES

COL_UA = 0
COL_ZA = COL_UA + W_A
COL_Q = COL_ZA + W_A
COL_K = COL_Q + QK_W
COL_V = COL_K + QK_W
COL_ZB = COL_V + W_B
COL_G = COL_ZB + W_B
COL_MA = COL_G + GATE_RANK
COL_MB = COL_MA + D_MODEL
IN_COLS = COL_MB + D_MODEL
PROJ_COLS = IN_COLS - GATE_RANK
PCOL_MA = COL_G
PCOL_MB = PCOL_MA + D_MODEL
assert COL_MA % SUBLANES == 0 and COL_G % GATE_RANK == 0


def _dot(a, b):
    return jnp.dot(a, b, preferred_element_type=F32)


def _dot_nt(a, b):
    return lax.dot_general(a, b, (((1,), (1,)), ((), ())), preferred_element_type=F32)


def _dot_tn(a, b):
    return lax.dot_general(a, b, (((0,), (0,)), ((), ())), preferred_element_type=F32)


def _shr(x, pow2):
    sh = int(pow2).bit_length() - 1
    assert 1 << sh == pow2
    return jnp.right_shift(x, sh)


def _cparams(n_axes):
    return pltpu.CompilerParams(dimension_semantics=("arbitrary",) * n_axes,
                                vmem_limit_bytes=VMEM_LIMIT_BYTES)


def _const_spec(shape, n_axes):
    zeros = (0,) * len(shape)
    index_map = (lambda a: zeros) if n_axes == 1 else (lambda a, b: zeros)
    return pl.BlockSpec(shape, index_map, pipeline_mode=pl.Buffered(1))


def _seq_rows(ref, row0, rc, rows_per_seq, seq_base):
    s = seq_base + lax.div(row0, jnp.int32(rows_per_seq))
    if rows_per_seq >= rc:
        return ref[pl.ds(s, 1), :]
    assert rc % rows_per_seq == 0 and rows_per_seq % SUBLANES == 0
    width = ref.shape[1]
    return jnp.concatenate([jnp.broadcast_to(ref[pl.ds(s + j, 1), :], (rows_per_seq, width))
                            for j in range(rc // rows_per_seq)], axis=0)


def _merge_gate(lo_ref, hi_ref):
    return jax.nn.sigmoid(jnp.concatenate([lo_ref[...], hi_ref[...]], axis=1))


def _mod_kernel(c_ref, w_ref, b_ref, o_ref):
    s = jax.nn.silu(c_ref[...]).astype(BF16)
    o_ref[...] = _dot(s, w_ref[...].astype(BF16)) + b_ref[...]


def _mod_call(c_all, w_ada, b_ada):
    ns = c_all.shape[0]
    tn = 1024
    return pl.pallas_call(
        _mod_kernel,
        grid=(3 * D_MODEL // tn,),
        in_specs=[_const_spec((ns, D_MODEL), 1),
                  pl.BlockSpec((D_MODEL, tn), lambda n: (0, n)),
                  pl.BlockSpec((1, tn), lambda n: (0, n))],
        out_specs=pl.BlockSpec((ns, tn), lambda n: (0, n)),
        out_shape=jax.ShapeDtypeStruct((ns, 3 * D_MODEL), F32),
        compiler_params=_cparams(1),
        name="adaln_mod",
    )(c_all, w_ada, b_ada)


def _s5prep_kernel(lr_ref, li_ref, ldt_ref, bre_ref, bim_ref, cre_ref, cim_ref,
                   are_ref, aim_ref, wbre_ref, wbim_ref, wc_ref):
    dt = jnp.exp(ldt_ref[...])
    lr, li = lr_ref[...], li_ref[...]
    mag = jnp.exp(lr * dt)
    ab_re, ab_im = mag * jnp.cos(li * dt), mag * jnp.sin(li * dt)
    nr, ni = ab_re - 1.0, ab_im
    den = lr * lr + li * li
    cf_re = (nr * lr + ni * li) / den
    cf_im = (ni * lr - nr * li) / den
    are_ref[...] = ab_re
    aim_ref[...] = ab_im
    r = lax.broadcasted_iota(jnp.int32, (S5_TILE_IN, S5_TILE_ST), 0)
    c = lax.broadcasted_iota(jnp.int32, (S5_TILE_IN, S5_TILE_ST), 1)
    on_b = _shr(r, GROUP_A) == _shr(c, P_STATE)
    bre, bim = bre_ref[...], bim_ref[...]
    rep = lambda a: jnp.concatenate([a] * S5_TILE_GROUPS, axis=0)
    wbre_ref[...] = jnp.where(on_b, rep(cf_re * bre - cf_im * bim), 0.0).astype(BF16)
    wbim_ref[...] = jnp.where(on_b, rep(cf_re * bim + cf_im * bre), 0.0).astype(BF16)
    h = lax.broadcasted_iota(jnp.int32, (GROUP_A, S5_TILE_IN), 0)
    c = lax.broadcasted_iota(jnp.int32, (GROUP_A, S5_TILE_IN), 1)
    spread = jnp.where((c & (GROUP_A - 1)) == h, 1.0, 0.0).astype(BF16)
    r = lax.broadcasted_iota(jnp.int32, (S5_TILE_ST, S5_TILE_IN), 0)
    c = lax.broadcasted_iota(jnp.int32, (S5_TILE_ST, S5_TILE_IN), 1)
    on_c = _shr(r, P_STATE) == _shr(c, GROUP_A)
    wc_ref[0:S5_TILE_ST, :] = jnp.where(on_c, _dot(cre_ref[...].astype(BF16), spread), 0.0).astype(BF16)
    wc_ref[S5_TILE_ST:2 * S5_TILE_ST, :] = jnp.where(on_c, -_dot(cim_ref[...].astype(BF16), spread), 0.0).astype(BF16)


def _s5prep_call(lam_re, lam_im, log_dt, b_re, b_im, c_re, c_im):
    row = lambda a: a.reshape(1, N_STATE)
    ldt = jnp.broadcast_to(log_dt[:, None], (N_GROUPS_A, P_STATE))
    bt = lambda b: b.transpose(2, 0, 1).reshape(GROUP_A, N_STATE)
    ct = lambda c: c.transpose(0, 2, 1).reshape(N_STATE, GROUP_A)
    rspec = pl.BlockSpec((1, S5_TILE_ST), lambda i: (0, i))
    bspec = pl.BlockSpec((S5_TILE_IN, S5_TILE_ST), lambda i: (0, i))
    bin_spec = pl.BlockSpec((GROUP_A, S5_TILE_ST), lambda i: (0, i))
    cin_spec = pl.BlockSpec((S5_TILE_ST, GROUP_A), lambda i: (i, 0))
    return pl.pallas_call(
        _s5prep_kernel,
        grid=(S5_TILES,),
        in_specs=[rspec, rspec, rspec, bin_spec, bin_spec, cin_spec, cin_spec],
        out_specs=[rspec, rspec, bspec, bspec,
                   pl.BlockSpec((None, 2 * S5_TILE_ST, S5_TILE_IN), lambda i: (i, 0, 0))],
        out_shape=[jax.ShapeDtypeStruct((1, N_STATE), F32), jax.ShapeDtypeStruct((1, N_STATE), F32),
                   jax.ShapeDtypeStruct((S5_TILE_IN, N_STATE), BF16),
                   jax.ShapeDtypeStruct((S5_TILE_IN, N_STATE), BF16),
                   jax.ShapeDtypeStruct((S5_TILES, 2 * S5_TILE_ST, S5_TILE_IN), BF16)],
        compiler_params=_cparams(1),
        name="s5_prep",
    )(row(lam_re), row(lam_im), row(ldt), bt(b_re), bt(b_im), ct(c_re), ct(c_im))


def _inproj_kernel(x_hbm, shift_ref, scale_ref, gain_ref, w_ref, wt_ref, p_ref, pt_ref, h_scr, x_buf, x_sem,
                   *, tm, rc, rows_per_seq, seq_base, n_mtiles):
    mt, n = pl.program_id(0), pl.program_id(1)

    def x_copy(tile):
        return pltpu.make_async_copy(x_hbm.at[pl.ds(pl.multiple_of(tile * tm, tm), tm), :], x_buf, x_sem)

    @pl.when(n == 0)
    def _():
        pl.when(mt == 0)(lambda: x_copy(0).start())
        x_copy(mt).wait()
        gain = gain_ref[...]

        def body(i, carry):
            r0 = pl.multiple_of(i * rc, rc)
            row0 = mt * tm + r0
            xx = x_buf[pl.ds(r0, rc), :]
            y = xx * lax.rsqrt(jnp.mean(xx * xx, axis=-1, keepdims=True) + EPS)
            sc = _seq_rows(scale_ref, row0, rc, rows_per_seq, seq_base)
            sh = _seq_rows(shift_ref, row0, rc, rows_per_seq, seq_base)
            h_scr[pl.ds(r0, rc), :] = (y * (gain * (1.0 + sc)) + sh).astype(BF16)
            return carry

        lax.fori_loop(0, tm // rc, body, 0)
        pl.when(mt + 1 < n_mtiles)(lambda: x_copy(mt + 1).start())
        pt_ref[...] = _dot_nt(h_scr[...], wt_ref[...].astype(BF16))

    p_ref[...] = _dot_nt(h_scr[...], w_ref[...].astype(BF16))


def _inproj_call(x2d, mod, gain, w_in_t, *, rows_per_seq, seq_base, tm, tn):
    tokens = x2d.shape[0]
    ns = mod.shape[0]
    rc = 128 if rows_per_seq >= 128 else 8 * rows_per_seq
    kern = functools.partial(_inproj_kernel, tm=tm, rc=rc, rows_per_seq=rows_per_seq, seq_base=seq_base,
                             n_mtiles=tokens // tm)
    assert COL_G % tn == 0
    n_before = COL_G // tn

    def w_rows(m, n):
        tiles = jnp.where(n < n_before, n * (tn // SUBLANES), COL_MA // SUBLANES + (n - n_before) * (tn // SUBLANES))
        return (tiles * SUBLANES, 0)

    return pl.pallas_call(
        kern,
        grid=(tokens // tm, PROJ_COLS // tn),
        in_specs=[pl.BlockSpec(memory_space=pl.ANY),
                  pl.BlockSpec((ns, D_MODEL), lambda m, n: (0, 0), pipeline_mode=pl.Buffered(1)),
                  pl.BlockSpec((ns, D_MODEL), lambda m, n: (0, 1), pipeline_mode=pl.Buffered(1)),
                  _const_spec((1, D_MODEL), 2),
                  pl.BlockSpec((pl.Element(tn), pl.Element(D_MODEL)), w_rows),
                  pl.BlockSpec((GATE_RANK, D_MODEL), lambda m, n: (COL_G // GATE_RANK, 0),
                               pipeline_mode=pl.Buffered(1))],
        out_specs=[pl.BlockSpec((tm, tn), lambda m, n: (m, n)),
                   pl.BlockSpec((tm, GATE_RANK), lambda m, n: (m, 0))],
        out_shape=[jax.ShapeDtypeStruct((tokens, PROJ_COLS), F32),
                   jax.ShapeDtypeStruct((tokens, GATE_RANK), F32)],
        scratch_shapes=[pltpu.VMEM((tm, D_MODEL), BF16), pltpu.VMEM((tm, D_MODEL), F32),
                        pltpu.SemaphoreType.DMA(())],
        compiler_params=_cparams(2),
        name="in_proj",
    )(x2d, mod, mod, gain, w_in_t, w_in_t)


def _s5_kernel(*refs, tc, n_sub, ls, has_h0):
    n_in = 12 if has_h0 else 10
    if has_h0:
        (u_ref, za_ref, h0re_ref, h0im_ref, wbre_ref, wbim_ref, wc_ref,
         are_ref, aim_ref, dskip_ref, wglu_ref, bglu_ref) = refs[:n_in]
    else:
        (u_ref, za_ref, wbre_ref, wbim_ref, wc_ref,
         are_ref, aim_ref, dskip_ref, wglu_ref, bglu_ref) = refs[:n_in]
    gated_ref, fre_ref, fim_ref = refs[n_in:n_in + 3]
    scr = refs[n_in + 3:n_in + 3 + 2 * n_sub * S5_TILES]
    xr = [scr[2 * c * S5_TILES:(2 * c + 1) * S5_TILES] for c in range(n_sub)]
    xi = [scr[(2 * c + 1) * S5_TILES:(2 * c + 2) * S5_TILES] for c in range(n_sub)]
    hc_re, hc_im = refs[n_in + 3 + 2 * n_sub * S5_TILES:]
    ct = pl.program_id(1)
    n_seq = tc // ls

    us = []
    for c in range(n_sub):
        u = u_ref[c * tc:(c + 1) * tc, :]
        us.append(u)
        ub = u.astype(BF16)
        for i in range(S5_TILES):
            ui = ub[:, i * S5_TILE_IN:(i + 1) * S5_TILE_IN]
            for w_ref, x_scr in ((wbre_ref, xr[c][i]), (wbim_ref, xi[c][i])):
                res = _dot(ui, w_ref[:, i * S5_TILE_ST:(i + 1) * S5_TILE_ST])
                for j in range(SUBLANES):
                    x_scr[pl.ds(j, tc, stride=SUBLANES), :] = res[:, j * LANES:(j + 1) * LANES]

    a_re = [are_ref[i] for i in range(S5_TILES)]
    a_im = [aim_ref[i] for i in range(S5_TILES)]
    if not has_h0:
        h_re = [jnp.where(ct == 0, 0.0, hc_re[i]) for i in range(S5_TILES)]
        h_im = [jnp.where(ct == 0, 0.0, hc_im[i]) for i in range(S5_TILES)]
    for c in range(n_sub):
        for s in range(n_seq):
            if has_h0:
                h_re = [h0re_ref[c * n_seq + s, i] for i in range(S5_TILES)]
                h_im = [h0im_ref[c * n_seq + s, i] for i in range(S5_TILES)]
            for tl in range(ls):
                r0 = (s * ls + tl) * SUBLANES
                for i in range(S5_TILES):
                    nr = a_re[i] * h_re[i] - a_im[i] * h_im[i] + xr[c][i][r0:r0 + SUBLANES, :]
                    ni = a_re[i] * h_im[i] + a_im[i] * h_re[i] + xi[c][i][r0:r0 + SUBLANES, :]
                    xr[c][i][r0:r0 + SUBLANES, :] = nr
                    xi[c][i][r0:r0 + SUBLANES, :] = ni
                    h_re[i], h_im[i] = nr, ni
            if has_h0:
                for i in range(S5_TILES):
                    fre_ref[c * n_seq + s, i] = h_re[i]
                    fim_ref[c * n_seq + s, i] = h_im[i]
    if not has_h0:
        for i in range(S5_TILES):
            hc_re[i] = h_re[i]
            hc_im[i] = h_im[i]
            fre_ref[0, i] = h_re[i]
            fim_ref[0, i] = h_im[i]

    for c in range(n_sub):
        ys = []
        for i in range(S5_TILES):
            parts = [x_scr[pl.ds(j, tc, stride=SUBLANES), :]
                     for x_scr in (xr[c][i], xi[c][i]) for j in range(SUBLANES)]
            ys.append(_dot(jnp.concatenate(parts, axis=1).astype(BF16), wc_ref[i]))
        y = jnp.concatenate(ys, axis=1) + dskip_ref[...] * us[c]
        g = jax.nn.gelu(y)
        a_y = g * jax.nn.sigmoid(_dot(g.astype(BF16), wglu_ref[...]) + bglu_ref[...])
        gated_ref[c * tc:(c + 1) * tc, :] = (a_y * jax.nn.silu(za_ref[c * tc:(c + 1) * tc, :])).astype(BF16)


def _s5_call(proj, prep, d_skip, w_glu, b_glu, h0, *, n_batch, seq_len, tc, n_sub):
    a_re, a_im, wb_re, wb_im, wc = prep
    tokens = proj.shape[0]
    has_h0 = h0 is not None
    tt = tc * n_sub
    if has_h0:
        ls = seq_len
        grid = (tokens // tt, 1)
        tok = lambda b, c: b
        seq_block = tt // ls
    else:
        ls = tc
        nct = seq_len // tt
        grid = (n_batch, nct)
        tok = lambda b, c: b * nct + c
        seq_block = 1
    const = lambda shape: _const_spec(shape, 2)
    st_spec = pl.BlockSpec((seq_block, S5_TILES, SUBLANES, LANES), lambda b, c: (b, 0, 0, 0))
    in_specs = [pl.BlockSpec((tt, W_A), lambda b, c: (tok(b, c), COL_UA // W_A)),
                pl.BlockSpec((tt, W_A), lambda b, c: (tok(b, c), COL_ZA // W_A))]
    args = [proj] * 2
    if has_h0:
        in_specs += [st_spec, st_spec]
        args += [h0[0], h0[1]]
    in_specs += [const((S5_TILE_IN, N_STATE)), const((S5_TILE_IN, N_STATE)),
                 const((S5_TILES, 2 * S5_TILE_ST, S5_TILE_IN)),
                 const((S5_TILES, SUBLANES, LANES)), const((S5_TILES, SUBLANES, LANES)),
                 const((1, W_A)), const((W_A, W_A)), const((1, W_A))]
    args += [wb_re, wb_im, wc, a_re, a_im, d_skip, w_glu, b_glu]
    n_seq_total = tokens // seq_len
    st_shape = jax.ShapeDtypeStruct((n_seq_total, S5_TILES, SUBLANES, LANES), F32)
    return pl.pallas_call(
        functools.partial(_s5_kernel, tc=tc, n_sub=n_sub, ls=ls, has_h0=has_h0),
        grid=grid,
        in_specs=in_specs,
        out_specs=[pl.BlockSpec((tt, W_A), lambda b, c: (tok(b, c), 0)), st_spec, st_spec],
        out_shape=[jax.ShapeDtypeStruct((tokens, W_A), BF16), st_shape, st_shape],
        scratch_shapes=([pltpu.VMEM((tc * SUBLANES, LANES), F32)] * (2 * n_sub * S5_TILES)
                        + [pltpu.VMEM((S5_TILES, SUBLANES, LANES), F32)] * 2),
        compiler_params=_cparams(2),
        name="s5_branch",
    )(*args)


def _gla_levels(t, ls):
    ms = []
    m = ls
    while m >= 1:
        ms.append(m)
        m //= 2
    idx = np.arange(t)
    mats = []
    for m in ms:
        if m == 1:
            continue
        same = (idx[:, None] // m) == (idx[None, :] // m)
        mats.append(same & (idx[None, :] <= idx[:, None]))
        mats.append(same)
    return ms, np.concatenate(mats, axis=0).astype(np.float32)


def _gla_kernel(*refs, t, n_sub, ls, has_s0, ms):
    if has_s0:
        (q_ref, k_ref, v_ref, zb_ref, ma0_ref, ma1_ref, mb0_ref, mb1_ref, glr_ref, ya_ref,
         s0_ref, lv_ref, wg_ref, bg_ref, gg_ref, waout_ref, wbout_ref, merged_ref, sfin_ref, s_scr) = refs
        assert n_sub == 1
    else:
        (q_ref, k_ref, v_ref, zb_ref, ma0_ref, ma1_ref, mb0_ref, mb1_ref, glr_ref, ya_ref,
         lv_ref, wg_ref, bg_ref, gg_ref, waout_ref, wbout_ref, merged_ref, sfin_ref, s_scr) = refs
        assert ls == t
    ct = pl.program_id(1)
    n_seq = t // ls

    ti = lax.broadcasted_iota(jnp.int32, (t, t), 0)
    si = lax.broadcasted_iota(jnp.int32, (t, t), 1)
    masks = {}
    for li in range(1, len(ms)):
        m = ms[li]
        masks[li] = ((_shr(ti, 2 * m) == _shr(si, 2 * m)) & ((_shr(ti, m) & 1) == 1) & ((_shr(si, m) & 1) == 0))
    rid = lax.broadcasted_iota(jnp.int32, (t, 1), 0)

    if not has_s0:
        states = [jnp.where(ct == 0, 0.0, s_scr[h]) for h in range(N_HEADS_B)]
    ga = jax.nn.log_sigmoid(_dot(glr_ref[...].astype(BF16), wg_ref[...]) + bg_ref[...]) * (LOG2_E / GATE_TAU)
    ga_b = ga.astype(BF16)
    ga_r = ga_b.astype(F32)
    e_all = _dot(lv_ref[...], jnp.concatenate([ga_b[c * t:(c + 1) * t] for c in range(n_sub)], axis=1))
    gated_rows = []
    for c in range(n_sub):
        rows = slice(c * t, (c + 1) * t)
        cols = slice(c * QK_W, (c + 1) * QK_W)
        cum = lambda li: e_all[2 * li * t:(2 * li + 1) * t, cols]
        tot = lambda li: e_all[(2 * li + 1) * t:(2 * li + 2) * t, cols]

        q, k, v = q_ref[rows, :], k_ref[rows, :], v_ref[rows, :]
        heads = []
        for h in range(N_HEADS_B):
            hs = slice(h * DK_B, (h + 1) * DK_B)
            qs = q[:, hs] * (DK_B ** -0.5)
            kh = k[:, hs]
            vh = v[:, h * DV_B:(h + 1) * DV_B]
            vb = vh.astype(BF16)
            b = cum(0)[:, hs]
            bend = tot(0)[:, hs]
            q_dec = (qs * jnp.exp2(b)).astype(BF16)
            k_dec = kh * jnp.exp2(bend - b)
            scores = jnp.zeros((t, t), F32)
            for li in range(1, len(ms)):
                if ms[li] == 1:
                    qm = (qs * jnp.exp2(ga_r[rows, hs])).astype(BF16)
                    km = kh.astype(BF16)
                else:
                    eq = cum(li)[:, hs]
                    qm = (qs * jnp.exp2(eq)).astype(BF16)
                    km = (kh * jnp.exp2(tot(li)[:, hs] - eq)).astype(BF16)
                scores = jnp.where(masks[li], _dot_nt(qm, km), scores)
            o = _dot(scores.astype(BF16), vb) + jnp.sum(qs * kh, axis=-1, keepdims=True) * vh
            o_inter = []
            for i in range(n_seq):
                r0 = i * ls
                s_old = s0_ref[i, h] if has_s0 else states[h]
                o_inter.append(_dot(q_dec[r0:r0 + ls], s_old.astype(BF16)))
                kd = k_dec if n_seq == 1 else jnp.where((rid >= r0) & (rid < r0 + ls), k_dec, 0.0)
                d_end = jnp.exp2(bend[r0:r0 + 1, :])
                d_col = jnp.broadcast_to(d_end, (DK_B, DK_B)).T
                s_new = jnp.concatenate([d_col] * (DV_B // DK_B), axis=1) * s_old + _dot_tn(kd.astype(BF16), vb)
                if has_s0:
                    sfin_ref[i, h] = s_new
                else:
                    states[h] = s_new
            o = o + (o_inter[0] if n_seq == 1 else jnp.concatenate(o_inter, axis=0))
            heads.append(o * lax.rsqrt(jnp.mean(o * o, axis=-1, keepdims=True) + EPS))
        b_y = jnp.concatenate(heads, axis=1) * gg_ref[...]
        gated_rows.append((b_y * jax.nn.silu(zb_ref[rows, :])).astype(BF16))
    if not has_s0:
        for h in range(N_HEADS_B):
            s_scr[h] = states[h]
            sfin_ref[0, h] = states[h]
    gated = gated_rows[0] if n_sub == 1 else jnp.concatenate(gated_rows, axis=0)
    merged = (_merge_gate(ma0_ref, ma1_ref) * _dot(ya_ref[...], waout_ref[...])
              + _merge_gate(mb0_ref, mb1_ref) * _dot(gated, wbout_ref[...]))
    merged_ref[...] = merged.astype(BF16)


def _gla_call(proj, proj_g, y_a, w_gate, b_gate, g_gain, w_a_out, w_b_out, s0, *, n_batch, seq_len, t, n_sub):
    tokens = proj.shape[0]
    has_s0 = s0 is not None
    tt = t * n_sub
    if has_s0:
        ls = seq_len
        grid = (tokens // tt, 1)
        tok = lambda b, c: b
        seq_block = tt // ls
    else:
        ls = t
        nct = seq_len // tt
        grid = (n_batch, nct)
        tok = lambda b, c: b * nct + c
        seq_block = 1
    ms, lv = _gla_levels(t, ls)
    lv = jnp.asarray(lv, dtype=BF16)
    const = lambda shape: _const_spec(shape, 2)
    st_spec = pl.BlockSpec((seq_block, N_HEADS_B, DK_B, DV_B), lambda b, c: (b, 0, 0, 0))
    in_specs = [pl.BlockSpec((tt, QK_W), lambda b, c: (tok(b, c), COL_Q // QK_W)),
                pl.BlockSpec((tt, QK_W), lambda b, c: (tok(b, c), COL_K // QK_W)),
                pl.BlockSpec((tt, W_B), lambda b, c: (tok(b, c), COL_V // W_B)),
                pl.BlockSpec((tt, W_B), lambda b, c: (tok(b, c), COL_ZB // W_B)),
                pl.BlockSpec((tt, W_A), lambda b, c: (tok(b, c), PCOL_MA // W_A)),
                pl.BlockSpec((tt, W_A), lambda b, c: (tok(b, c), PCOL_MA // W_A + 1)),
                pl.BlockSpec((tt, W_B), lambda b, c: (tok(b, c), PCOL_MB // W_B)),
                pl.BlockSpec((tt, W_B), lambda b, c: (tok(b, c), PCOL_MB // W_B + 1)),
                pl.BlockSpec((tt, GATE_RANK), lambda b, c: (tok(b, c), 0)),
                pl.BlockSpec((tt, W_A), lambda b, c: (tok(b, c), 0))]
    args = [proj, proj, proj, proj, proj, proj, proj, proj, proj_g, y_a]
    if has_s0:
        in_specs.append(st_spec)
        args.append(s0)
    in_specs += [const(lv.shape), const((GATE_RANK, QK_W)), const((1, QK_W)), const((1, W_B)),
                 const((W_A, D_MODEL)), const((W_B, D_MODEL))]
    args += [lv, w_gate, b_gate, g_gain, w_a_out, w_b_out]
    n_seq_total = tokens // seq_len
    return pl.pallas_call(
        functools.partial(_gla_kernel, t=t, n_sub=n_sub, ls=ls, has_s0=has_s0, ms=tuple(ms)),
        grid=grid,
        in_specs=in_specs,
        out_specs=[pl.BlockSpec((tt, D_MODEL), lambda b, c: (tok(b, c), 0)), st_spec],
        out_shape=[jax.ShapeDtypeStruct((tokens, D_MODEL), BF16),
                   jax.ShapeDtypeStruct((n_seq_total, N_HEADS_B, DK_B, DV_B), F32)],
        scratch_shapes=[pltpu.VMEM((N_HEADS_B, DK_B, DV_B), F32)],
        compiler_params=_cparams(2),
        name="gla_branch",
    )(*args)


def _out_kernel(m_ref, x_ref, gate_ref, w_ref, fg_ref, y_ref, acc0, acc1,
                *, tm, rc, rows_per_seq, seq_base, n_tiles):
    s = pl.program_id(0)

    def matmul(acc):
        acc[...] = _dot(m_ref[...], w_ref[...])

    def epilogue(acc):
        fg = fg_ref[...]
        for i in range(tm // rc):
            r0 = i * rc
            gate = _seq_rows(gate_ref, (s - 1) * tm + r0, rc, rows_per_seq, seq_base)
            yy = x_ref[r0:r0 + rc, :] + gate * acc[r0:r0 + rc, :]
            y_ref[r0:r0 + rc, :] = yy * lax.rsqrt(jnp.mean(yy * yy, axis=-1, keepdims=True) + EPS) * fg

    accs = (acc0, acc1)

    pl.when(s == 0)(lambda: matmul(acc0))
    for par in (0, 1):
        @pl.when((s > 0) & (s < n_tiles) & (lax.rem(s, 2) == par))
        def _(par=par):
            matmul(accs[par])
            epilogue(accs[1 - par])

    pl.when(s == n_tiles)(lambda: epilogue(accs[(n_tiles - 1) % 2]))


def _out_call(merged, x2d, mod, w_out, fgain, *, rows_per_seq, seq_base, tm):
    tokens = x2d.shape[0]
    ns = mod.shape[0]
    n_tiles = tokens // tm
    rc = 128 if rows_per_seq >= 128 else 8 * rows_per_seq
    kern = functools.partial(_out_kernel, tm=tm, rc=rc, rows_per_seq=rows_per_seq, seq_base=seq_base,
                             n_tiles=n_tiles)
    prev = lambda m: (jnp.maximum(m - 1, 0), 0)
    return pl.pallas_call(
        kern,
        grid=(n_tiles + 1,),
        in_specs=[pl.BlockSpec((tm, D_MODEL), lambda m: (jnp.minimum(m, n_tiles - 1), 0)),
                  pl.BlockSpec((tm, D_MODEL), prev),
                  pl.BlockSpec((ns, D_MODEL), lambda m: (0, 2), pipeline_mode=pl.Buffered(1)),
                  _const_spec((D_MODEL, D_MODEL), 1),
                  _const_spec((1, D_MODEL), 1)],
        out_specs=pl.BlockSpec((tm, D_MODEL), prev),
        out_shape=jax.ShapeDtypeStruct((tokens, D_MODEL), F32),
        scratch_shapes=[pltpu.VMEM((tm, D_MODEL), F32), pltpu.VMEM((tm, D_MODEL), F32)],
        compiler_params=_cparams(1),
        name="out_proj",
    )(merged, x2d, mod, w_out, fgain)


def _tile(n, target):
    t = min(n, target)
    assert n % t == 0
    return t


def _plan(seq_len, tokens, long_seq):
    span = seq_len if long_seq else tokens
    s5_tc = _tile(span, 256)
    gla_t = _tile(span, 128)
    return dict(in_tm=_tile(span, 2048), in_tn=512 if long_seq else 1024,
                s5_tc=s5_tc, s5_sub=2 if span % (2 * s5_tc) == 0 else 1,
                gla_t=gla_t, gla_sub=2 if long_seq and seq_len % (2 * gla_t) == 0 else 1,
                out_tm=_tile(span, 512))


def _layer(x, n_seq_before, mod, weights, prep, s0_ssm, s0_gla):
    (gain, w_in_t, d_skip, w_glu, b_glu, w_a_out, w_gate, b_gate, g_gain, w_b_out, w_out, fgain) = weights
    n_batch, seq_len, _ = x.shape
    tokens = n_batch * seq_len
    x2d = x.reshape(tokens, D_MODEL)
    plan = _plan(seq_len, tokens, long_seq=s0_ssm is None)
    proj, proj_g = _inproj_call(x2d, mod, gain, w_in_t, rows_per_seq=seq_len, seq_base=n_seq_before,
                                tm=plan["in_tm"], tn=plan["in_tn"])
    y_a, f_re, f_im = _s5_call(proj, prep, d_skip, w_glu, b_glu, s0_ssm, n_batch=n_batch, seq_len=seq_len,
                               tc=plan["s5_tc"], n_sub=plan["s5_sub"])
    merged, s_fin = _gla_call(proj, proj_g, y_a, w_gate, b_gate, g_gain, w_a_out, w_b_out, s0_gla,
                              n_batch=n_batch, seq_len=seq_len, t=plan["gla_t"], n_sub=plan["gla_sub"])
    y = _out_call(merged, x2d, mod, w_out, fgain, rows_per_seq=seq_len, seq_base=n_seq_before,
                  tm=plan["out_tm"])
    st = lambda f: f.reshape(1, n_batch, N_GROUPS_A, P_STATE)
    return (y.reshape(n_batch, seq_len, D_MODEL), st(f_re), st(f_im),
            s_fin.reshape(1, n_batch, N_HEADS_B, DK_B, DV_B))


def kernel(x_prompt, x_sample, c_prompt, c_sample, state_ssm_re, state_ssm_im, state_gla, w_ada, b_ada, norm_gain, w_in, lambda_re, lambda_im, log_dt, ssm_b_re, ssm_b_im, ssm_c_re, ssm_c_im, d_skip, w_glu, b_glu, w_gate_up, b_gate, gla_norm_gain, w_a_out, w_b_out, w_out, final_norm_gain):
    assert w_ada.shape[0] == 1, "single-layer step"
    assert w_in.shape[2] == IN_COLS
    n_prompt, n_sample = x_prompt.shape[0], x_sample.shape[0]

    mod = _mod_call(jnp.concatenate([c_prompt, c_sample], axis=0), w_ada[0], b_ada)
    a_re, a_im, wb_re, wb_im, wc = _s5prep_call(lambda_re[0], lambda_im[0], log_dt[0], ssm_b_re[0], ssm_b_im[0],
                                                ssm_c_re[0], ssm_c_im[0])
    state_tiles = lambda a: a.reshape(-1, S5_TILES, SUBLANES, LANES)
    prep = (state_tiles(a_re)[0], state_tiles(a_im)[0], wb_re, wb_im, wc)

    w_in_t = jnp.swapaxes(w_in, 1, 2).reshape(IN_COLS, D_MODEL)
    w_gate = w_gate_up[0].astype(BF16)
    weights = (norm_gain, w_in_t, d_skip, w_glu[0].astype(BF16), b_glu, w_a_out[0].astype(BF16),
               w_gate, b_gate, gla_norm_gain, w_b_out[0].astype(BF16), w_out[0].astype(BF16),
               final_norm_gain.reshape(1, D_MODEL))

    y_p, pre, pim, pgla = _layer(x_prompt, 0, mod, weights, prep, None, None)
    y_s, sre, sim, sgla = _layer(x_sample, n_prompt, mod, weights, prep,
                                 (state_tiles(state_ssm_re[0]), state_tiles(state_ssm_im[0])), state_gla[0])
    return (y_p, y_s, pre, pim, pgla, sre, sim, sgla)
```

```python
import functools

import numpy as np
import jax
import jax.numpy as jnp
from jax import lax
from jax.experimental import pallas as pl
from jax.experimental.pallas import tpu as pltpu

F32 = jnp.float32
BF16 = jnp.bfloat16

D_MODEL = 2048
W_A = D_MODEL // 2
GROUP_A = 16
N_GROUPS_A = W_A // GROUP_A
P_STATE = 64
N_STATE = N_GROUPS_A * P_STATE
W_B = D_MODEL // 2
N_HEADS_B = 4
DK_B = W_B // 2 // N_HEADS_B
DV_B = W_B // N_HEADS_B
QK_W = N_HEADS_B * DK_B
GATE_RANK = 16
GATE_TAU = 16.0
EPS = 1e-6
LOG2_E = 1.4426950408889634

LANES = 128
SUBLANES = 8
V7X_VMEM_BYTES = 64 * 1024 * 1024
VMEM_LIMIT_BYTES = V7X_VMEM_BYTES - 8 * 1024 * 1024

S5_TILE_GROUPS = 16
S5_TILES = N_GROUPS_A // S5_TILE_GROUPS
S5_TILE_IN = S5_TILE_GROUPS * GROUP_A
S5_TILE_ST = S5_TILE_GROUPS * P_STATE
assert S5_TILE_ST == SUBLANES * LANES

COL_UA = 0
COL_ZA = COL_UA + W_A
COL_Q = COL_ZA + W_A
COL_K = COL_Q + QK_W
COL_V = COL_K + QK_W
COL_ZB = COL_V + W_B
COL_G = COL_ZB + W_B
COL_MA = COL_G + GATE_RANK
COL_MB = COL_MA + D_MODEL
IN_COLS = COL_MB + D_MODEL
PROJ_COLS = IN_COLS - GATE_RANK
PCOL_MA = COL_G
PCOL_MB = PCOL_MA + D_MODEL
assert COL_MA % SUBLANES == 0 and COL_G % GATE_RANK == 0


def _dot(a, b):
    return jnp.dot(a, b, preferred_element_type=F32)


def _dot_nt(a, b):
    return lax.dot_general(a, b, (((1,), (1,)), ((), ())), preferred_element_type=F32)


def _dot_tn(a, b):
    return lax.dot_general(a, b, (((0,), (0,)), ((), ())), preferred_element_type=F32)


def _shr(x, pow2):
    sh = int(pow2).bit_length() - 1
    assert 1 << sh == pow2
    return jnp.right_shift(x, sh)


def _cparams(n_axes):
    return pltpu.CompilerParams(dimension_semantics=("arbitrary",) * n_axes,
                                vmem_limit_bytes=VMEM_LIMIT_BYTES)


def _const_spec(shape, n_axes):
    zeros = (0,) * len(shape)
    index_map = (lambda a: zeros) if n_axes == 1 else (lambda a, b: zeros)
    return pl.BlockSpec(shape, index_map, pipeline_mode=pl.Buffered(1))


def _seq_rows(ref, row0, rc, rows_per_seq, seq_base):
    s = seq_base + lax.div(row0, jnp.int32(rows_per_seq))
    if rows_per_seq >= rc:
        return ref[pl.ds(s, 1), :]
    assert rc % rows_per_seq == 0 and rows_per_seq % SUBLANES == 0
    width = ref.shape[1]
    return jnp.concatenate([jnp.broadcast_to(ref[pl.ds(s + j, 1), :], (rows_per_seq, width))
                            for j in range(rc // rows_per_seq)], axis=0)


def _merge_gate(lo_ref, hi_ref):
    return jax.nn.sigmoid(jnp.concatenate([lo_ref[...], hi_ref[...]], axis=1).astype(F32))


def _mod_kernel(c_ref, w_ref, b_ref, o_ref):
    s = jax.nn.silu(c_ref[...]).astype(BF16)
    o_ref[...] = _dot(s, w_ref[...].astype(BF16)) + b_ref[...]


def _mod_call(c_all, w_ada, b_ada):
    ns = c_all.shape[0]
    tn = 1024
    return pl.pallas_call(
        _mod_kernel,
        grid=(3 * D_MODEL // tn,),
        in_specs=[_const_spec((ns, D_MODEL), 1),
                  pl.BlockSpec((D_MODEL, tn), lambda n: (0, n)),
                  pl.BlockSpec((1, tn), lambda n: (0, n))],
        out_specs=pl.BlockSpec((ns, tn), lambda n: (0, n)),
        out_shape=jax.ShapeDtypeStruct((ns, 3 * D_MODEL), F32),
        compiler_params=_cparams(1),
        name="adaln_mod",
    )(c_all, w_ada, b_ada)


def _s5prep_kernel(lr_ref, li_ref, ldt_ref, bre_ref, bim_ref, cre_ref, cim_ref,
                   are_ref, aim_ref, wbre_ref, wbim_ref, wc_ref):
    dt = jnp.exp(ldt_ref[...])
    lr, li = lr_ref[...], li_ref[...]
    mag = jnp.exp(lr * dt)
    ab_re, ab_im = mag * jnp.cos(li * dt), mag * jnp.sin(li * dt)
    nr, ni = ab_re - 1.0, ab_im
    den = lr * lr + li * li
    cf_re = (nr * lr + ni * li) / den
    cf_im = (ni * lr - nr * li) / den
    are_ref[...] = ab_re
    aim_ref[...] = ab_im
    r = lax.broadcasted_iota(jnp.int32, (S5_TILE_IN, S5_TILE_ST), 0)
    c = lax.broadcasted_iota(jnp.int32, (S5_TILE_IN, S5_TILE_ST), 1)
    on_b = _shr(r, GROUP_A) == _shr(c, P_STATE)
    bre, bim = bre_ref[...], bim_ref[...]
    rep = lambda a: jnp.concatenate([a] * S5_TILE_GROUPS, axis=0)
    wbre_ref[...] = jnp.where(on_b, rep(cf_re * bre - cf_im * bim), 0.0).astype(BF16)
    wbim_ref[...] = jnp.where(on_b, rep(cf_re * bim + cf_im * bre), 0.0).astype(BF16)
    h = lax.broadcasted_iota(jnp.int32, (GROUP_A, S5_TILE_IN), 0)
    c = lax.broadcasted_iota(jnp.int32, (GROUP_A, S5_TILE_IN), 1)
    spread = jnp.where((c & (GROUP_A - 1)) == h, 1.0, 0.0).astype(BF16)
    r = lax.broadcasted_iota(jnp.int32, (S5_TILE_ST, S5_TILE_IN), 0)
    c = lax.broadcasted_iota(jnp.int32, (S5_TILE_ST, S5_TILE_IN), 1)
    on_c = _shr(r, P_STATE) == _shr(c, GROUP_A)
    wc_ref[0:S5_TILE_ST, :] = jnp.where(on_c, _dot(cre_ref[...].astype(BF16), spread), 0.0).astype(BF16)
    wc_ref[S5_TILE_ST:2 * S5_TILE_ST, :] = jnp.where(on_c, -_dot(cim_ref[...].astype(BF16), spread), 0.0).astype(BF16)


def _s5prep_call(lam_re, lam_im, log_dt, b_re, b_im, c_re, c_im):
    row = lambda a: a.reshape(1, N_STATE)
    ldt = jnp.broadcast_to(log_dt[:, None], (N_GROUPS_A, P_STATE))
    bt = lambda b: b.transpose(2, 0, 1).reshape(GROUP_A, N_STATE)
    ct = lambda c: c.transpose(0, 2, 1).reshape(N_STATE, GROUP_A)
    rspec = pl.BlockSpec((1, S5_TILE_ST), lambda i: (0, i))
    bspec = pl.BlockSpec((S5_TILE_IN, S5_TILE_ST), lambda i: (0, i))
    bin_spec = pl.BlockSpec((GROUP_A, S5_TILE_ST), lambda i: (0, i))
    cin_spec = pl.BlockSpec((S5_TILE_ST, GROUP_A), lambda i: (i, 0))
    return pl.pallas_call(
        _s5prep_kernel,
        grid=(S5_TILES,),
        in_specs=[rspec, rspec, rspec, bin_spec, bin_spec, cin_spec, cin_spec],
        out_specs=[rspec, rspec, bspec, bspec,
                   pl.BlockSpec((None, 2 * S5_TILE_ST, S5_TILE_IN), lambda i: (i, 0, 0))],
        out_shape=[jax.ShapeDtypeStruct((1, N_STATE), F32), jax.ShapeDtypeStruct((1, N_STATE), F32),
                   jax.ShapeDtypeStruct((S5_TILE_IN, N_STATE), BF16),
                   jax.ShapeDtypeStruct((S5_TILE_IN, N_STATE), BF16),
                   jax.ShapeDtypeStruct((S5_TILES, 2 * S5_TILE_ST, S5_TILE_IN), BF16)],
        compiler_params=_cparams(1),
        name="s5_prep",
    )(row(lam_re), row(lam_im), row(ldt), bt(b_re), bt(b_im), ct(c_re), ct(c_im))


def _inproj_kernel(x_hbm, shift_ref, scale_ref, gain_ref, w_ref, wt_ref, p_ref, pt_ref, h_scr, x_buf, x_sem,
                   *, tm, rc, rows_per_seq, seq_base, n_mtiles):
    mt, n = pl.program_id(0), pl.program_id(1)

    def x_copy(tile):
        return pltpu.make_async_copy(x_hbm.at[pl.ds(pl.multiple_of(tile * tm, tm), tm), :], x_buf, x_sem)

    @pl.when(n == 0)
    def _():
        pl.when(mt == 0)(lambda: x_copy(0).start())
        x_copy(mt).wait()
        gain = gain_ref[...]

        def body(i, carry):
            r0 = pl.multiple_of(i * rc, rc)
            row0 = mt * tm + r0
            xx = x_buf[pl.ds(r0, rc), :]
            y = xx * lax.rsqrt(jnp.mean(xx * xx, axis=-1, keepdims=True) + EPS)
            sc = _seq_rows(scale_ref, row0, rc, rows_per_seq, seq_base)
            sh = _seq_rows(shift_ref, row0, rc, rows_per_seq, seq_base)
            h_scr[pl.ds(r0, rc), :] = (y * (gain * (1.0 + sc)) + sh).astype(BF16)
            return carry

        lax.fori_loop(0, tm // rc, body, 0)
        pl.when(mt + 1 < n_mtiles)(lambda: x_copy(mt + 1).start())
        pt_ref[...] = _dot_nt(h_scr[...], wt_ref[...].astype(BF16))

    p_ref[...] = _dot_nt(h_scr[...], w_ref[...].astype(BF16)).astype(BF16)


def _inproj_call(x2d, mod, gain, w_in_t, *, rows_per_seq, seq_base, tm, tn):
    tokens = x2d.shape[0]
    ns = mod.shape[0]
    rc = 128 if rows_per_seq >= 128 else 8 * rows_per_seq
    kern = functools.partial(_inproj_kernel, tm=tm, rc=rc, rows_per_seq=rows_per_seq, seq_base=seq_base,
                             n_mtiles=tokens // tm)
    assert COL_G % tn == 0
    n_before = COL_G // tn

    def w_rows(m, n):
        tiles = jnp.where(n < n_before, n * (tn // SUBLANES), COL_MA // SUBLANES + (n - n_before) * (tn // SUBLANES))
        return (tiles * SUBLANES, 0)

    return pl.pallas_call(
        kern,
        grid=(tokens // tm, PROJ_COLS // tn),
        in_specs=[pl.BlockSpec(memory_space=pl.ANY),
                  pl.BlockSpec((ns, D_MODEL), lambda m, n: (0, 0), pipeline_mode=pl.Buffered(1)),
                  pl.BlockSpec((ns, D_MODEL), lambda m, n: (0, 1), pipeline_mode=pl.Buffered(1)),
                  _const_spec((1, D_MODEL), 2),
                  pl.BlockSpec((pl.Element(tn), pl.Element(D_MODEL)), w_rows),
                  pl.BlockSpec((GATE_RANK, D_MODEL), lambda m, n: (COL_G // GATE_RANK, 0),
                               pipeline_mode=pl.Buffered(1))],
        out_specs=[pl.BlockSpec((tm, tn), lambda m, n: (m, n)),
                   pl.BlockSpec((tm, GATE_RANK), lambda m, n: (m, 0))],
        out_shape=[jax.ShapeDtypeStruct((tokens, PROJ_COLS), BF16),
                   jax.ShapeDtypeStruct((tokens, GATE_RANK), F32)],
        scratch_shapes=[pltpu.VMEM((tm, D_MODEL), BF16), pltpu.VMEM((tm, D_MODEL), F32),
                        pltpu.SemaphoreType.DMA(())],
        compiler_params=_cparams(2),
        name="in_proj",
    )(x2d, mod, mod, gain, w_in_t, w_in_t)


def _s5_kernel(*refs, tc, n_sub, ls, has_h0):
    n_in = 12 if has_h0 else 10
    if has_h0:
        (u_ref, za_ref, h0re_ref, h0im_ref, wbre_ref, wbim_ref, wc_ref,
         are_ref, aim_ref, dskip_ref, wglu_ref, bglu_ref) = refs[:n_in]
    else:
        (u_ref, za_ref, wbre_ref, wbim_ref, wc_ref,
         are_ref, aim_ref, dskip_ref, wglu_ref, bglu_ref) = refs[:n_in]
    gated_ref, fre_ref, fim_ref = refs[n_in:n_in + 3]
    scr = refs[n_in + 3:n_in + 3 + 2 * n_sub * S5_TILES]
    xr = [scr[2 * c * S5_TILES:(2 * c + 1) * S5_TILES] for c in range(n_sub)]
    xi = [scr[(2 * c + 1) * S5_TILES:(2 * c + 2) * S5_TILES] for c in range(n_sub)]
    hc_re, hc_im = refs[n_in + 3 + 2 * n_sub * S5_TILES:]
    ct = pl.program_id(1)
    n_seq = tc // ls

    us = []
    for c in range(n_sub):
        ub = u_ref[c * tc:(c + 1) * tc, :]
        us.append(ub.astype(F32))
        for i in range(S5_TILES):
            ui = ub[:, i * S5_TILE_IN:(i + 1) * S5_TILE_IN]
            for w_ref, x_scr in ((wbre_ref, xr[c][i]), (wbim_ref, xi[c][i])):
                res = _dot(ui, w_ref[:, i * S5_TILE_ST:(i + 1) * S5_TILE_ST])
                for j in range(SUBLANES):
                    x_scr[pl.ds(j, tc, stride=SUBLANES), :] = res[:, j * LANES:(j + 1) * LANES]

    a_re = [are_ref[i] for i in range(S5_TILES)]
    a_im = [aim_ref[i] for i in range(S5_TILES)]
    if not has_h0:
        h_re = [jnp.where(ct == 0, 0.0, hc_re[i]) for i in range(S5_TILES)]
        h_im = [jnp.where(ct == 0, 0.0, hc_im[i]) for i in range(S5_TILES)]
    for c in range(n_sub):
        for s in range(n_seq):
            if has_h0:
                h_re = [h0re_ref[c * n_seq + s, i] for i in range(S5_TILES)]
                h_im = [h0im_ref[c * n_seq + s, i] for i in range(S5_TILES)]
            for tl in range(ls):
                r0 = (s * ls + tl) * SUBLANES
                for i in range(S5_TILES):
                    nr = a_re[i] * h_re[i] - a_im[i] * h_im[i] + xr[c][i][r0:r0 + SUBLANES, :]
                    ni = a_re[i] * h_im[i] + a_im[i] * h_re[i] + xi[c][i][r0:r0 + SUBLANES, :]
                    xr[c][i][r0:r0 + SUBLANES, :] = nr
                    xi[c][i][r0:r0 + SUBLANES, :] = ni
                    h_re[i], h_im[i] = nr, ni
            if has_h0:
                for i in range(S5_TILES):
                    fre_ref[c * n_seq + s, i] = h_re[i]
                    fim_ref[c * n_seq + s, i] = h_im[i]
    if not has_h0:
        for i in range(S5_TILES):
            hc_re[i] = h_re[i]
            hc_im[i] = h_im[i]
            fre_ref[0, i] = h_re[i]
            fim_ref[0, i] = h_im[i]

    for c in range(n_sub):
        ys = []
        for i in range(S5_TILES):
            parts = [x_scr[pl.ds(j, tc, stride=SUBLANES), :]
                     for x_scr in (xr[c][i], xi[c][i]) for j in range(SUBLANES)]
            ys.append(_dot(jnp.concatenate(parts, axis=1).astype(BF16), wc_ref[i]))
        y = jnp.concatenate(ys, axis=1) + dskip_ref[...] * us[c]
        g = jax.nn.gelu(y)
        a_y = g * jax.nn.sigmoid(_dot(g.astype(BF16), wglu_ref[...]) + bglu_ref[...])
        z_a = za_ref[c * tc:(c + 1) * tc, :].astype(F32)
        gated_ref[c * tc:(c + 1) * tc, :] = (a_y * jax.nn.silu(z_a)).astype(BF16)


def _s5_call(proj, prep, d_skip, w_glu, b_glu, h0, *, n_batch, seq_len, tc, n_sub):
    a_re, a_im, wb_re, wb_im, wc = prep
    tokens = proj.shape[0]
    has_h0 = h0 is not None
    tt = tc * n_sub
    if has_h0:
        ls = seq_len
        grid = (tokens // tt, 1)
        tok = lambda b, c: b
        seq_block = tt // ls
    else:
        ls = tc
        nct = seq_len // tt
        grid = (n_batch, nct)
        tok = lambda b, c: b * nct + c
        seq_block = 1
    const = lambda shape: _const_spec(shape, 2)
    st_spec = pl.BlockSpec((seq_block, S5_TILES, SUBLANES, LANES), lambda b, c: (b, 0, 0, 0))
    in_specs = [pl.BlockSpec((tt, W_A), lambda b, c: (tok(b, c), COL_UA // W_A)),
                pl.BlockSpec((tt, W_A), lambda b, c: (tok(b, c), COL_ZA // W_A))]
    args = [proj] * 2
    if has_h0:
        in_specs += [st_spec, st_spec]
        args += [h0[0], h0[1]]
    in_specs += [const((S5_TILE_IN, N_STATE)), const((S5_TILE_IN, N_STATE)),
                 const((S5_TILES, 2 * S5_TILE_ST, S5_TILE_IN)),
                 const((S5_TILES, SUBLANES, LANES)), const((S5_TILES, SUBLANES, LANES)),
                 const((1, W_A)), const((W_A, W_A)), const((1, W_A))]
    args += [wb_re, wb_im, wc, a_re, a_im, d_skip, w_glu, b_glu]
    n_seq_total = tokens // seq_len
    st_shape = jax.ShapeDtypeStruct((n_seq_total, S5_TILES, SUBLANES, LANES), F32)
    return pl.pallas_call(
        functools.partial(_s5_kernel, tc=tc, n_sub=n_sub, ls=ls, has_h0=has_h0),
        grid=grid,
        in_specs=in_specs,
        out_specs=[pl.BlockSpec((tt, W_A), lambda b, c: (tok(b, c), 0)), st_spec, st_spec],
        out_shape=[jax.ShapeDtypeStruct((tokens, W_A), BF16), st_shape, st_shape],
        scratch_shapes=([pltpu.VMEM((tc * SUBLANES, LANES), F32)] * (2 * n_sub * S5_TILES)
                        + [pltpu.VMEM((S5_TILES, SUBLANES, LANES), F32)] * 2),
        compiler_params=_cparams(2),
        name="s5_branch",
    )(*args)


def _gla_levels(t, ls):
    ms = []
    m = ls
    while m >= 1:
        ms.append(m)
        m //= 2
    idx = np.arange(t)
    mats = []
    for m in ms:
        if m == 1:
            continue
        same = (idx[:, None] // m) == (idx[None, :] // m)
        mats.append(same & (idx[None, :] <= idx[:, None]))
        mats.append(same)
    return ms, np.concatenate(mats, axis=0).astype(np.float32)


def _gla_kernel(*refs, t, n_sub, ls, has_s0, ms):
    if has_s0:
        (q_ref, k_ref, v_ref, zb_ref, ma0_ref, ma1_ref, mb0_ref, mb1_ref, glr_ref, ya_ref,
         s0_ref, lv_ref, wg_ref, bg_ref, gg_ref, waout_ref, wbout_ref, merged_ref, sfin_ref, s_scr) = refs
        assert n_sub == 1
    else:
        (q_ref, k_ref, v_ref, zb_ref, ma0_ref, ma1_ref, mb0_ref, mb1_ref, glr_ref, ya_ref,
         lv_ref, wg_ref, bg_ref, gg_ref, waout_ref, wbout_ref, merged_ref, sfin_ref, s_scr) = refs
        assert ls == t
    ct = pl.program_id(1)
    n_seq = t // ls

    ti = lax.broadcasted_iota(jnp.int32, (t, t), 0)
    si = lax.broadcasted_iota(jnp.int32, (t, t), 1)
    masks = {}
    for li in range(1, len(ms)):
        m = ms[li]
        masks[li] = ((_shr(ti, 2 * m) == _shr(si, 2 * m)) & ((_shr(ti, m) & 1) == 1) & ((_shr(si, m) & 1) == 0))
    rid = lax.broadcasted_iota(jnp.int32, (t, 1), 0)

    if not has_s0:
        states = [jnp.where(ct == 0, 0.0, s_scr[h]) for h in range(N_HEADS_B)]
    ga = jax.nn.log_sigmoid(_dot(glr_ref[...].astype(BF16), wg_ref[...]) + bg_ref[...]) * (LOG2_E / GATE_TAU)
    ga_b = ga.astype(BF16)
    ga_r = ga_b.astype(F32)
    e_all = _dot(lv_ref[...], jnp.concatenate([ga_b[c * t:(c + 1) * t] for c in range(n_sub)], axis=1))
    gated_rows = []
    for c in range(n_sub):
        rows = slice(c * t, (c + 1) * t)
        cols = slice(c * QK_W, (c + 1) * QK_W)
        cum = lambda li: e_all[2 * li * t:(2 * li + 1) * t, cols]
        tot = lambda li: e_all[(2 * li + 1) * t:(2 * li + 2) * t, cols]

        q, k = q_ref[rows, :].astype(F32), k_ref[rows, :].astype(F32)
        heads = []
        for h in range(N_HEADS_B):
            hs = slice(h * DK_B, (h + 1) * DK_B)
            qs = q[:, hs] * (DK_B ** -0.5)
            kh = k[:, hs]
            vb = v_ref[rows, h * DV_B:(h + 1) * DV_B]
            vh = vb.astype(F32)
            b = cum(0)[:, hs]
            bend = tot(0)[:, hs]
            q_dec = (qs * jnp.exp2(b)).astype(BF16)
            k_dec = kh * jnp.exp2(bend - b)
            scores = jnp.zeros((t, t), F32)
            for li in range(1, len(ms)):
                if ms[li] == 1:
                    qm = (qs * jnp.exp2(ga_r[rows, hs])).astype(BF16)
                    km = kh.astype(BF16)
                else:
                    eq = cum(li)[:, hs]
                    qm = (qs * jnp.exp2(eq)).astype(BF16)
                    km = (kh * jnp.exp2(tot(li)[:, hs] - eq)).astype(BF16)
                scores = jnp.where(masks[li], _dot_nt(qm, km), scores)
            o = _dot(scores.astype(BF16), vb) + jnp.sum(qs * kh, axis=-1, keepdims=True) * vh
            o_inter = []
            for i in range(n_seq):
                r0 = i * ls
                s_old = s0_ref[i, h] if has_s0 else states[h]
                o_inter.append(_dot(q_dec[r0:r0 + ls], s_old.astype(BF16)))
                kd = k_dec if n_seq == 1 else jnp.where((rid >= r0) & (rid < r0 + ls), k_dec, 0.0)
                d_end = jnp.exp2(bend[r0:r0 + 1, :])
                d_col = jnp.broadcast_to(d_end, (DK_B, DK_B)).T
                s_new = jnp.concatenate([d_col] * (DV_B // DK_B), axis=1) * s_old + _dot_tn(kd.astype(BF16), vb)
                if has_s0:
                    sfin_ref[i, h] = s_new
                else:
                    states[h] = s_new
            o = o + (o_inter[0] if n_seq == 1 else jnp.concatenate(o_inter, axis=0))
            heads.append(o * lax.rsqrt(jnp.mean(o * o, axis=-1, keepdims=True) + EPS))
        b_y = jnp.concatenate(heads, axis=1) * gg_ref[...]
        gated_rows.append((b_y * jax.nn.silu(zb_ref[rows, :].astype(F32))).astype(BF16))
    if not has_s0:
        for h in range(N_HEADS_B):
            s_scr[h] = states[h]
            sfin_ref[0, h] = states[h]
    gated = gated_rows[0] if n_sub == 1 else jnp.concatenate(gated_rows, axis=0)
    merged = (_merge_gate(ma0_ref, ma1_ref) * _dot(ya_ref[...], waout_ref[...])
              + _merge_gate(mb0_ref, mb1_ref) * _dot(gated, wbout_ref[...]))
    merged_ref[...] = merged.astype(BF16)


def _gla_call(proj, proj_g, y_a, w_gate, b_gate, g_gain, w_a_out, w_b_out, s0, *, n_batch, seq_len, t, n_sub):
    tokens = proj.shape[0]
    has_s0 = s0 is not None
    tt = t * n_sub
    if has_s0:
        ls = seq_len
        grid = (tokens // tt, 1)
        tok = lambda b, c: b
        seq_block = tt // ls
    else:
        ls = t
        nct = seq_len // tt
        grid = (n_batch, nct)
        tok = lambda b, c: b * nct + c
        seq_block = 1
    ms, lv = _gla_levels(t, ls)
    lv = jnp.asarray(lv, dtype=BF16)
    const = lambda shape: _const_spec(shape, 2)
    st_spec = pl.BlockSpec((seq_block, N_HEADS_B, DK_B, DV_B), lambda b, c: (b, 0, 0, 0))
    in_specs = [pl.BlockSpec((tt, QK_W), lambda b, c: (tok(b, c), COL_Q // QK_W)),
                pl.BlockSpec((tt, QK_W), lambda b, c: (tok(b, c), COL_K // QK_W)),
                pl.BlockSpec((tt, W_B), lambda b, c: (tok(b, c), COL_V // W_B)),
                pl.BlockSpec((tt, W_B), lambda b, c: (tok(b, c), COL_ZB // W_B)),
                pl.BlockSpec((tt, W_A), lambda b, c: (tok(b, c), PCOL_MA // W_A)),
                pl.BlockSpec((tt, W_A), lambda b, c: (tok(b, c), PCOL_MA // W_A + 1)),
                pl.BlockSpec((tt, W_B), lambda b, c: (tok(b, c), PCOL_MB // W_B)),
                pl.BlockSpec((tt, W_B), lambda b, c: (tok(b, c), PCOL_MB // W_B + 1)),
                pl.BlockSpec((tt, GATE_RANK), lambda b, c: (tok(b, c), 0)),
                pl.BlockSpec((tt, W_A), lambda b, c: (tok(b, c), 0))]
    args = [proj, proj, proj, proj, proj, proj, proj, proj, proj_g, y_a]
    if has_s0:
        in_specs.append(st_spec)
        args.append(s0)
    in_specs += [const(lv.shape), const((GATE_RANK, QK_W)), const((1, QK_W)), const((1, W_B)),
                 const((W_A, D_MODEL)), const((W_B, D_MODEL))]
    args += [lv, w_gate, b_gate, g_gain, w_a_out, w_b_out]
    n_seq_total = tokens // seq_len
    return pl.pallas_call(
        functools.partial(_gla_kernel, t=t, n_sub=n_sub, ls=ls, has_s0=has_s0, ms=tuple(ms)),
        grid=grid,
        in_specs=in_specs,
        out_specs=[pl.BlockSpec((tt, D_MODEL), lambda b, c: (tok(b, c), 0)), st_spec],
        out_shape=[jax.ShapeDtypeStruct((tokens, D_MODEL), BF16),
                   jax.ShapeDtypeStruct((n_seq_total, N_HEADS_B, DK_B, DV_B), F32)],
        scratch_shapes=[pltpu.VMEM((N_HEADS_B, DK_B, DV_B), F32)],
        compiler_params=_cparams(2),
        name="gla_branch",
    )(*args)


def _out_kernel(m_ref, x_ref, gate_ref, w_ref, fg_ref, y_ref, acc0, acc1,
                *, tm, rc, rows_per_seq, seq_base, n_tiles):
    s = pl.program_id(0)

    def matmul(acc):
        acc[...] = _dot(m_ref[...], w_ref[...])

    def epilogue(acc):
        fg = fg_ref[...]
        for i in range(tm // rc):
            r0 = i * rc
            gate = _seq_rows(gate_ref, (s - 1) * tm + r0, rc, rows_per_seq, seq_base)
            yy = x_ref[r0:r0 + rc, :] + gate * acc[r0:r0 + rc, :]
            y_ref[r0:r0 + rc, :] = yy * lax.rsqrt(jnp.mean(yy * yy, axis=-1, keepdims=True) + EPS) * fg

    accs = (acc0, acc1)

    pl.when(s == 0)(lambda: matmul(acc0))
    for par in (0, 1):
        @pl.when((s > 0) & (s < n_tiles) & (lax.rem(s, 2) == par))
        def _(par=par):
            matmul(accs[par])
            epilogue(accs[1 - par])

    pl.when(s == n_tiles)(lambda: epilogue(accs[(n_tiles - 1) % 2]))


def _out_call(merged, x2d, mod, w_out, fgain, *, rows_per_seq, seq_base, tm):
    tokens = x2d.shape[0]
    ns = mod.shape[0]
    n_tiles = tokens // tm
    rc = 128 if rows_per_seq >= 128 else 8 * rows_per_seq
    kern = functools.partial(_out_kernel, tm=tm, rc=rc, rows_per_seq=rows_per_seq, seq_base=seq_base,
                             n_tiles=n_tiles)
    prev = lambda m: (jnp.maximum(m - 1, 0), 0)
    return pl.pallas_call(
        kern,
        grid=(n_tiles + 1,),
        in_specs=[pl.BlockSpec((tm, D_MODEL), lambda m: (jnp.minimum(m, n_tiles - 1), 0)),
                  pl.BlockSpec((tm, D_MODEL), prev),
                  pl.BlockSpec((ns, D_MODEL), lambda m: (0, 2), pipeline_mode=pl.Buffered(1)),
                  _const_spec((D_MODEL, D_MODEL), 1),
                  _const_spec((1, D_MODEL), 1)],
        out_specs=pl.BlockSpec((tm, D_MODEL), prev),
        out_shape=jax.ShapeDtypeStruct((tokens, D_MODEL), F32),
        scratch_shapes=[pltpu.VMEM((tm, D_MODEL), F32), pltpu.VMEM((tm, D_MODEL), F32)],
        compiler_params=_cparams(1),
        name="out_proj",
    )(merged, x2d, mod, w_out, fgain)


def _tile(n, target):
    t = min(n, target)
    assert n % t == 0
    return t


def _plan(seq_len, tokens, long_seq):
    span = seq_len if long_seq else tokens
    s5_tc = _tile(span, 256)
    gla_t = _tile(span, 128)
    return dict(in_tm=_tile(span, 2048), in_tn=512 if long_seq else 1024,
                s5_tc=s5_tc, s5_sub=2 if span % (2 * s5_tc) == 0 else 1,
                gla_t=gla_t, gla_sub=2 if long_seq and seq_len % (2 * gla_t) == 0 else 1,
                out_tm=_tile(span, 512))


def _layer(x, n_seq_before, mod, weights, prep, s0_ssm, s0_gla):
    (gain, w_in_t, d_skip, w_glu, b_glu, w_a_out, w_gate, b_gate, g_gain, w_b_out, w_out, fgain) = weights
    n_batch, seq_len, _ = x.shape
    tokens = n_batch * seq_len
    x2d = x.reshape(tokens, D_MODEL)
    plan = _plan(seq_len, tokens, long_seq=s0_ssm is None)
    proj, proj_g = _inproj_call(x2d, mod, gain, w_in_t, rows_per_seq=seq_len, seq_base=n_seq_before,
                                tm=plan["in_tm"], tn=plan["in_tn"])
    y_a, f_re, f_im = _s5_call(proj, prep, d_skip, w_glu, b_glu, s0_ssm, n_batch=n_batch, seq_len=seq_len,
                               tc=plan["s5_tc"], n_sub=plan["s5_sub"])
    merged, s_fin = _gla_call(proj, proj_g, y_a, w_gate, b_gate, g_gain, w_a_out, w_b_out, s0_gla,
                              n_batch=n_batch, seq_len=seq_len, t=plan["gla_t"], n_sub=plan["gla_sub"])
    y = _out_call(merged, x2d, mod, w_out, fgain, rows_per_seq=seq_len, seq_base=n_seq_before,
                  tm=plan["out_tm"])
    st = lambda f: f.reshape(1, n_batch, N_GROUPS_A, P_STATE)
    return (y.reshape(n_batch, seq_len, D_MODEL), st(f_re), st(f_im),
            s_fin.reshape(1, n_batch, N_HEADS_B, DK_B, DV_B))


def kernel(x_prompt, x_sample, c_prompt, c_sample, state_ssm_re, state_ssm_im, state_gla, w_ada, b_ada, norm_gain, w_in, lambda_re, lambda_im, log_dt, ssm_b_re, ssm_b_im, ssm_c_re, ssm_c_im, d_skip, w_glu, b_glu, w_gate_up, b_gate, gla_norm_gain, w_a_out, w_b_out, w_out, final_norm_gain):
    assert w_ada.shape[0] == 1, "single-layer step"
    assert w_in.shape[2] == IN_COLS
    n_prompt, n_sample = x_prompt.shape[0], x_sample.shape[0]

    mod = _mod_call(jnp.concatenate([c_prompt, c_sample], axis=0), w_ada[0], b_ada)
    a_re, a_im, wb_re, wb_im, wc = _s5prep_call(lambda_re[0], lambda_im[0], log_dt[0], ssm_b_re[0], ssm_b_im[0],
                                                ssm_c_re[0], ssm_c_im[0])
    state_tiles = lambda a: a.reshape(-1, S5_TILES, SUBLANES, LANES)
    prep = (state_tiles(a_re)[0], state_tiles(a_im)[0], wb_re, wb_im, wc)

    w_in_t = jnp.swapaxes(w_in, 1, 2).reshape(IN_COLS, D_MODEL)
    w_gate = w_gate_up[0].astype(BF16)
    weights = (norm_gain, w_in_t, d_skip, w_glu[0].astype(BF16), b_glu, w_a_out[0].astype(BF16),
               w_gate, b_gate, gla_norm_gain, w_b_out[0].astype(BF16), w_out[0].astype(BF16),
               final_norm_gain.reshape(1, D_MODEL))

    y_p, pre, pim, pgla = _layer(x_prompt, 0, mod, weights, prep, None, None)
    y_s, sre, sim, sgla = _layer(x_sample, n_prompt, mod, weights, prep,
                                 (state_tiles(state_ssm_re[0]), state_tiles(state_ssm_im[0])), state_gla[0])
    return (y_p, y_s, pre, pim, pgla, sre, sim, sgla)
```

```python
import functools

import numpy as np
import jax
import jax.numpy as jnp
from jax import lax
from jax.experimental import pallas as pl
from jax.experimental.pallas import tpu as pltpu

F32 = jnp.float32
BF16 = jnp.bfloat16

D_MODEL = 2048
W_A = D_MODEL // 2
GROUP_A = 16
N_GROUPS_A = W_A // GROUP_A
P_STATE = 64
N_STATE = N_GROUPS_A * P_STATE
W_B = D_MODEL // 2
N_HEADS_B = 4
DK_B = W_B // 2 // N_HEADS_B
DV_B = W_B // N_HEADS_B
QK_W = N_HEADS_B * DK_B
GATE_RANK = 16
GATE_TAU = 16.0
EPS = 1e-6
LOG2_E = 1.4426950408889634

LANES = 128
SUBLANES = 8
V7X_VMEM_BYTES = 64 * 1024 * 1024
VMEM_LIMIT_BYTES = V7X_VMEM_BYTES - 8 * 1024 * 1024

S5_TILE_GROUPS = 16
S5_TILES = N_GROUPS_A // S5_TILE_GROUPS
S5_TILE_IN = S5_TILE_GROUPS * GROUP_A
S5_TILE_ST = S5_TILE_GROUPS * P_STATE
assert S5_TILE_ST == SUBLANES * LANES

COL_UA = 0
COL_ZA = COL_UA + W_A
COL_Q = COL_ZA + W_A
COL_K = COL_Q + QK_W
COL_V = COL_K + QK_W
COL_ZB = COL_V + W_B
COL_G = COL_ZB + W_B
COL_MA = COL_G + GATE_RANK
COL_MB = COL_MA + D_MODEL
IN_COLS = COL_MB + D_MODEL
PROJ_COLS = IN_COLS - GATE_RANK
PCOL_MA = COL_G
PCOL_MB = PCOL_MA + D_MODEL
assert COL_MA % SUBLANES == 0 and COL_G % GATE_RANK == 0


def _dot(a, b):
    return jnp.dot(a, b, preferred_element_type=F32)


def _dot_nt(a, b):
    return lax.dot_general(a, b, (((1,), (1,)), ((), ())), preferred_element_type=F32)


def _dot_tn(a, b):
    return lax.dot_general(a, b, (((0,), (0,)), ((), ())), preferred_element_type=F32)


def _shr(x, pow2):
    sh = int(pow2).bit_length() - 1
    assert 1 << sh == pow2
    return jnp.right_shift(x, sh)


def _cparams(n_axes):
    return pltpu.CompilerParams(dimension_semantics=("arbitrary",) * n_axes,
                                vmem_limit_bytes=VMEM_LIMIT_BYTES)


def _const_spec(shape, n_axes):
    zeros = (0,) * len(shape)
    index_map = (lambda a: zeros) if n_axes == 1 else (lambda a, b: zeros)
    return pl.BlockSpec(shape, index_map, pipeline_mode=pl.Buffered(1))


def _seq_rows(ref, row0, rc, rows_per_seq, seq_base):
    s = seq_base + lax.div(row0, jnp.int32(rows_per_seq))
    if rows_per_seq >= rc:
        return ref[pl.ds(s, 1), :]
    assert rc % rows_per_seq == 0 and rows_per_seq % SUBLANES == 0
    width = ref.shape[1]
    return jnp.concatenate([jnp.broadcast_to(ref[pl.ds(s + j, 1), :], (rows_per_seq, width))
                            for j in range(rc // rows_per_seq)], axis=0)


def _merge_gate(lo_ref, hi_ref):
    return jax.nn.sigmoid(jnp.concatenate([lo_ref[...], hi_ref[...]], axis=1))


def _mod_kernel(c_ref, w_ref, b_ref, o_ref):
    s = jax.nn.silu(c_ref[...]).astype(BF16)
    o_ref[...] = _dot(s, w_ref[...].astype(BF16)) + b_ref[...]


def _mod_call(c_all, w_ada, b_ada):
    ns = c_all.shape[0]
    tn = 1024
    return pl.pallas_call(
        _mod_kernel,
        grid=(3 * D_MODEL // tn,),
        in_specs=[_const_spec((ns, D_MODEL), 1),
                  pl.BlockSpec((D_MODEL, tn), lambda n: (0, n)),
                  pl.BlockSpec((1, tn), lambda n: (0, n))],
        out_specs=pl.BlockSpec((ns, tn), lambda n: (0, n)),
        out_shape=jax.ShapeDtypeStruct((ns, 3 * D_MODEL), F32),
        compiler_params=_cparams(1),
        name="adaln_mod",
    )(c_all, w_ada, b_ada)


def _s5prep_kernel(lr_ref, li_ref, ldt_ref, bre_ref, bim_ref, cre_ref, cim_ref,
                   are_ref, aim_ref, wbre_ref, wbim_ref, wc_ref):
    dt = jnp.exp(ldt_ref[...])
    lr, li = lr_ref[...], li_ref[...]
    mag = jnp.exp(lr * dt)
    ab_re, ab_im = mag * jnp.cos(li * dt), mag * jnp.sin(li * dt)
    nr, ni = ab_re - 1.0, ab_im
    den = lr * lr + li * li
    cf_re = (nr * lr + ni * li) / den
    cf_im = (ni * lr - nr * li) / den
    are_ref[...] = ab_re
    aim_ref[...] = ab_im
    r = lax.broadcasted_iota(jnp.int32, (S5_TILE_IN, S5_TILE_ST), 0)
    c = lax.broadcasted_iota(jnp.int32, (S5_TILE_IN, S5_TILE_ST), 1)
    on_b = _shr(r, GROUP_A) == _shr(c, P_STATE)
    bre, bim = bre_ref[...], bim_ref[...]
    rep = lambda a: jnp.concatenate([a] * S5_TILE_GROUPS, axis=0)
    wbre_ref[...] = jnp.where(on_b, rep(cf_re * bre - cf_im * bim), 0.0).astype(BF16)
    wbim_ref[...] = jnp.where(on_b, rep(cf_re * bim + cf_im * bre), 0.0).astype(BF16)
    h = lax.broadcasted_iota(jnp.int32, (GROUP_A, S5_TILE_IN), 0)
    c = lax.broadcasted_iota(jnp.int32, (GROUP_A, S5_TILE_IN), 1)
    spread = jnp.where((c & (GROUP_A - 1)) == h, 1.0, 0.0).astype(BF16)
    r = lax.broadcasted_iota(jnp.int32, (S5_TILE_ST, S5_TILE_IN), 0)
    c = lax.broadcasted_iota(jnp.int32, (S5_TILE_ST, S5_TILE_IN), 1)
    on_c = _shr(r, P_STATE) == _shr(c, GROUP_A)
    wc_ref[0:S5_TILE_ST, :] = jnp.where(on_c, _dot(cre_ref[...].astype(BF16), spread), 0.0).astype(BF16)
    wc_ref[S5_TILE_ST:2 * S5_TILE_ST, :] = jnp.where(on_c, -_dot(cim_ref[...].astype(BF16), spread), 0.0).astype(BF16)


def _s5prep_call(lam_re, lam_im, log_dt, b_re, b_im, c_re, c_im):
    row = lambda a: a.reshape(1, N_STATE)
    ldt = jnp.broadcast_to(log_dt[:, None], (N_GROUPS_A, P_STATE))
    bt = lambda b: b.transpose(2, 0, 1).reshape(GROUP_A, N_STATE)
    ct = lambda c: c.transpose(0, 2, 1).reshape(N_STATE, GROUP_A)
    rspec = pl.BlockSpec((1, S5_TILE_ST), lambda i: (0, i))
    bspec = pl.BlockSpec((S5_TILE_IN, S5_TILE_ST), lambda i: (0, i))
    bin_spec = pl.BlockSpec((GROUP_A, S5_TILE_ST), lambda i: (0, i))
    cin_spec = pl.BlockSpec((S5_TILE_ST, GROUP_A), lambda i: (i, 0))
    return pl.pallas_call(
        _s5prep_kernel,
        grid=(S5_TILES,),
        in_specs=[rspec, rspec, rspec, bin_spec, bin_spec, cin_spec, cin_spec],
        out_specs=[rspec, rspec, bspec, bspec,
                   pl.BlockSpec((None, 2 * S5_TILE_ST, S5_TILE_IN), lambda i: (i, 0, 0))],
        out_shape=[jax.ShapeDtypeStruct((1, N_STATE), F32), jax.ShapeDtypeStruct((1, N_STATE), F32),
                   jax.ShapeDtypeStruct((S5_TILE_IN, N_STATE), BF16),
                   jax.ShapeDtypeStruct((S5_TILE_IN, N_STATE), BF16),
                   jax.ShapeDtypeStruct((S5_TILES, 2 * S5_TILE_ST, S5_TILE_IN), BF16)],
        compiler_params=_cparams(1),
        name="s5_prep",
    )(row(lam_re), row(lam_im), row(ldt), bt(b_re), bt(b_im), ct(c_re), ct(c_im))


def _inproj_kernel(x_hbm, shift_ref, scale_ref, gain_ref, w_ref, wt_ref, p_ref, pt_ref, h_scr, x_buf, x_sem,
                   *, tm, rc, rows_per_seq, seq_base, n_mtiles):
    mt, n = pl.program_id(0), pl.program_id(1)

    def x_copy(tile):
        return pltpu.make_async_copy(x_hbm.at[pl.ds(pl.multiple_of(tile * tm, tm), tm), :], x_buf, x_sem)

    @pl.when(n == 0)
    def _():
        pl.when(mt == 0)(lambda: x_copy(0).start())
        x_copy(mt).wait()
        gain = gain_ref[...]

        def body(i, carry):
            r0 = pl.multiple_of(i * rc, rc)
            row0 = mt * tm + r0
            xx = x_buf[pl.ds(r0, rc), :]
            y = xx * lax.rsqrt(jnp.mean(xx * xx, axis=-1, keepdims=True) + EPS)
            sc = _seq_rows(scale_ref, row0, rc, rows_per_seq, seq_base)
            sh = _seq_rows(shift_ref, row0, rc, rows_per_seq, seq_base)
            h_scr[pl.ds(r0, rc), :] = (y * (gain * (1.0 + sc)) + sh).astype(BF16)
            return carry

        lax.fori_loop(0, tm // rc, body, 0)
        pl.when(mt + 1 < n_mtiles)(lambda: x_copy(mt + 1).start())
        pt_ref[...] = _dot_nt(h_scr[...], wt_ref[...].astype(BF16))

    p_ref[...] = _dot_nt(h_scr[...], w_ref[...].astype(BF16))


def _inproj_call(x2d, mod, gain, w_in_t, *, rows_per_seq, seq_base, tm, tn):
    tokens = x2d.shape[0]
    ns = mod.shape[0]
    rc = 128 if rows_per_seq >= 128 else 8 * rows_per_seq
    kern = functools.partial(_inproj_kernel, tm=tm, rc=rc, rows_per_seq=rows_per_seq, seq_base=seq_base,
                             n_mtiles=tokens // tm)
    assert COL_G % tn == 0
    n_before = COL_G // tn

    def w_rows(m, n):
        tiles = jnp.where(n < n_before, n * (tn // SUBLANES), COL_MA // SUBLANES + (n - n_before) * (tn // SUBLANES))
        return (tiles * SUBLANES, 0)

    return pl.pallas_call(
        kern,
        grid=(tokens // tm, PROJ_COLS // tn),
        in_specs=[pl.BlockSpec(memory_space=pl.ANY),
                  pl.BlockSpec((ns, D_MODEL), lambda m, n: (0, 0), pipeline_mode=pl.Buffered(1)),
                  pl.BlockSpec((ns, D_MODEL), lambda m, n: (0, 1), pipeline_mode=pl.Buffered(1)),
                  _const_spec((1, D_MODEL), 2),
                  pl.BlockSpec((pl.Element(tn), pl.Element(D_MODEL)), w_rows),
                  pl.BlockSpec((GATE_RANK, D_MODEL), lambda m, n: (COL_G // GATE_RANK, 0),
                               pipeline_mode=pl.Buffered(1))],
        out_specs=[pl.BlockSpec((tm, tn), lambda m, n: (m, n)),
                   pl.BlockSpec((tm, GATE_RANK), lambda m, n: (m, 0))],
        out_shape=[jax.ShapeDtypeStruct((tokens, PROJ_COLS), F32),
                   jax.ShapeDtypeStruct((tokens, GATE_RANK), F32)],
        scratch_shapes=[pltpu.VMEM((tm, D_MODEL), BF16), pltpu.VMEM((tm, D_MODEL), F32),
                        pltpu.SemaphoreType.DMA(())],
        compiler_params=_cparams(2),
        name="in_proj",
    )(x2d, mod, mod, gain, w_in_t, w_in_t)


def _s5_kernel(*refs, tc, n_sub, ls, has_h0):
    n_in = 12 if has_h0 else 10
    if has_h0:
        (u_ref, za_ref, h0re_ref, h0im_ref, wbre_ref, wbim_ref, wc_ref,
         are_ref, aim_ref, dskip_ref, wglu_ref, bglu_ref) = refs[:n_in]
    else:
        (u_ref, za_ref, wbre_ref, wbim_ref, wc_ref,
         are_ref, aim_ref, dskip_ref, wglu_ref, bglu_ref) = refs[:n_in]
    gated_ref, fre_ref, fim_ref = refs[n_in:n_in + 3]
    scr = refs[n_in + 3:n_in + 3 + 2 * n_sub * S5_TILES]
    xr = [scr[2 * c * S5_TILES:(2 * c + 1) * S5_TILES] for c in range(n_sub)]
    xi = [scr[(2 * c + 1) * S5_TILES:(2 * c + 2) * S5_TILES] for c in range(n_sub)]
    hc_re, hc_im = refs[n_in + 3 + 2 * n_sub * S5_TILES:]
    ct = pl.program_id(1)
    n_seq = tc // ls

    us = []
    for c in range(n_sub):
        u = u_ref[c * tc:(c + 1) * tc, :]
        us.append(u)
        ub = u.astype(BF16)
        for i in range(S5_TILES):
            ui = ub[:, i * S5_TILE_IN:(i + 1) * S5_TILE_IN]
            for w_ref, x_scr in ((wbre_ref, xr[c][i]), (wbim_ref, xi[c][i])):
                res = _dot(ui, w_ref[:, i * S5_TILE_ST:(i + 1) * S5_TILE_ST])
                for j in range(SUBLANES):
                    x_scr[pl.ds(j, tc, stride=SUBLANES), :] = res[:, j * LANES:(j + 1) * LANES]

    a_re = [are_ref[i] for i in range(S5_TILES)]
    a_im = [aim_ref[i] for i in range(S5_TILES)]
    if not has_h0:
        h_re = [jnp.where(ct == 0, 0.0, hc_re[i]) for i in range(S5_TILES)]
        h_im = [jnp.where(ct == 0, 0.0, hc_im[i]) for i in range(S5_TILES)]
    for c in range(n_sub):
        for s in range(n_seq):
            if has_h0:
                h_re = [h0re_ref[c * n_seq + s, i] for i in range(S5_TILES)]
                h_im = [h0im_ref[c * n_seq + s, i] for i in range(S5_TILES)]
            for tl in range(ls):
                r0 = (s * ls + tl) * SUBLANES
                for i in range(S5_TILES):
                    nr = a_re[i] * h_re[i] - a_im[i] * h_im[i] + xr[c][i][r0:r0 + SUBLANES, :]
                    ni = a_re[i] * h_im[i] + a_im[i] * h_re[i] + xi[c][i][r0:r0 + SUBLANES, :]
                    xr[c][i][r0:r0 + SUBLANES, :] = nr
                    xi[c][i][r0:r0 + SUBLANES, :] = ni
                    h_re[i], h_im[i] = nr, ni
            if has_h0:
                for i in range(S5_TILES):
                    fre_ref[c * n_seq + s, i] = h_re[i]
                    fim_ref[c * n_seq + s, i] = h_im[i]
    if not has_h0:
        for i in range(S5_TILES):
            hc_re[i] = h_re[i]
            hc_im[i] = h_im[i]
            fre_ref[0, i] = h_re[i]
            fim_ref[0, i] = h_im[i]

    for c in range(n_sub):
        ys = []
        for i in range(S5_TILES):
            parts = [x_scr[pl.ds(j, tc, stride=SUBLANES), :]
                     for x_scr in (xr[c][i], xi[c][i]) for j in range(SUBLANES)]
            ys.append(_dot(jnp.concatenate(parts, axis=1).astype(BF16), wc_ref[i]))
        y = jnp.concatenate(ys, axis=1) + dskip_ref[...] * us[c]
        g = jax.nn.gelu(y)
        a_y = g * jax.nn.sigmoid(_dot(g.astype(BF16), wglu_ref[...]) + bglu_ref[...])
        gated_ref[c * tc:(c + 1) * tc, :] = (a_y * jax.nn.silu(za_ref[c * tc:(c + 1) * tc, :])).astype(BF16)


def _s5_call(proj, prep, d_skip, w_glu, b_glu, h0, *, n_batch, seq_len, tc, n_sub):
    a_re, a_im, wb_re, wb_im, wc = prep
    tokens = proj.shape[0]
    has_h0 = h0 is not None
    tt = tc * n_sub
    if has_h0:
        ls = seq_len
        grid = (tokens // tt, 1)
        tok = lambda b, c: b
        seq_block = tt // ls
    else:
        ls = tc
        nct = seq_len // tt
        grid = (n_batch, nct)
        tok = lambda b, c: b * nct + c
        seq_block = 1
    const = lambda shape: _const_spec(shape, 2)
    st_spec = pl.BlockSpec((seq_block, S5_TILES, SUBLANES, LANES), lambda b, c: (b, 0, 0, 0))
    in_specs = [pl.BlockSpec((tt, W_A), lambda b, c: (tok(b, c), COL_UA // W_A)),
                pl.BlockSpec((tt, W_A), lambda b, c: (tok(b, c), COL_ZA // W_A))]
    args = [proj] * 2
    if has_h0:
        in_specs += [st_spec, st_spec]
        args += [h0[0], h0[1]]
    in_specs += [const((S5_TILE_IN, N_STATE)), const((S5_TILE_IN, N_STATE)),
                 const((S5_TILES, 2 * S5_TILE_ST, S5_TILE_IN)),
                 const((S5_TILES, SUBLANES, LANES)), const((S5_TILES, SUBLANES, LANES)),
                 const((1, W_A)), const((W_A, W_A)), const((1, W_A))]
    args += [wb_re, wb_im, wc, a_re, a_im, d_skip, w_glu, b_glu]
    n_seq_total = tokens // seq_len
    st_shape = jax.ShapeDtypeStruct((n_seq_total, S5_TILES, SUBLANES, LANES), F32)
    return pl.pallas_call(
        functools.partial(_s5_kernel, tc=tc, n_sub=n_sub, ls=ls, has_h0=has_h0),
        grid=grid,
        in_specs=in_specs,
        out_specs=[pl.BlockSpec((tt, W_A), lambda b, c: (tok(b, c), 0)), st_spec, st_spec],
        out_shape=[jax.ShapeDtypeStruct((tokens, W_A), BF16), st_shape, st_shape],
        scratch_shapes=([pltpu.VMEM((tc * SUBLANES, LANES), F32)] * (2 * n_sub * S5_TILES)
                        + [pltpu.VMEM((S5_TILES, SUBLANES, LANES), F32)] * 2),
        compiler_params=_cparams(2),
        name="s5_branch",
    )(*args)


def _gla_levels(t, ls):
    ms = []
    m = ls
    while m >= 1:
        ms.append(m)
        m //= 2
    idx = np.arange(t)
    same = lambda m: (idx[:, None] // m) == (idx[None, :] // m)
    mats = [same(SUBLANES) & (idx[None, :] <= idx[:, None])]
    for m in ms:
        if 1 < m < SUBLANES:
            mats.append(same(m) & (idx[None, :] <= idx[:, None]))
            mats.append(same(m))
    return ms, np.concatenate(mats, axis=0).astype(np.float32)


def _gla_kernel(*refs, t, n_sub, ls, has_s0, ms):
    if has_s0:
        (q_ref, k_ref, v_ref, zb_ref, ma0_ref, ma1_ref, mb0_ref, mb1_ref, glr_ref, ya_ref,
         s0_ref, lv_ref, wg_ref, bg_ref, gg_ref, waout_ref, wbout_ref, merged_ref, sfin_ref, s_scr) = refs
        assert n_sub == 1
    else:
        (q_ref, k_ref, v_ref, zb_ref, ma0_ref, ma1_ref, mb0_ref, mb1_ref, glr_ref, ya_ref,
         lv_ref, wg_ref, bg_ref, gg_ref, waout_ref, wbout_ref, merged_ref, sfin_ref, s_scr) = refs
        assert ls == t
    ct = pl.program_id(1)
    n_seq = t // ls

    ti = lax.broadcasted_iota(jnp.int32, (t, t), 0)
    si = lax.broadcasted_iota(jnp.int32, (t, t), 1)
    masks = {}
    for li in range(1, len(ms)):
        m = ms[li]
        masks[li] = ((_shr(ti, 2 * m) == _shr(si, 2 * m)) & ((_shr(ti, m) & 1) == 1) & ((_shr(si, m) & 1) == 0))
    rid = lax.broadcasted_iota(jnp.int32, (t, 1), 0)

    if not has_s0:
        states = [jnp.where(ct == 0, 0.0, s_scr[h]) for h in range(N_HEADS_B)]
    ga = jax.nn.log_sigmoid(_dot(glr_ref[...].astype(BF16), wg_ref[...]) + bg_ref[...]) * (LOG2_E / GATE_TAU)
    ga_b = ga.astype(BF16)
    ga_r = ga_b.astype(F32)
    e_all = _dot(lv_ref[...], jnp.concatenate([ga_b[c * t:(c + 1) * t] for c in range(n_sub)], axis=1))
    gated_rows = []
    for c in range(n_sub):
        rows = slice(c * t, (c + 1) * t)
        cols = slice(c * QK_W, (c + 1) * QK_W)
        cum_tile = e_all[0:t, cols]
        n_tiles = t // SUBLANES
        tile_tot = cum_tile.reshape(n_tiles, SUBLANES, QK_W)[:, SUBLANES - 1, :]
        tile_row = [tile_tot[j:j + 1, :] for j in range(n_tiles)]
        cums, tots = {}, {}
        n_fine = 0
        for li, m in enumerate(ms):
            if m >= SUBLANES:
                per_blk = m // SUBLANES
                before, total = [], []
                for blk in range(t // m):
                    acc = jnp.zeros((1, QK_W), F32)
                    for j in range(blk * per_blk, (blk + 1) * per_blk):
                        before.append(jnp.broadcast_to(acc, (SUBLANES, QK_W)))
                        acc = acc + tile_row[j]
                    total.append(jnp.broadcast_to(acc, (m, QK_W)))
                cums[li] = cum_tile + jnp.concatenate(before, axis=0)
                tots[li] = jnp.concatenate(total, axis=0)
            elif m > 1:
                cums[li] = e_all[(1 + 2 * n_fine) * t:(2 + 2 * n_fine) * t, cols]
                tots[li] = e_all[(2 + 2 * n_fine) * t:(3 + 2 * n_fine) * t, cols]
                n_fine += 1
        cum = lambda li: cums[li]
        tot = lambda li: tots[li]

        q, k, v = q_ref[rows, :], k_ref[rows, :], v_ref[rows, :]
        heads = []
        for h in range(N_HEADS_B):
            hs = slice(h * DK_B, (h + 1) * DK_B)
            qs = q[:, hs] * (DK_B ** -0.5)
            kh = k[:, hs]
            vh = v[:, h * DV_B:(h + 1) * DV_B]
            vb = vh.astype(BF16)
            b = cum(0)[:, hs]
            bend = tot(0)[:, hs]
            q_dec = (qs * jnp.exp2(b)).astype(BF16)
            k_dec = kh * jnp.exp2(bend - b)
            scores = jnp.zeros((t, t), F32)
            for li in range(1, len(ms)):
                if ms[li] == 1:
                    qm = (qs * jnp.exp2(ga_r[rows, hs])).astype(BF16)
                    km = kh.astype(BF16)
                else:
                    eq = cum(li)[:, hs]
                    qm = (qs * jnp.exp2(eq)).astype(BF16)
                    km = (kh * jnp.exp2(tot(li)[:, hs] - eq)).astype(BF16)
                scores = jnp.where(masks[li], _dot_nt(qm, km), scores)
            o = _dot(scores.astype(BF16), vb) + jnp.sum(qs * kh, axis=-1, keepdims=True) * vh
            o_inter = []
            for i in range(n_seq):
                r0 = i * ls
                s_old = s0_ref[i, h] if has_s0 else states[h]
                o_inter.append(_dot(q_dec[r0:r0 + ls], s_old.astype(BF16)))
                kd = k_dec if n_seq == 1 else jnp.where((rid >= r0) & (rid < r0 + ls), k_dec, 0.0)
                d_end = jnp.exp2(bend[r0:r0 + 1, :])
                d_col = jnp.broadcast_to(d_end, (DK_B, DK_B)).T
                s_new = jnp.concatenate([d_col] * (DV_B // DK_B), axis=1) * s_old + _dot_tn(kd.astype(BF16), vb)
                if has_s0:
                    sfin_ref[i, h] = s_new
                else:
                    states[h] = s_new
            o = o + (o_inter[0] if n_seq == 1 else jnp.concatenate(o_inter, axis=0))
            heads.append(o * lax.rsqrt(jnp.mean(o * o, axis=-1, keepdims=True) + EPS))
        b_y = jnp.concatenate(heads, axis=1) * gg_ref[...]
        gated_rows.append((b_y * jax.nn.silu(zb_ref[rows, :])).astype(BF16))
    if not has_s0:
        for h in range(N_HEADS_B):
            s_scr[h] = states[h]
            sfin_ref[0, h] = states[h]
    gated = gated_rows[0] if n_sub == 1 else jnp.concatenate(gated_rows, axis=0)
    merged = (_merge_gate(ma0_ref, ma1_ref) * _dot(ya_ref[...], waout_ref[...])
              + _merge_gate(mb0_ref, mb1_ref) * _dot(gated, wbout_ref[...]))
    merged_ref[...] = merged.astype(BF16)


def _gla_call(proj, proj_g, y_a, w_gate, b_gate, g_gain, w_a_out, w_b_out, s0, *, n_batch, seq_len, t, n_sub):
    tokens = proj.shape[0]
    has_s0 = s0 is not None
    tt = t * n_sub
    if has_s0:
        ls = seq_len
        grid = (tokens // tt, 1)
        tok = lambda b, c: b
        seq_block = tt // ls
    else:
        ls = t
        nct = seq_len // tt
        grid = (n_batch, nct)
        tok = lambda b, c: b * nct + c
        seq_block = 1
    ms, lv = _gla_levels(t, ls)
    lv = jnp.asarray(lv, dtype=BF16)
    const = lambda shape: _const_spec(shape, 2)
    st_spec = pl.BlockSpec((seq_block, N_HEADS_B, DK_B, DV_B), lambda b, c: (b, 0, 0, 0))
    in_specs = [pl.BlockSpec((tt, QK_W), lambda b, c: (tok(b, c), COL_Q // QK_W)),
                pl.BlockSpec((tt, QK_W), lambda b, c: (tok(b, c), COL_K // QK_W)),
                pl.BlockSpec((tt, W_B), lambda b, c: (tok(b, c), COL_V // W_B)),
                pl.BlockSpec((tt, W_B), lambda b, c: (tok(b, c), COL_ZB // W_B)),
                pl.BlockSpec((tt, W_A), lambda b, c: (tok(b, c), PCOL_MA // W_A)),
                pl.BlockSpec((tt, W_A), lambda b, c: (tok(b, c), PCOL_MA // W_A + 1)),
                pl.BlockSpec((tt, W_B), lambda b, c: (tok(b, c), PCOL_MB // W_B)),
                pl.BlockSpec((tt, W_B), lambda b, c: (tok(b, c), PCOL_MB // W_B + 1)),
                pl.BlockSpec((tt, GATE_RANK), lambda b, c: (tok(b, c), 0)),
                pl.BlockSpec((tt, W_A), lambda b, c: (tok(b, c), 0))]
    args = [proj, proj, proj, proj, proj, proj, proj, proj, proj_g, y_a]
    if has_s0:
        in_specs.append(st_spec)
        args.append(s0)
    in_specs += [const(lv.shape), const((GATE_RANK, QK_W)), const((1, QK_W)), const((1, W_B)),
                 const((W_A, D_MODEL)), const((W_B, D_MODEL))]
    args += [lv, w_gate, b_gate, g_gain, w_a_out, w_b_out]
    n_seq_total = tokens // seq_len
    return pl.pallas_call(
        functools.partial(_gla_kernel, t=t, n_sub=n_sub, ls=ls, has_s0=has_s0, ms=tuple(ms)),
        grid=grid,
        in_specs=in_specs,
        out_specs=[pl.BlockSpec((tt, D_MODEL), lambda b, c: (tok(b, c), 0)), st_spec],
        out_shape=[jax.ShapeDtypeStruct((tokens, D_MODEL), BF16),
                   jax.ShapeDtypeStruct((n_seq_total, N_HEADS_B, DK_B, DV_B), F32)],
        scratch_shapes=[pltpu.VMEM((N_HEADS_B, DK_B, DV_B), F32)],
        compiler_params=_cparams(2),
        name="gla_branch",
    )(*args)


def _out_kernel(m_ref, x_ref, gate_ref, w_ref, fg_ref, y_ref, acc0, acc1,
                *, tm, rc, rows_per_seq, seq_base, n_tiles):
    s = pl.program_id(0)

    def matmul(acc):
        acc[...] = _dot(m_ref[...], w_ref[...])

    def epilogue(acc):
        fg = fg_ref[...]
        for i in range(tm // rc):
            r0 = i * rc
            gate = _seq_rows(gate_ref, (s - 1) * tm + r0, rc, rows_per_seq, seq_base)
            yy = x_ref[r0:r0 + rc, :] + gate * acc[r0:r0 + rc, :]
            y_ref[r0:r0 + rc, :] = yy * lax.rsqrt(jnp.mean(yy * yy, axis=-1, keepdims=True) + EPS) * fg

    accs = (acc0, acc1)

    pl.when(s == 0)(lambda: matmul(acc0))
    for par in (0, 1):
        @pl.when((s > 0) & (s < n_tiles) & (lax.rem(s, 2) == par))
        def _(par=par):
            matmul(accs[par])
            epilogue(accs[1 - par])

    pl.when(s == n_tiles)(lambda: epilogue(accs[(n_tiles - 1) % 2]))


def _out_call(merged, x2d, mod, w_out, fgain, *, rows_per_seq, seq_base, tm):
    tokens = x2d.shape[0]
    ns = mod.shape[0]
    n_tiles = tokens // tm
    rc = 128 if rows_per_seq >= 128 else 8 * rows_per_seq
    kern = functools.partial(_out_kernel, tm=tm, rc=rc, rows_per_seq=rows_per_seq, seq_base=seq_base,
                             n_tiles=n_tiles)
    prev = lambda m: (jnp.maximum(m - 1, 0), 0)
    return pl.pallas_call(
        kern,
        grid=(n_tiles + 1,),
        in_specs=[pl.BlockSpec((tm, D_MODEL), lambda m: (jnp.minimum(m, n_tiles - 1), 0)),
                  pl.BlockSpec((tm, D_MODEL), prev),
                  pl.BlockSpec((ns, D_MODEL), lambda m: (0, 2), pipeline_mode=pl.Buffered(1)),
                  _const_spec((D_MODEL, D_MODEL), 1),
                  _const_spec((1, D_MODEL), 1)],
        out_specs=pl.BlockSpec((tm, D_MODEL), prev),
        out_shape=jax.ShapeDtypeStruct((tokens, D_MODEL), F32),
        scratch_shapes=[pltpu.VMEM((tm, D_MODEL), F32), pltpu.VMEM((tm, D_MODEL), F32)],
        compiler_params=_cparams(1),
        name="out_proj",
    )(merged, x2d, mod, w_out, fgain)


def _tile(n, target):
    t = min(n, target)
    assert n % t == 0
    return t


def _plan(seq_len, tokens, long_seq):
    span = seq_len if long_seq else tokens
    s5_tc = _tile(span, 256)
    gla_t = _tile(span, 128)
    return dict(in_tm=_tile(span, 2048), in_tn=512 if long_seq else 1024,
                s5_tc=s5_tc, s5_sub=2 if span % (2 * s5_tc) == 0 else 1,
                gla_t=gla_t, gla_sub=2 if long_seq and seq_len % (2 * gla_t) == 0 else 1,
                out_tm=_tile(span, 512))


def _layer(x, n_seq_before, mod, weights, prep, s0_ssm, s0_gla):
    (gain, w_in_t, d_skip, w_glu, b_glu, w_a_out, w_gate, b_gate, g_gain, w_b_out, w_out, fgain) = weights
    n_batch, seq_len, _ = x.shape
    tokens = n_batch * seq_len
    x2d = x.reshape(tokens, D_MODEL)
    plan = _plan(seq_len, tokens, long_seq=s0_ssm is None)
    proj, proj_g = _inproj_call(x2d, mod, gain, w_in_t, rows_per_seq=seq_len, seq_base=n_seq_before,
                                tm=plan["in_tm"], tn=plan["in_tn"])
    y_a, f_re, f_im = _s5_call(proj, prep, d_skip, w_glu, b_glu, s0_ssm, n_batch=n_batch, seq_len=seq_len,
                               tc=plan["s5_tc"], n_sub=plan["s5_sub"])
    merged, s_fin = _gla_call(proj, proj_g, y_a, w_gate, b_gate, g_gain, w_a_out, w_b_out, s0_gla,
                              n_batch=n_batch, seq_len=seq_len, t=plan["gla_t"], n_sub=plan["gla_sub"])
    y = _out_call(merged, x2d, mod, w_out, fgain, rows_per_seq=seq_len, seq_base=n_seq_before,
                  tm=plan["out_tm"])
    st = lambda f: f.reshape(1, n_batch, N_GROUPS_A, P_STATE)
    return (y.reshape(n_batch, seq_len, D_MODEL), st(f_re), st(f_im),
            s_fin.reshape(1, n_batch, N_HEADS_B, DK_B, DV_B))


def kernel(x_prompt, x_sample, c_prompt, c_sample, state_ssm_re, state_ssm_im, state_gla, w_ada, b_ada, norm_gain, w_in, lambda_re, lambda_im, log_dt, ssm_b_re, ssm_b_im, ssm_c_re, ssm_c_im, d_skip, w_glu, b_glu, w_gate_up, b_gate, gla_norm_gain, w_a_out, w_b_out, w_out, final_norm_gain):
    assert w_ada.shape[0] == 1, "single-layer step"
    assert w_in.shape[2] == IN_COLS
    n_prompt, n_sample = x_prompt.shape[0], x_sample.shape[0]

    mod = _mod_call(jnp.concatenate([c_prompt, c_sample], axis=0), w_ada[0], b_ada)
    a_re, a_im, wb_re, wb_im, wc = _s5prep_call(lambda_re[0], lambda_im[0], log_dt[0], ssm_b_re[0], ssm_b_im[0],
                                                ssm_c_re[0], ssm_c_im[0])
    state_tiles = lambda a: a.reshape(-1, S5_TILES, SUBLANES, LANES)
    prep = (state_tiles(a_re)[0], state_tiles(a_im)[0], wb_re, wb_im, wc)

    w_in_t = jnp.swapaxes(w_in, 1, 2).reshape(IN_COLS, D_MODEL)
    w_gate = w_gate_up[0].astype(BF16)
    weights = (norm_gain, w_in_t, d_skip, w_glu[0].astype(BF16), b_glu, w_a_out[0].astype(BF16),
               w_gate, b_gate, gla_norm_gain, w_b_out[0].astype(BF16), w_out[0].astype(BF16),
               final_norm_gain.reshape(1, D_MODEL))

    y_p, pre, pim, pgla = _layer(x_prompt, 0, mod, weights, prep, None, None)
    y_s, sre, sim, sgla = _layer(x_sample, n_prompt, mod, weights, prep,
                                 (state_tiles(state_ssm_re[0]), state_tiles(state_ssm_im[0])), state_gla[0])
    return (y_p, y_s, pre, pim, pgla, sre, sim, sgla)
```

```python
import functools

import numpy as np
import jax
import jax.numpy as jnp
from jax import lax
from jax.experimental import pallas as pl
from jax.experimental.pallas import tpu as pltpu

F32 = jnp.float32
BF16 = jnp.bfloat16

D_MODEL = 2048
W_A = D_MODEL // 2
GROUP_A = 16
N_GROUPS_A = W_A // GROUP_A
P_STATE = 64
N_STATE = N_GROUPS_A * P_STATE
W_B = D_MODEL // 2
N_HEADS_B = 4
DK_B = W_B // 2 // N_HEADS_B
DV_B = W_B // N_HEADS_B
QK_W = N_HEADS_B * DK_B
GATE_RANK = 16
GATE_TAU = 16.0
EPS = 1e-6
LOG2_E = 1.4426950408889634

LANES = 128
SUBLANES = 8
V7X_VMEM_BYTES = 64 * 1024 * 1024
VMEM_LIMIT_BYTES = V7X_VMEM_BYTES - 8 * 1024 * 1024

S5_TILE_GROUPS = 16
S5_TILES = N_GROUPS_A // S5_TILE_GROUPS
S5_TILE_IN = S5_TILE_GROUPS * GROUP_A
S5_TILE_ST = S5_TILE_GROUPS * P_STATE
assert S5_TILE_ST == SUBLANES * LANES

COL_UA = 0
COL_ZA = COL_UA + W_A
COL_Q = COL_ZA + W_A
COL_K = COL_Q + QK_W
COL_V = COL_K + QK_W
COL_ZB = COL_V + W_B
COL_G = COL_ZB + W_B
COL_MA = COL_G + GATE_RANK
COL_MB = COL_MA + D_MODEL
IN_COLS = COL_MB + D_MODEL
PROJ_COLS = IN_COLS - GATE_RANK
PCOL_MA = COL_G
PCOL_MB = PCOL_MA + D_MODEL
assert COL_MA % SUBLANES == 0 and COL_G % GATE_RANK == 0


def _dot(a, b):
    return jnp.dot(a, b, preferred_element_type=F32)


def _dot_nt(a, b):
    return lax.dot_general(a, b, (((1,), (1,)), ((), ())), preferred_element_type=F32)


def _dot_tn(a, b):
    return lax.dot_general(a, b, (((0,), (0,)), ((), ())), preferred_element_type=F32)


def _shr(x, pow2):
    sh = int(pow2).bit_length() - 1
    assert 1 << sh == pow2
    return jnp.right_shift(x, sh)


def _cparams(n_axes):
    return pltpu.CompilerParams(dimension_semantics=("arbitrary",) * n_axes,
                                vmem_limit_bytes=VMEM_LIMIT_BYTES)


def _const_spec(shape, n_axes):
    zeros = (0,) * len(shape)
    index_map = (lambda a: zeros) if n_axes == 1 else (lambda a, b: zeros)
    return pl.BlockSpec(shape, index_map, pipeline_mode=pl.Buffered(1))


def _seq_rows(ref, row0, rc, rows_per_seq, seq_base):
    s = seq_base + lax.div(row0, jnp.int32(rows_per_seq))
    if rows_per_seq >= rc:
        return ref[pl.ds(s, 1), :]
    assert rc % rows_per_seq == 0 and rows_per_seq % SUBLANES == 0
    width = ref.shape[1]
    return jnp.concatenate([jnp.broadcast_to(ref[pl.ds(s + j, 1), :], (rows_per_seq, width))
                            for j in range(rc // rows_per_seq)], axis=0)


def _merge_gate(lo_ref, hi_ref):
    return jax.nn.sigmoid(jnp.concatenate([lo_ref[...], hi_ref[...]], axis=1))


def _mod_kernel(c_ref, w_ref, b_ref, o_ref):
    s = jax.nn.silu(c_ref[...]).astype(BF16)
    o_ref[...] = _dot(s, w_ref[...].astype(BF16)) + b_ref[...]


def _mod_call(c_all, w_ada, b_ada):
    ns = c_all.shape[0]
    tn = 1024
    return pl.pallas_call(
        _mod_kernel,
        grid=(3 * D_MODEL // tn,),
        in_specs=[_const_spec((ns, D_MODEL), 1),
                  pl.BlockSpec((D_MODEL, tn), lambda n: (0, n)),
                  pl.BlockSpec((1, tn), lambda n: (0, n))],
        out_specs=pl.BlockSpec((ns, tn), lambda n: (0, n)),
        out_shape=jax.ShapeDtypeStruct((ns, 3 * D_MODEL), F32),
        compiler_params=_cparams(1),
        name="adaln_mod",
    )(c_all, w_ada, b_ada)


def _s5prep_kernel(lr_ref, li_ref, ldt_ref, bre_ref, bim_ref, cre_ref, cim_ref,
                   are_ref, aim_ref, wbre_ref, wbim_ref, wc_ref):
    dt = jnp.exp(ldt_ref[...])
    lr, li = lr_ref[...], li_ref[...]
    mag = jnp.exp(lr * dt)
    ab_re, ab_im = mag * jnp.cos(li * dt), mag * jnp.sin(li * dt)
    nr, ni = ab_re - 1.0, ab_im
    den = lr * lr + li * li
    cf_re = (nr * lr + ni * li) / den
    cf_im = (ni * lr - nr * li) / den
    are_ref[...] = ab_re
    aim_ref[...] = ab_im
    r = lax.broadcasted_iota(jnp.int32, (S5_TILE_IN, S5_TILE_ST), 0)
    c = lax.broadcasted_iota(jnp.int32, (S5_TILE_IN, S5_TILE_ST), 1)
    on_b = _shr(r, GROUP_A) == _shr(c, P_STATE)
    bre, bim = bre_ref[...], bim_ref[...]
    rep = lambda a: jnp.concatenate([a] * S5_TILE_GROUPS, axis=0)
    wbre_ref[...] = jnp.where(on_b, rep(cf_re * bre - cf_im * bim), 0.0).astype(BF16)
    wbim_ref[...] = jnp.where(on_b, rep(cf_re * bim + cf_im * bre), 0.0).astype(BF16)
    h = lax.broadcasted_iota(jnp.int32, (GROUP_A, S5_TILE_IN), 0)
    c = lax.broadcasted_iota(jnp.int32, (GROUP_A, S5_TILE_IN), 1)
    spread = jnp.where((c & (GROUP_A - 1)) == h, 1.0, 0.0).astype(BF16)
    r = lax.broadcasted_iota(jnp.int32, (S5_TILE_ST, S5_TILE_IN), 0)
    c = lax.broadcasted_iota(jnp.int32, (S5_TILE_ST, S5_TILE_IN), 1)
    on_c = _shr(r, P_STATE) == _shr(c, GROUP_A)
    wc_ref[0:S5_TILE_ST, :] = jnp.where(on_c, _dot(cre_ref[...].astype(BF16), spread), 0.0).astype(BF16)
    wc_ref[S5_TILE_ST:2 * S5_TILE_ST, :] = jnp.where(on_c, -_dot(cim_ref[...].astype(BF16), spread), 0.0).astype(BF16)


def _s5prep_call(lam_re, lam_im, log_dt, b_re, b_im, c_re, c_im):
    row = lambda a: a.reshape(1, N_STATE)
    ldt = jnp.broadcast_to(log_dt[:, None], (N_GROUPS_A, P_STATE))
    bt = lambda b: b.transpose(2, 0, 1).reshape(GROUP_A, N_STATE)
    ct = lambda c: c.transpose(0, 2, 1).reshape(N_STATE, GROUP_A)
    rspec = pl.BlockSpec((1, S5_TILE_ST), lambda i: (0, i))
    bspec = pl.BlockSpec((S5_TILE_IN, S5_TILE_ST), lambda i: (0, i))
    bin_spec = pl.BlockSpec((GROUP_A, S5_TILE_ST), lambda i: (0, i))
    cin_spec = pl.BlockSpec((S5_TILE_ST, GROUP_A), lambda i: (i, 0))
    return pl.pallas_call(
        _s5prep_kernel,
        grid=(S5_TILES,),
        in_specs=[rspec, rspec, rspec, bin_spec, bin_spec, cin_spec, cin_spec],
        out_specs=[rspec, rspec, bspec, bspec,
                   pl.BlockSpec((None, 2 * S5_TILE_ST, S5_TILE_IN), lambda i: (i, 0, 0))],
        out_shape=[jax.ShapeDtypeStruct((1, N_STATE), F32), jax.ShapeDtypeStruct((1, N_STATE), F32),
                   jax.ShapeDtypeStruct((S5_TILE_IN, N_STATE), BF16),
                   jax.ShapeDtypeStruct((S5_TILE_IN, N_STATE), BF16),
                   jax.ShapeDtypeStruct((S5_TILES, 2 * S5_TILE_ST, S5_TILE_IN), BF16)],
        compiler_params=_cparams(1),
        name="s5_prep",
    )(row(lam_re), row(lam_im), row(ldt), bt(b_re), bt(b_im), ct(c_re), ct(c_im))


def _inproj_kernel(x_hbm, shift_ref, scale_ref, gain_ref, w_ref, wt_ref, p_ref, pt_ref, h_scr, x_buf, x_sem,
                   *, tm, rc, rows_per_seq, seq_base, n_mtiles):
    mt, n = pl.program_id(0), pl.program_id(1)

    def x_copy(tile):
        return pltpu.make_async_copy(x_hbm.at[pl.ds(pl.multiple_of(tile * tm, tm), tm), :], x_buf, x_sem)

    @pl.when(n == 0)
    def _():
        pl.when(mt == 0)(lambda: x_copy(0).start())
        x_copy(mt).wait()
        gain = gain_ref[...]

        def body(i, carry):
            r0 = pl.multiple_of(i * rc, rc)
            row0 = mt * tm + r0
            xx = x_buf[pl.ds(r0, rc), :]
            y = xx * lax.rsqrt(jnp.mean(xx * xx, axis=-1, keepdims=True) + EPS)
            sc = _seq_rows(scale_ref, row0, rc, rows_per_seq, seq_base)
            sh = _seq_rows(shift_ref, row0, rc, rows_per_seq, seq_base)
            h_scr[pl.ds(r0, rc), :] = (y * (gain * (1.0 + sc)) + sh).astype(BF16)
            return carry

        lax.fori_loop(0, tm // rc, body, 0)
        pl.when(mt + 1 < n_mtiles)(lambda: x_copy(mt + 1).start())
        pt_ref[...] = _dot_nt(h_scr[...], wt_ref[...].astype(BF16))

    p_ref[...] = _dot_nt(h_scr[...], w_ref[...].astype(BF16))


def _inproj_call(x2d, mod, gain, w_in_t, *, rows_per_seq, seq_base, tm, tn):
    tokens = x2d.shape[0]
    ns = mod.shape[0]
    rc = 128 if rows_per_seq >= 128 else 8 * rows_per_seq
    kern = functools.partial(_inproj_kernel, tm=tm, rc=rc, rows_per_seq=rows_per_seq, seq_base=seq_base,
                             n_mtiles=tokens // tm)
    assert COL_G % tn == 0
    n_before = COL_G // tn

    def w_rows(m, n):
        tiles = jnp.where(n < n_before, n * (tn // SUBLANES), COL_MA // SUBLANES + (n - n_before) * (tn // SUBLANES))
        return (tiles * SUBLANES, 0)

    return pl.pallas_call(
        kern,
        grid=(tokens // tm, PROJ_COLS // tn),
        in_specs=[pl.BlockSpec(memory_space=pl.ANY),
                  pl.BlockSpec((ns, D_MODEL), lambda m, n: (0, 0), pipeline_mode=pl.Buffered(1)),
                  pl.BlockSpec((ns, D_MODEL), lambda m, n: (0, 1), pipeline_mode=pl.Buffered(1)),
                  _const_spec((1, D_MODEL), 2),
                  pl.BlockSpec((pl.Element(tn), pl.Element(D_MODEL)), w_rows),
                  pl.BlockSpec((GATE_RANK, D_MODEL), lambda m, n: (COL_G // GATE_RANK, 0),
                               pipeline_mode=pl.Buffered(1))],
        out_specs=[pl.BlockSpec((tm, tn), lambda m, n: (m, n)),
                   pl.BlockSpec((tm, GATE_RANK), lambda m, n: (m, 0))],
        out_shape=[jax.ShapeDtypeStruct((tokens, PROJ_COLS), F32),
                   jax.ShapeDtypeStruct((tokens, GATE_RANK), F32)],
        scratch_shapes=[pltpu.VMEM((tm, D_MODEL), BF16), pltpu.VMEM((tm, D_MODEL), F32),
                        pltpu.SemaphoreType.DMA(())],
        compiler_params=_cparams(2),
        name="in_proj",
    )(x2d, mod, mod, gain, w_in_t, w_in_t)


def _s5_kernel(*refs, tc, n_sub, ls, has_h0):
    n_in = 12 if has_h0 else 10
    if has_h0:
        (u_ref, za_ref, h0re_ref, h0im_ref, wbre_ref, wbim_ref, wc_ref,
         are_ref, aim_ref, dskip_ref, wglu_ref, bglu_ref) = refs[:n_in]
    else:
        (u_ref, za_ref, wbre_ref, wbim_ref, wc_ref,
         are_ref, aim_ref, dskip_ref, wglu_ref, bglu_ref) = refs[:n_in]
    gated_ref, fre_ref, fim_ref = refs[n_in:n_in + 3]
    scr = refs[n_in + 3:n_in + 3 + 2 * n_sub * S5_TILES]
    xr = [scr[2 * c * S5_TILES:(2 * c + 1) * S5_TILES] for c in range(n_sub)]
    xi = [scr[(2 * c + 1) * S5_TILES:(2 * c + 2) * S5_TILES] for c in range(n_sub)]
    hc_re, hc_im = refs[n_in + 3 + 2 * n_sub * S5_TILES:]
    ct = pl.program_id(1)
    n_seq = tc // ls

    us = []
    for c in range(n_sub):
        u = u_ref[c * tc:(c + 1) * tc, :]
        us.append(u)
        ub = u.astype(BF16)
        for i in range(S5_TILES):
            ui = ub[:, i * S5_TILE_IN:(i + 1) * S5_TILE_IN]
            for w_ref, x_scr in ((wbre_ref, xr[c][i]), (wbim_ref, xi[c][i])):
                res = _dot(ui, w_ref[:, i * S5_TILE_ST:(i + 1) * S5_TILE_ST])
                for j in range(SUBLANES):
                    x_scr[pl.ds(j, tc, stride=SUBLANES), :] = res[:, j * LANES:(j + 1) * LANES]

    a_re = [are_ref[i] for i in range(S5_TILES)]
    a_im = [aim_ref[i] for i in range(S5_TILES)]
    if not has_h0:
        h_re = [jnp.where(ct == 0, 0.0, hc_re[i]) for i in range(S5_TILES)]
        h_im = [jnp.where(ct == 0, 0.0, hc_im[i]) for i in range(S5_TILES)]
    for c in range(n_sub):
        for s in range(n_seq):
            if has_h0:
                h_re = [h0re_ref[c * n_seq + s, i] for i in range(S5_TILES)]
                h_im = [h0im_ref[c * n_seq + s, i] for i in range(S5_TILES)]
            for tl in range(ls):
                r0 = (s * ls + tl) * SUBLANES
                for i in range(S5_TILES):
                    nr = a_re[i] * h_re[i] - a_im[i] * h_im[i] + xr[c][i][r0:r0 + SUBLANES, :]
                    ni = a_re[i] * h_im[i] + a_im[i] * h_re[i] + xi[c][i][r0:r0 + SUBLANES, :]
                    xr[c][i][r0:r0 + SUBLANES, :] = nr
                    xi[c][i][r0:r0 + SUBLANES, :] = ni
                    h_re[i], h_im[i] = nr, ni
            if has_h0:
                for i in range(S5_TILES):
                    fre_ref[c * n_seq + s, i] = h_re[i]
                    fim_ref[c * n_seq + s, i] = h_im[i]
    if not has_h0:
        for i in range(S5_TILES):
            hc_re[i] = h_re[i]
            hc_im[i] = h_im[i]
            fre_ref[0, i] = h_re[i]
            fim_ref[0, i] = h_im[i]

    for c in range(n_sub):
        ys = []
        for i in range(S5_TILES):
            parts = [x_scr[pl.ds(j, tc, stride=SUBLANES), :]
                     for x_scr in (xr[c][i], xi[c][i]) for j in range(SUBLANES)]
            ys.append(_dot(jnp.concatenate(parts, axis=1).astype(BF16), wc_ref[i]))
        y = jnp.concatenate(ys, axis=1) + dskip_ref[...] * us[c]
        g = jax.nn.gelu(y)
        a_y = g * jax.nn.sigmoid(_dot(g.astype(BF16), wglu_ref[...]) + bglu_ref[...])
        gated_ref[c * tc:(c + 1) * tc, :] = (a_y * jax.nn.silu(za_ref[c * tc:(c + 1) * tc, :])).astype(BF16)


def _s5_call(proj, prep, d_skip, w_glu, b_glu, h0, *, n_batch, seq_len, tc, n_sub):
    a_re, a_im, wb_re, wb_im, wc = prep
    tokens = proj.shape[0]
    has_h0 = h0 is not None
    tt = tc * n_sub
    if has_h0:
        ls = seq_len
        grid = (tokens // tt, 1)
        tok = lambda b, c: b
        seq_block = tt // ls
    else:
        ls = tc
        nct = seq_len // tt
        grid = (n_batch, nct)
        tok = lambda b, c: b * nct + c
        seq_block = 1
    const = lambda shape: _const_spec(shape, 2)
    st_spec = pl.BlockSpec((seq_block, S5_TILES, SUBLANES, LANES), lambda b, c: (b, 0, 0, 0))
    in_specs = [pl.BlockSpec((tt, W_A), lambda b, c: (tok(b, c), COL_UA // W_A)),
                pl.BlockSpec((tt, W_A), lambda b, c: (tok(b, c), COL_ZA // W_A))]
    args = [proj] * 2
    if has_h0:
        in_specs += [st_spec, st_spec]
        args += [h0[0], h0[1]]
    in_specs += [const((S5_TILE_IN, N_STATE)), const((S5_TILE_IN, N_STATE)),
                 const((S5_TILES, 2 * S5_TILE_ST, S5_TILE_IN)),
                 const((S5_TILES, SUBLANES, LANES)), const((S5_TILES, SUBLANES, LANES)),
                 const((1, W_A)), const((W_A, W_A)), const((1, W_A))]
    args += [wb_re, wb_im, wc, a_re, a_im, d_skip, w_glu, b_glu]
    n_seq_total = tokens // seq_len
    st_shape = jax.ShapeDtypeStruct((n_seq_total, S5_TILES, SUBLANES, LANES), F32)
    return pl.pallas_call(
        functools.partial(_s5_kernel, tc=tc, n_sub=n_sub, ls=ls, has_h0=has_h0),
        grid=grid,
        in_specs=in_specs,
        out_specs=[pl.BlockSpec((tt, W_A), lambda b, c: (tok(b, c), 0)), st_spec, st_spec],
        out_shape=[jax.ShapeDtypeStruct((tokens, W_A), BF16), st_shape, st_shape],
        scratch_shapes=([pltpu.VMEM((tc * SUBLANES, LANES), F32)] * (2 * n_sub * S5_TILES)
                        + [pltpu.VMEM((S5_TILES, SUBLANES, LANES), F32)] * 2),
        compiler_params=_cparams(2),
        name="s5_branch",
    )(*args)


def _gla_levels(t, ls):
    ms = []
    m = ls
    while m >= 1:
        ms.append(m)
        m //= 2
    idx = np.arange(t)
    same = lambda m: (idx[:, None] // m) == (idx[None, :] // m)
    mats = [same(SUBLANES) & (idx[None, :] <= idx[:, None])]
    for m in ms:
        if 1 < m < SUBLANES:
            mats.append(same(m) & (idx[None, :] <= idx[:, None]))
            mats.append(same(m))
    return ms, np.concatenate(mats, axis=0).astype(np.float32)


def _gla_kernel(*refs, t, n_sub, ls, has_s0, ms):
    if has_s0:
        (q_ref, k_ref, v_ref, zb_ref, ma0_ref, ma1_ref, mb0_ref, mb1_ref, glr_ref, ya_ref,
         s0_ref, lv_ref, wg_ref, bg_ref, gg_ref, waout_ref, wbout_ref, merged_ref, sfin_ref, s_scr) = refs
        assert n_sub == 1
    else:
        (q_ref, k_ref, v_ref, zb_ref, ma0_ref, ma1_ref, mb0_ref, mb1_ref, glr_ref, glrn_ref, ya_ref,
         lv_ref, wg_ref, bg_ref, gg_ref, waout_ref, wbout_ref, merged_ref, sfin_ref, s_scr,
         e0_scr, e1_scr, g0_scr, g1_scr) = refs
        assert ls == t
    ct = pl.program_id(1)
    n_seq = t // ls

    ti = lax.broadcasted_iota(jnp.int32, (t, t), 0)
    si = lax.broadcasted_iota(jnp.int32, (t, t), 1)
    masks = {}
    for li in range(1, len(ms)):
        m = ms[li]
        masks[li] = ((_shr(ti, 2 * m) == _shr(si, 2 * m)) & ((_shr(ti, m) & 1) == 1) & ((_shr(si, m) & 1) == 0))
    rid = lax.broadcasted_iota(jnp.int32, (t, 1), 0)

    def gate_sums(g_ref):
        ga = jax.nn.log_sigmoid(_dot(g_ref[...].astype(BF16), wg_ref[...]) + bg_ref[...]) * (LOG2_E / GATE_TAU)
        ga_b = ga.astype(BF16)
        sums = _dot(lv_ref[...], jnp.concatenate([ga_b[c * t:(c + 1) * t] for c in range(n_sub)], axis=1))
        return ga_b.astype(F32), sums

    def chunks(ga_r, e_all):
        if not has_s0:
            states = [jnp.where(ct == 0, 0.0, s_scr[h]) for h in range(N_HEADS_B)]
        gated_rows = []
        for c in range(n_sub):
            rows = slice(c * t, (c + 1) * t)
            cols = slice(c * QK_W, (c + 1) * QK_W)
            cum_tile = e_all[0:t, cols]
            n_tiles = t // SUBLANES
            tile_tot = cum_tile.reshape(n_tiles, SUBLANES, QK_W)[:, SUBLANES - 1, :]
            tile_row = [tile_tot[j:j + 1, :] for j in range(n_tiles)]
            cums, tots = {}, {}
            n_fine = 0
            for li, m in enumerate(ms):
                if m >= SUBLANES:
                    per_blk = m // SUBLANES
                    before, total = [], []
                    for blk in range(t // m):
                        acc = jnp.zeros((1, QK_W), F32)
                        for j in range(blk * per_blk, (blk + 1) * per_blk):
                            before.append(jnp.broadcast_to(acc, (SUBLANES, QK_W)))
                            acc = acc + tile_row[j]
                        total.append(jnp.broadcast_to(acc, (m, QK_W)))
                    cums[li] = cum_tile + jnp.concatenate(before, axis=0)
                    tots[li] = jnp.concatenate(total, axis=0)
                elif m > 1:
                    cums[li] = e_all[(1 + 2 * n_fine) * t:(2 + 2 * n_fine) * t, cols]
                    tots[li] = e_all[(2 + 2 * n_fine) * t:(3 + 2 * n_fine) * t, cols]
                    n_fine += 1
            cum = lambda li: cums[li]
            tot = lambda li: tots[li]

            q, k, v = q_ref[rows, :], k_ref[rows, :], v_ref[rows, :]
            heads = []
            for h in range(N_HEADS_B):
                hs = slice(h * DK_B, (h + 1) * DK_B)
                qs = q[:, hs] * (DK_B ** -0.5)
                kh = k[:, hs]
                vh = v[:, h * DV_B:(h + 1) * DV_B]
                vb = vh.astype(BF16)
                b = cum(0)[:, hs]
                bend = tot(0)[:, hs]
                q_dec = (qs * jnp.exp2(b)).astype(BF16)
                k_dec = kh * jnp.exp2(bend - b)
                scores = jnp.zeros((t, t), F32)
                for li in range(1, len(ms)):
                    if ms[li] == 1:
                        qm = (qs * jnp.exp2(ga_r[rows, hs])).astype(BF16)
                        km = kh.astype(BF16)
                    else:
                        eq = cum(li)[:, hs]
                        qm = (qs * jnp.exp2(eq)).astype(BF16)
                        km = (kh * jnp.exp2(tot(li)[:, hs] - eq)).astype(BF16)
                    scores = jnp.where(masks[li], _dot_nt(qm, km), scores)
                o = _dot(scores.astype(BF16), vb) + jnp.sum(qs * kh, axis=-1, keepdims=True) * vh
                o_inter = []
                for i in range(n_seq):
                    r0 = i * ls
                    s_old = s0_ref[i, h] if has_s0 else states[h]
                    o_inter.append(_dot(q_dec[r0:r0 + ls], s_old.astype(BF16)))
                    kd = k_dec if n_seq == 1 else jnp.where((rid >= r0) & (rid < r0 + ls), k_dec, 0.0)
                    d_end = jnp.exp2(bend[r0:r0 + 1, :])
                    d_col = jnp.broadcast_to(d_end, (DK_B, DK_B)).T
                    s_new = jnp.concatenate([d_col] * (DV_B // DK_B), axis=1) * s_old + _dot_tn(kd.astype(BF16), vb)
                    if has_s0:
                        sfin_ref[i, h] = s_new
                    else:
                        states[h] = s_new
                o = o + (o_inter[0] if n_seq == 1 else jnp.concatenate(o_inter, axis=0))
                heads.append(o * lax.rsqrt(jnp.mean(o * o, axis=-1, keepdims=True) + EPS))
            b_y = jnp.concatenate(heads, axis=1) * gg_ref[...]
            gated_rows.append((b_y * jax.nn.silu(zb_ref[rows, :])).astype(BF16))
        if not has_s0:
            for h in range(N_HEADS_B):
                s_scr[h] = states[h]
                sfin_ref[0, h] = states[h]
        gated = gated_rows[0] if n_sub == 1 else jnp.concatenate(gated_rows, axis=0)
        merged = (_merge_gate(ma0_ref, ma1_ref) * _dot(ya_ref[...], waout_ref[...])
                  + _merge_gate(mb0_ref, mb1_ref) * _dot(gated, wbout_ref[...]))
        merged_ref[...] = merged.astype(BF16)

    if has_s0:
        chunks(*gate_sums(glr_ref))
    else:
        step = pl.program_id(0) * pl.num_programs(1) + ct
        e_scrs, g_scrs = (e0_scr, e1_scr), (g0_scr, g1_scr)

        @pl.when(step == 0)
        def _():
            g_scrs[0][...], e_scrs[0][...] = gate_sums(glr_ref)

        for par in (0, 1):
            @pl.when(lax.rem(step, 2) == par)
            def _(par=par):
                g_scrs[1 - par][...], e_scrs[1 - par][...] = gate_sums(glrn_ref)
                chunks(g_scrs[par][...], e_scrs[par][...])


def _gla_call(proj, proj_g, y_a, w_gate, b_gate, g_gain, w_a_out, w_b_out, s0, *, n_batch, seq_len, t, n_sub):
    tokens = proj.shape[0]
    has_s0 = s0 is not None
    tt = t * n_sub
    if has_s0:
        ls = seq_len
        grid = (tokens // tt, 1)
        tok = lambda b, c: b
        seq_block = tt // ls
    else:
        ls = t
        nct = seq_len // tt
        grid = (n_batch, nct)
        tok = lambda b, c: b * nct + c
        seq_block = 1
    ms, lv = _gla_levels(t, ls)
    lv = jnp.asarray(lv, dtype=BF16)
    const = lambda shape: _const_spec(shape, 2)
    st_spec = pl.BlockSpec((seq_block, N_HEADS_B, DK_B, DV_B), lambda b, c: (b, 0, 0, 0))
    in_specs = [pl.BlockSpec((tt, QK_W), lambda b, c: (tok(b, c), COL_Q // QK_W)),
                pl.BlockSpec((tt, QK_W), lambda b, c: (tok(b, c), COL_K // QK_W)),
                pl.BlockSpec((tt, W_B), lambda b, c: (tok(b, c), COL_V // W_B)),
                pl.BlockSpec((tt, W_B), lambda b, c: (tok(b, c), COL_ZB // W_B)),
                pl.BlockSpec((tt, W_A), lambda b, c: (tok(b, c), PCOL_MA // W_A)),
                pl.BlockSpec((tt, W_A), lambda b, c: (tok(b, c), PCOL_MA // W_A + 1)),
                pl.BlockSpec((tt, W_B), lambda b, c: (tok(b, c), PCOL_MB // W_B)),
                pl.BlockSpec((tt, W_B), lambda b, c: (tok(b, c), PCOL_MB // W_B + 1)),
                pl.BlockSpec((tt, GATE_RANK), lambda b, c: (tok(b, c), 0))]
    args = [proj, proj, proj, proj, proj, proj, proj, proj, proj_g]
    scratch = [pltpu.VMEM((N_HEADS_B, DK_B, DV_B), F32)]
    if not has_s0:
        last = tokens // tt - 1
        in_specs.append(pl.BlockSpec((tt, GATE_RANK), lambda b, c: (jnp.minimum(tok(b, c) + 1, last), 0)))
        args.append(proj_g)
        scratch += [pltpu.VMEM((lv.shape[0], n_sub * QK_W), F32)] * 2 + [pltpu.VMEM((tt, QK_W), F32)] * 2
    in_specs.append(pl.BlockSpec((tt, W_A), lambda b, c: (tok(b, c), 0)))
    args.append(y_a)
    if has_s0:
        in_specs.append(st_spec)
        args.append(s0)
    in_specs += [const(lv.shape), const((GATE_RANK, QK_W)), const((1, QK_W)), const((1, W_B)),
                 const((W_A, D_MODEL)), const((W_B, D_MODEL))]
    args += [lv, w_gate, b_gate, g_gain, w_a_out, w_b_out]
    n_seq_total = tokens // seq_len
    return pl.pallas_call(
        functools.partial(_gla_kernel, t=t, n_sub=n_sub, ls=ls, has_s0=has_s0, ms=tuple(ms)),
        grid=grid,
        in_specs=in_specs,
        out_specs=[pl.BlockSpec((tt, D_MODEL), lambda b, c: (tok(b, c), 0)), st_spec],
        out_shape=[jax.ShapeDtypeStruct((tokens, D_MODEL), BF16),
                   jax.ShapeDtypeStruct((n_seq_total, N_HEADS_B, DK_B, DV_B), F32)],
        scratch_shapes=scratch,
        compiler_params=_cparams(2),
        name="gla_branch",
    )(*args)


def _out_kernel(m_ref, x_ref, gate_ref, w_ref, fg_ref, y_ref, acc0, acc1,
                *, tm, rc, rows_per_seq, seq_base, n_tiles):
    s = pl.program_id(0)

    def matmul(acc):
        acc[...] = _dot(m_ref[...], w_ref[...])

    def epilogue(acc):
        fg = fg_ref[...]
        for i in range(tm // rc):
            r0 = i * rc
            gate = _seq_rows(gate_ref, (s - 1) * tm + r0, rc, rows_per_seq, seq_base)
            yy = x_ref[r0:r0 + rc, :] + gate * acc[r0:r0 + rc, :]
            y_ref[r0:r0 + rc, :] = yy * lax.rsqrt(jnp.mean(yy * yy, axis=-1, keepdims=True) + EPS) * fg

    accs = (acc0, acc1)

    pl.when(s == 0)(lambda: matmul(acc0))
    for par in (0, 1):
        @pl.when((s > 0) & (s < n_tiles) & (lax.rem(s, 2) == par))
        def _(par=par):
            matmul(accs[par])
            epilogue(accs[1 - par])

    pl.when(s == n_tiles)(lambda: epilogue(accs[(n_tiles - 1) % 2]))


def _out_call(merged, x2d, mod, w_out, fgain, *, rows_per_seq, seq_base, tm):
    tokens = x2d.shape[0]
    ns = mod.shape[0]
    n_tiles = tokens // tm
    rc = 128 if rows_per_seq >= 128 else 8 * rows_per_seq
    kern = functools.partial(_out_kernel, tm=tm, rc=rc, rows_per_seq=rows_per_seq, seq_base=seq_base,
                             n_tiles=n_tiles)
    prev = lambda m: (jnp.maximum(m - 1, 0), 0)
    return pl.pallas_call(
        kern,
        grid=(n_tiles + 1,),
        in_specs=[pl.BlockSpec((tm, D_MODEL), lambda m: (jnp.minimum(m, n_tiles - 1), 0)),
                  pl.BlockSpec((tm, D_MODEL), prev),
                  pl.BlockSpec((ns, D_MODEL), lambda m: (0, 2), pipeline_mode=pl.Buffered(1)),
                  _const_spec((D_MODEL, D_MODEL), 1),
                  _const_spec((1, D_MODEL), 1)],
        out_specs=pl.BlockSpec((tm, D_MODEL), prev),
        out_shape=jax.ShapeDtypeStruct((tokens, D_MODEL), F32),
        scratch_shapes=[pltpu.VMEM((tm, D_MODEL), F32), pltpu.VMEM((tm, D_MODEL), F32)],
        compiler_params=_cparams(1),
        name="out_proj",
    )(merged, x2d, mod, w_out, fgain)


def _tile(n, target):
    t = min(n, target)
    assert n % t == 0
    return t


def _plan(seq_len, tokens, long_seq):
    span = seq_len if long_seq else tokens
    s5_tc = _tile(span, 256)
    gla_t = _tile(span, 128)
    return dict(in_tm=_tile(span, 2048), in_tn=512 if long_seq else 1024,
                s5_tc=s5_tc, s5_sub=2 if span % (2 * s5_tc) == 0 else 1,
                gla_t=gla_t, gla_sub=2 if long_seq and seq_len % (2 * gla_t) == 0 else 1,
                out_tm=_tile(span, 512))


def _layer(x, n_seq_before, mod, weights, prep, s0_ssm, s0_gla):
    (gain, w_in_t, d_skip, w_glu, b_glu, w_a_out, w_gate, b_gate, g_gain, w_b_out, w_out, fgain) = weights
    n_batch, seq_len, _ = x.shape
    tokens = n_batch * seq_len
    x2d = x.reshape(tokens, D_MODEL)
    plan = _plan(seq_len, tokens, long_seq=s0_ssm is None)
    proj, proj_g = _inproj_call(x2d, mod, gain, w_in_t, rows_per_seq=seq_len, seq_base=n_seq_before,
                                tm=plan["in_tm"], tn=plan["in_tn"])
    y_a, f_re, f_im = _s5_call(proj, prep, d_skip, w_glu, b_glu, s0_ssm, n_batch=n_batch, seq_len=seq_len,
                               tc=plan["s5_tc"], n_sub=plan["s5_sub"])
    merged, s_fin = _gla_call(proj, proj_g, y_a, w_gate, b_gate, g_gain, w_a_out, w_b_out, s0_gla,
                              n_batch=n_batch, seq_len=seq_len, t=plan["gla_t"], n_sub=plan["gla_sub"])
    y = _out_call(merged, x2d, mod, w_out, fgain, rows_per_seq=seq_len, seq_base=n_seq_before,
                  tm=plan["out_tm"])
    st = lambda f: f.reshape(1, n_batch, N_GROUPS_A, P_STATE)
    return (y.reshape(n_batch, seq_len, D_MODEL), st(f_re), st(f_im),
            s_fin.reshape(1, n_batch, N_HEADS_B, DK_B, DV_B))


def kernel(x_prompt, x_sample, c_prompt, c_sample, state_ssm_re, state_ssm_im, state_gla, w_ada, b_ada, norm_gain, w_in, lambda_re, lambda_im, log_dt, ssm_b_re, ssm_b_im, ssm_c_re, ssm_c_im, d_skip, w_glu, b_glu, w_gate_up, b_gate, gla_norm_gain, w_a_out, w_b_out, w_out, final_norm_gain):
    assert w_ada.shape[0] == 1, "single-layer step"
    assert w_in.shape[2] == IN_COLS
    n_prompt, n_sample = x_prompt.shape[0], x_sample.shape[0]

    mod = _mod_call(jnp.concatenate([c_prompt, c_sample], axis=0), w_ada[0], b_ada)
    a_re, a_im, wb_re, wb_im, wc = _s5prep_call(lambda_re[0], lambda_im[0], log_dt[0], ssm_b_re[0], ssm_b_im[0],
                                                ssm_c_re[0], ssm_c_im[0])
    state_tiles = lambda a: a.reshape(-1, S5_TILES, SUBLANES, LANES)
    prep = (state_tiles(a_re)[0], state_tiles(a_im)[0], wb_re, wb_im, wc)

    w_in_t = jnp.swapaxes(w_in, 1, 2).reshape(IN_COLS, D_MODEL)
    w_gate = w_gate_up[0].astype(BF16)
    weights = (norm_gain, w_in_t, d_skip, w_glu[0].astype(BF16), b_glu, w_a_out[0].astype(BF16),
               w_gate, b_gate, gla_norm_gain, w_b_out[0].astype(BF16), w_out[0].astype(BF16),
               final_norm_gain.reshape(1, D_MODEL))

    y_p, pre, pim, pgla = _layer(x_prompt, 0, mod, weights, prep, None, None)
    y_s, sre, sim, sgla = _layer(x_sample, n_prompt, mod, weights, prep,
                                 (state_tiles(state_ssm_re[0]), state_tiles(state_ssm_im[0])), state_gla[0])
    return (y_p, y_s, pre, pim, pgla, sre, sim, sgla)
```

```python
import functools

import numpy as np
import jax
import jax.numpy as jnp
from jax import lax
from jax.experimental import pallas as pl
from jax.experimental.pallas import tpu as pltpu

F32 = jnp.float32
BF16 = jnp.bfloat16

D_MODEL = 2048
W_A = D_MODEL // 2
GROUP_A = 16
N_GROUPS_A = W_A // GROUP_A
P_STATE = 64
N_STATE = N_GROUPS_A * P_STATE
W_B = D_MODEL // 2
N_HEADS_B = 4
DK_B = W_B // 2 // N_HEADS_B
DV_B = W_B // N_HEADS_B
QK_W = N_HEADS_B * DK_B
GATE_RANK = 16
GATE_TAU = 16.0
EPS = 1e-6
LOG2_E = 1.4426950408889634

LANES = 128
SUBLANES = 8
V7X_VMEM_BYTES = 64 * 1024 * 1024
VMEM_LIMIT_BYTES = V7X_VMEM_BYTES - 8 * 1024 * 1024

S5_TILE_GROUPS = 16
S5_TILES = N_GROUPS_A // S5_TILE_GROUPS
S5_TILE_IN = S5_TILE_GROUPS * GROUP_A
S5_TILE_ST = S5_TILE_GROUPS * P_STATE
assert S5_TILE_ST == SUBLANES * LANES

COL_UA = 0
COL_ZA = COL_UA + W_A
COL_Q = COL_ZA + W_A
COL_K = COL_Q + QK_W
COL_V = COL_K + QK_W
COL_ZB = COL_V + W_B
COL_G = COL_ZB + W_B
COL_MA = COL_G + GATE_RANK
COL_MB = COL_MA + D_MODEL
IN_COLS = COL_MB + D_MODEL
PROJ_COLS = IN_COLS - GATE_RANK
PCOL_MA = COL_G
PCOL_MB = PCOL_MA + D_MODEL
assert COL_MA % SUBLANES == 0 and COL_G % GATE_RANK == 0


def _dot(a, b):
    return jnp.dot(a, b, preferred_element_type=F32)


def _dot_nt(a, b):
    return lax.dot_general(a, b, (((1,), (1,)), ((), ())), preferred_element_type=F32)


def _dot_tn(a, b):
    return lax.dot_general(a, b, (((0,), (0,)), ((), ())), preferred_element_type=F32)


def _shr(x, pow2):
    sh = int(pow2).bit_length() - 1
    assert 1 << sh == pow2
    return jnp.right_shift(x, sh)


def _cparams(n_axes, vmem_limit_bytes=VMEM_LIMIT_BYTES):
    return pltpu.CompilerParams(dimension_semantics=("arbitrary",) * n_axes,
                                vmem_limit_bytes=vmem_limit_bytes)


def _const_spec(shape, n_axes):
    zeros = (0,) * len(shape)
    index_map = (lambda a: zeros) if n_axes == 1 else (lambda a, b: zeros)
    return pl.BlockSpec(shape, index_map, pipeline_mode=pl.Buffered(1))


def _seq_rows(ref, row0, rc, rows_per_seq, seq_base):
    s = seq_base + lax.div(row0, jnp.int32(rows_per_seq))
    if rows_per_seq >= rc:
        return ref[pl.ds(s, 1), :]
    assert rc % rows_per_seq == 0 and rows_per_seq % SUBLANES == 0
    width = ref.shape[1]
    return jnp.concatenate([jnp.broadcast_to(ref[pl.ds(s + j, 1), :], (rows_per_seq, width))
                            for j in range(rc // rows_per_seq)], axis=0)


def _merge_gate(lo_ref, hi_ref):
    return jax.nn.sigmoid(jnp.concatenate([lo_ref[...], hi_ref[...]], axis=1))


def _mod_kernel(c_ref, w_ref, b_ref, o_ref):
    s = jax.nn.silu(c_ref[...]).astype(BF16)
    o_ref[...] = _dot(s, w_ref[...].astype(BF16)) + b_ref[...]


def _mod_call(c_all, w_ada, b_ada):
    ns = c_all.shape[0]
    tn = 1024
    return pl.pallas_call(
        _mod_kernel,
        grid=(3 * D_MODEL // tn,),
        in_specs=[_const_spec((ns, D_MODEL), 1),
                  pl.BlockSpec((D_MODEL, tn), lambda n: (0, n)),
                  pl.BlockSpec((1, tn), lambda n: (0, n))],
        out_specs=pl.BlockSpec((ns, tn), lambda n: (0, n)),
        out_shape=jax.ShapeDtypeStruct((ns, 3 * D_MODEL), F32),
        compiler_params=_cparams(1),
        name="adaln_mod",
    )(c_all, w_ada, b_ada)


def _s5prep_kernel(lr_ref, li_ref, ldt_ref, bre_ref, bim_ref, cre_ref, cim_ref,
                   are_ref, aim_ref, wbre_ref, wbim_ref, wc_ref):
    dt = jnp.exp(ldt_ref[...])
    lr, li = lr_ref[...], li_ref[...]
    mag = jnp.exp(lr * dt)
    ab_re, ab_im = mag * jnp.cos(li * dt), mag * jnp.sin(li * dt)
    nr, ni = ab_re - 1.0, ab_im
    den = lr * lr + li * li
    cf_re = (nr * lr + ni * li) / den
    cf_im = (ni * lr - nr * li) / den
    are_ref[...] = ab_re
    aim_ref[...] = ab_im
    r = lax.broadcasted_iota(jnp.int32, (S5_TILE_IN, S5_TILE_ST), 0)
    c = lax.broadcasted_iota(jnp.int32, (S5_TILE_IN, S5_TILE_ST), 1)
    on_b = _shr(r, GROUP_A) == _shr(c, P_STATE)
    bre, bim = bre_ref[...], bim_ref[...]
    rep = lambda a: jnp.concatenate([a] * S5_TILE_GROUPS, axis=0)
    wbre_ref[...] = jnp.where(on_b, rep(cf_re * bre - cf_im * bim), 0.0).astype(BF16)
    wbim_ref[...] = jnp.where(on_b, rep(cf_re * bim + cf_im * bre), 0.0).astype(BF16)
    h = lax.broadcasted_iota(jnp.int32, (GROUP_A, S5_TILE_IN), 0)
    c = lax.broadcasted_iota(jnp.int32, (GROUP_A, S5_TILE_IN), 1)
    spread = jnp.where((c & (GROUP_A - 1)) == h, 1.0, 0.0).astype(BF16)
    r = lax.broadcasted_iota(jnp.int32, (S5_TILE_ST, S5_TILE_IN), 0)
    c = lax.broadcasted_iota(jnp.int32, (S5_TILE_ST, S5_TILE_IN), 1)
    on_c = _shr(r, P_STATE) == _shr(c, GROUP_A)
    wc_ref[0:S5_TILE_ST, :] = jnp.where(on_c, _dot(cre_ref[...].astype(BF16), spread), 0.0).astype(BF16)
    wc_ref[S5_TILE_ST:2 * S5_TILE_ST, :] = jnp.where(on_c, -_dot(cim_ref[...].astype(BF16), spread), 0.0).astype(BF16)


def _s5prep_call(lam_re, lam_im, log_dt, b_re, b_im, c_re, c_im):
    row = lambda a: a.reshape(1, N_STATE)
    ldt = jnp.broadcast_to(log_dt[:, None], (N_GROUPS_A, P_STATE))
    bt = lambda b: b.transpose(2, 0, 1).reshape(GROUP_A, N_STATE)
    ct = lambda c: c.transpose(0, 2, 1).reshape(N_STATE, GROUP_A)
    rspec = pl.BlockSpec((1, S5_TILE_ST), lambda i: (0, i))
    bspec = pl.BlockSpec((S5_TILE_IN, S5_TILE_ST), lambda i: (0, i))
    bin_spec = pl.BlockSpec((GROUP_A, S5_TILE_ST), lambda i: (0, i))
    cin_spec = pl.BlockSpec((S5_TILE_ST, GROUP_A), lambda i: (i, 0))
    return pl.pallas_call(
        _s5prep_kernel,
        grid=(S5_TILES,),
        in_specs=[rspec, rspec, rspec, bin_spec, bin_spec, cin_spec, cin_spec],
        out_specs=[rspec, rspec, bspec, bspec,
                   pl.BlockSpec((None, 2 * S5_TILE_ST, S5_TILE_IN), lambda i: (i, 0, 0))],
        out_shape=[jax.ShapeDtypeStruct((1, N_STATE), F32), jax.ShapeDtypeStruct((1, N_STATE), F32),
                   jax.ShapeDtypeStruct((S5_TILE_IN, N_STATE), BF16),
                   jax.ShapeDtypeStruct((S5_TILE_IN, N_STATE), BF16),
                   jax.ShapeDtypeStruct((S5_TILES, 2 * S5_TILE_ST, S5_TILE_IN), BF16)],
        compiler_params=_cparams(1),
        name="s5_prep",
    )(row(lam_re), row(lam_im), row(ldt), bt(b_re), bt(b_im), ct(c_re), ct(c_im))


def _inproj_kernel(x_hbm, shift_ref, scale_ref, gain_ref, w_ref, wt_ref, p_ref, pt_ref, h0_scr, h1_scr, x_buf,
                   x_sem, *, tm, rc, rows_per_seq, seq_base, n_mtiles, n_ntiles):
    mt, n = pl.program_id(0), pl.program_id(1)
    n_chunks = tm // rc
    per_step = -(-n_chunks // max(n_ntiles - 2, 1))
    norm_steps = -(-n_chunks // per_step)
    assert n_mtiles == 1 or 2 + norm_steps <= n_ntiles

    def x_copy(tile):
        return pltpu.make_async_copy(x_hbm.at[pl.ds(pl.multiple_of(tile * tm, tm), tm), :], x_buf, x_sem)

    def norm_rows(tile, chunk, dst):
        r0 = pl.multiple_of(chunk * rc, rc)
        row0 = tile * tm + r0
        xx = x_buf[pl.ds(r0, rc), :]
        y = xx * lax.rsqrt(jnp.mean(xx * xx, axis=-1, keepdims=True) + EPS)
        sc = _seq_rows(scale_ref, row0, rc, rows_per_seq, seq_base)
        sh = _seq_rows(shift_ref, row0, rc, rows_per_seq, seq_base)
        dst[pl.ds(r0, rc), :] = (y * (gain_ref[...] * (1.0 + sc)) + sh).astype(BF16)

    def tile_steps(par, h_cur, h_next):
        @pl.when(n == 0)
        def _():
            if par == 0:
                @pl.when(mt == 0)
                def _():
                    x_copy(0).start()
                    x_copy(0).wait()

                    def body(i, carry):
                        norm_rows(0, i, h_cur)
                        return carry

                    lax.fori_loop(0, n_chunks, body, 0)
            pl.when(mt + 1 < n_mtiles)(lambda: x_copy(mt + 1).start())
            pt_ref[...] = _dot_nt(h_cur[...], wt_ref[...].astype(BF16))

        if n_mtiles > 1:
            @pl.when((n >= 2) & (n < 2 + norm_steps) & (mt + 1 < n_mtiles))
            def _():
                pl.when(n == 2)(lambda: x_copy(mt + 1).wait())
                for k in range(per_step):
                    chunk = (n - 2) * per_step + k
                    if n_chunks % per_step == 0:
                        norm_rows(mt + 1, chunk, h_next)
                    else:
                        pl.when(chunk < n_chunks)(functools.partial(norm_rows, mt + 1, chunk, h_next))

        p_ref[...] = _dot_nt(h_cur[...], w_ref[...].astype(BF16))

    for par, (h_cur, h_next) in enumerate(((h0_scr, h1_scr), (h1_scr, h0_scr))):
        pl.when(lax.rem(mt, 2) == par)(functools.partial(tile_steps, par, h_cur, h_next))


def _inproj_call(x2d, mod, gain, w_in_t, *, rows_per_seq, seq_base, tm, tn):
    tokens = x2d.shape[0]
    ns = mod.shape[0]
    rc = 128 if rows_per_seq >= 128 else 8 * rows_per_seq
    kern = functools.partial(_inproj_kernel, tm=tm, rc=rc, rows_per_seq=rows_per_seq, seq_base=seq_base,
                             n_mtiles=tokens // tm, n_ntiles=PROJ_COLS // tn)
    assert COL_G % tn == 0
    n_before = COL_G // tn

    def w_rows(m, n):
        tiles = jnp.where(n < n_before, n * (tn // SUBLANES), COL_MA // SUBLANES + (n - n_before) * (tn // SUBLANES))
        return (tiles * SUBLANES, 0)

    return pl.pallas_call(
        kern,
        grid=(tokens // tm, PROJ_COLS // tn),
        in_specs=[pl.BlockSpec(memory_space=pl.ANY),
                  pl.BlockSpec((ns, D_MODEL), lambda m, n: (0, 0), pipeline_mode=pl.Buffered(1)),
                  pl.BlockSpec((ns, D_MODEL), lambda m, n: (0, 1), pipeline_mode=pl.Buffered(1)),
                  _const_spec((1, D_MODEL), 2),
                  pl.BlockSpec((pl.Element(tn), pl.Element(D_MODEL)), w_rows),
                  pl.BlockSpec((GATE_RANK, D_MODEL), lambda m, n: (COL_G // GATE_RANK, 0),
                               pipeline_mode=pl.Buffered(1))],
        out_specs=[pl.BlockSpec((tm, tn), lambda m, n: (m, n)),
                   pl.BlockSpec((tm, GATE_RANK), lambda m, n: (m, 0))],
        out_shape=[jax.ShapeDtypeStruct((tokens, PROJ_COLS), F32),
                   jax.ShapeDtypeStruct((tokens, GATE_RANK), F32)],
        scratch_shapes=[pltpu.VMEM((tm, D_MODEL), BF16), pltpu.VMEM((tm, D_MODEL), BF16),
                        pltpu.VMEM((tm, D_MODEL), F32), pltpu.SemaphoreType.DMA(())],
        compiler_params=_cparams(2, V7X_VMEM_BYTES - 2 * 1024 * 1024),
        name="in_proj",
    )(x2d, mod, mod, gain, w_in_t, w_in_t)


def _s5_kernel(*refs, tc, n_sub, ls, has_h0):
    n_in = 12 if has_h0 else 10
    if has_h0:
        (u_ref, za_ref, h0re_ref, h0im_ref, wbre_ref, wbim_ref, wc_ref,
         are_ref, aim_ref, dskip_ref, wglu_ref, bglu_ref) = refs[:n_in]
    else:
        (u_ref, za_ref, wbre_ref, wbim_ref, wc_ref,
         are_ref, aim_ref, dskip_ref, wglu_ref, bglu_ref) = refs[:n_in]
    gated_ref, fre_ref, fim_ref = refs[n_in:n_in + 3]
    scr = refs[n_in + 3:n_in + 3 + 2 * n_sub * S5_TILES]
    xr = [scr[2 * c * S5_TILES:(2 * c + 1) * S5_TILES] for c in range(n_sub)]
    xi = [scr[(2 * c + 1) * S5_TILES:(2 * c + 2) * S5_TILES] for c in range(n_sub)]
    hc_re, hc_im = refs[n_in + 3 + 2 * n_sub * S5_TILES:]
    ct = pl.program_id(1)
    n_seq = tc // ls

    us = []
    for c in range(n_sub):
        u = u_ref[c * tc:(c + 1) * tc, :]
        us.append(u)
        ub = u.astype(BF16)
        for i in range(S5_TILES):
            ui = ub[:, i * S5_TILE_IN:(i + 1) * S5_TILE_IN]
            for w_ref, x_scr in ((wbre_ref, xr[c][i]), (wbim_ref, xi[c][i])):
                res = _dot(ui, w_ref[:, i * S5_TILE_ST:(i + 1) * S5_TILE_ST])
                for j in range(SUBLANES):
                    x_scr[pl.ds(j, tc, stride=SUBLANES), :] = res[:, j * LANES:(j + 1) * LANES]

    a_re = [are_ref[i] for i in range(S5_TILES)]
    a_im = [aim_ref[i] for i in range(S5_TILES)]
    if not has_h0:
        h_re = [jnp.where(ct == 0, 0.0, hc_re[i]) for i in range(S5_TILES)]
        h_im = [jnp.where(ct == 0, 0.0, hc_im[i]) for i in range(S5_TILES)]
    for c in range(n_sub):
        for s in range(n_seq):
            if has_h0:
                h_re = [h0re_ref[c * n_seq + s, i] for i in range(S5_TILES)]
                h_im = [h0im_ref[c * n_seq + s, i] for i in range(S5_TILES)]
            for tl in range(ls):
                r0 = (s * ls + tl) * SUBLANES
                for i in range(S5_TILES):
                    nr = a_re[i] * h_re[i] - a_im[i] * h_im[i] + xr[c][i][r0:r0 + SUBLANES, :]
                    ni = a_re[i] * h_im[i] + a_im[i] * h_re[i] + xi[c][i][r0:r0 + SUBLANES, :]
                    xr[c][i][r0:r0 + SUBLANES, :] = nr
                    xi[c][i][r0:r0 + SUBLANES, :] = ni
                    h_re[i], h_im[i] = nr, ni
            if has_h0:
                for i in range(S5_TILES):
                    fre_ref[c * n_seq + s, i] = h_re[i]
                    fim_ref[c * n_seq + s, i] = h_im[i]
    if not has_h0:
        for i in range(S5_TILES):
            hc_re[i] = h_re[i]
            hc_im[i] = h_im[i]
            fre_ref[0, i] = h_re[i]
            fim_ref[0, i] = h_im[i]

    for c in range(n_sub):
        ys = []
        for i in range(S5_TILES):
            parts = [x_scr[pl.ds(j, tc, stride=SUBLANES), :]
                     for x_scr in (xr[c][i], xi[c][i]) for j in range(SUBLANES)]
            ys.append(_dot(jnp.concatenate(parts, axis=1).astype(BF16), wc_ref[i]))
        y = jnp.concatenate(ys, axis=1) + dskip_ref[...] * us[c]
        g = jax.nn.gelu(y)
        a_y = g * jax.nn.sigmoid(_dot(g.astype(BF16), wglu_ref[...]) + bglu_ref[...])
        gated_ref[c * tc:(c + 1) * tc, :] = (a_y * jax.nn.silu(za_ref[c * tc:(c + 1) * tc, :])).astype(BF16)


def _s5_call(proj, prep, d_skip, w_glu, b_glu, h0, *, n_batch, seq_len, tc, n_sub):
    a_re, a_im, wb_re, wb_im, wc = prep
    tokens = proj.shape[0]
    has_h0 = h0 is not None
    tt = tc * n_sub
    if has_h0:
        ls = seq_len
        grid = (tokens // tt, 1)
        tok = lambda b, c: b
        seq_block = tt // ls
    else:
        ls = tc
        nct = seq_len // tt
        grid = (n_batch, nct)
        tok = lambda b, c: b * nct + c
        seq_block = 1
    const = lambda shape: _const_spec(shape, 2)
    st_spec = pl.BlockSpec((seq_block, S5_TILES, SUBLANES, LANES), lambda b, c: (b, 0, 0, 0))
    in_specs = [pl.BlockSpec((tt, W_A), lambda b, c: (tok(b, c), COL_UA // W_A)),
                pl.BlockSpec((tt, W_A), lambda b, c: (tok(b, c), COL_ZA // W_A))]
    args = [proj] * 2
    if has_h0:
        in_specs += [st_spec, st_spec]
        args += [h0[0], h0[1]]
    in_specs += [const((S5_TILE_IN, N_STATE)), const((S5_TILE_IN, N_STATE)),
                 const((S5_TILES, 2 * S5_TILE_ST, S5_TILE_IN)),
                 const((S5_TILES, SUBLANES, LANES)), const((S5_TILES, SUBLANES, LANES)),
                 const((1, W_A)), const((W_A, W_A)), const((1, W_A))]
    args += [wb_re, wb_im, wc, a_re, a_im, d_skip, w_glu, b_glu]
    n_seq_total = tokens // seq_len
    st_shape = jax.ShapeDtypeStruct((n_seq_total, S5_TILES, SUBLANES, LANES), F32)
    return pl.pallas_call(
        functools.partial(_s5_kernel, tc=tc, n_sub=n_sub, ls=ls, has_h0=has_h0),
        grid=grid,
        in_specs=in_specs,
        out_specs=[pl.BlockSpec((tt, W_A), lambda b, c: (tok(b, c), 0)), st_spec, st_spec],
        out_shape=[jax.ShapeDtypeStruct((tokens, W_A), BF16), st_shape, st_shape],
        scratch_shapes=([pltpu.VMEM((tc * SUBLANES, LANES), F32)] * (2 * n_sub * S5_TILES)
                        + [pltpu.VMEM((S5_TILES, SUBLANES, LANES), F32)] * 2),
        compiler_params=_cparams(2),
        name="s5_branch",
    )(*args)


def _gla_levels(t, ls):
    ms = []
    m = ls
    while m >= 1:
        ms.append(m)
        m //= 2
    idx = np.arange(t)
    same = lambda m: (idx[:, None] // m) == (idx[None, :] // m)
    mats = [same(SUBLANES) & (idx[None, :] <= idx[:, None])]
    for m in ms:
        if 1 < m < SUBLANES:
            mats.append(same(m) & (idx[None, :] <= idx[:, None]))
            mats.append(same(m))
    return ms, np.concatenate(mats, axis=0).astype(np.float32)


def _gla_kernel(*refs, t, n_sub, ls, has_s0, ms):
    if has_s0:
        (q_ref, k_ref, v_ref, zb_ref, ma0_ref, ma1_ref, mb0_ref, mb1_ref, glr_ref, ya_ref,
         s0_ref, lv_ref, wg_ref, bg_ref, gg_ref, waout_ref, wbout_ref, merged_ref, sfin_ref, s_scr) = refs
        assert n_sub == 1
    else:
        (q_ref, k_ref, v_ref, zb_ref, ma0_ref, ma1_ref, mb0_ref, mb1_ref, glr_ref, glrn_ref, ya_ref,
         lv_ref, wg_ref, bg_ref, gg_ref, waout_ref, wbout_ref, merged_ref, sfin_ref, s_scr,
         e0_scr, e1_scr, g0_scr, g1_scr) = refs
        assert ls == t
    ct = pl.program_id(1)
    n_seq = t // ls

    ti = lax.broadcasted_iota(jnp.int32, (t, t), 0)
    si = lax.broadcasted_iota(jnp.int32, (t, t), 1)
    masks = {}
    for li in range(1, len(ms)):
        m = ms[li]
        masks[li] = ((_shr(ti, 2 * m) == _shr(si, 2 * m)) & ((_shr(ti, m) & 1) == 1) & ((_shr(si, m) & 1) == 0))
    rid = lax.broadcasted_iota(jnp.int32, (t, 1), 0)

    def gate_sums(g_ref):
        ga = jax.nn.log_sigmoid(_dot(g_ref[...].astype(BF16), wg_ref[...]) + bg_ref[...]) * (LOG2_E / GATE_TAU)
        ga_b = ga.astype(BF16)
        sums = _dot(lv_ref[...], jnp.concatenate([ga_b[c * t:(c + 1) * t] for c in range(n_sub)], axis=1))
        return ga_b.astype(F32), sums

    def chunks(ga_r, e_all):
        if not has_s0:
            states = [jnp.where(ct == 0, 0.0, s_scr[h]) for h in range(N_HEADS_B)]
        gated_rows = []
        for c in range(n_sub):
            rows = slice(c * t, (c + 1) * t)
            cols = slice(c * QK_W, (c + 1) * QK_W)
            cum_tile = e_all[0:t, cols]
            n_tiles = t // SUBLANES
            tile_tot = cum_tile.reshape(n_tiles, SUBLANES, QK_W)[:, SUBLANES - 1, :]
            tile_row = [tile_tot[j:j + 1, :] for j in range(n_tiles)]
            cums, tots = {}, {}
            n_fine = 0
            for li, m in enumerate(ms):
                if m >= SUBLANES:
                    per_blk = m // SUBLANES
                    before, total = [], []
                    for blk in range(t // m):
                        acc = jnp.zeros((1, QK_W), F32)
                        for j in range(blk * per_blk, (blk + 1) * per_blk):
                            before.append(jnp.broadcast_to(acc, (SUBLANES, QK_W)))
                            acc = acc + tile_row[j]
                        total.append(jnp.broadcast_to(acc, (m, QK_W)))
                    cums[li] = cum_tile + jnp.concatenate(before, axis=0)
                    tots[li] = jnp.concatenate(total, axis=0)
                elif m > 1:
                    cums[li] = e_all[(1 + 2 * n_fine) * t:(2 + 2 * n_fine) * t, cols]
                    tots[li] = e_all[(2 + 2 * n_fine) * t:(3 + 2 * n_fine) * t, cols]
                    n_fine += 1
            cum = lambda li: cums[li]
            tot = lambda li: tots[li]

            q, k, v = q_ref[rows, :], k_ref[rows, :], v_ref[rows, :]
            heads = []
            for h in range(N_HEADS_B):
                hs = slice(h * DK_B, (h + 1) * DK_B)
                qs = q[:, hs] * (DK_B ** -0.5)
                kh = k[:, hs]
                vh = v[:, h * DV_B:(h + 1) * DV_B]
                vb = vh.astype(BF16)
                b = cum(0)[:, hs]
                bend = tot(0)[:, hs]
                q_dec = (qs * jnp.exp2(b)).astype(BF16)
                k_dec = kh * jnp.exp2(bend - b)
                scores = jnp.zeros((t, t), F32)
                for li in range(1, len(ms)):
                    if ms[li] == 1:
                        qm = (qs * jnp.exp2(ga_r[rows, hs])).astype(BF16)
                        km = kh.astype(BF16)
                    else:
                        eq = cum(li)[:, hs]
                        qm = (qs * jnp.exp2(eq)).astype(BF16)
                        km = (kh * jnp.exp2(tot(li)[:, hs] - eq)).astype(BF16)
                    scores = jnp.where(masks[li], _dot_nt(qm, km), scores)
                o = _dot(scores.astype(BF16), vb) + jnp.sum(qs * kh, axis=-1, keepdims=True) * vh
                o_inter = []
                for i in range(n_seq):
                    r0 = i * ls
                    s_old = s0_ref[i, h] if has_s0 else states[h]
                    o_inter.append(_dot(q_dec[r0:r0 + ls], s_old.astype(BF16)))
                    kd = k_dec if n_seq == 1 else jnp.where((rid >= r0) & (rid < r0 + ls), k_dec, 0.0)
                    d_end = jnp.exp2(bend[r0:r0 + 1, :])
                    d_col = jnp.broadcast_to(d_end, (DK_B, DK_B)).T
                    s_new = jnp.concatenate([d_col] * (DV_B // DK_B), axis=1) * s_old + _dot_tn(kd.astype(BF16), vb)
                    if has_s0:
                        sfin_ref[i, h] = s_new
                    else:
                        states[h] = s_new
                o = o + (o_inter[0] if n_seq == 1 else jnp.concatenate(o_inter, axis=0))
                heads.append(o * lax.rsqrt(jnp.mean(o * o, axis=-1, keepdims=True) + EPS))
            b_y = jnp.concatenate(heads, axis=1) * gg_ref[...]
            gated_rows.append((b_y * jax.nn.silu(zb_ref[rows, :])).astype(BF16))
        if not has_s0:
            for h in range(N_HEADS_B):
                s_scr[h] = states[h]
                sfin_ref[0, h] = states[h]
        gated = gated_rows[0] if n_sub == 1 else jnp.concatenate(gated_rows, axis=0)
        merged = (_merge_gate(ma0_ref, ma1_ref) * _dot(ya_ref[...], waout_ref[...])
                  + _merge_gate(mb0_ref, mb1_ref) * _dot(gated, wbout_ref[...]))
        merged_ref[...] = merged.astype(BF16)

    if has_s0:
        chunks(*gate_sums(glr_ref))
    else:
        step = pl.program_id(0) * pl.num_programs(1) + ct
        e_scrs, g_scrs = (e0_scr, e1_scr), (g0_scr, g1_scr)

        @pl.when(step == 0)
        def _():
            g_scrs[0][...], e_scrs[0][...] = gate_sums(glr_ref)

        for par in (0, 1):
            @pl.when(lax.rem(step, 2) == par)
            def _(par=par):
                g_scrs[1 - par][...], e_scrs[1 - par][...] = gate_sums(glrn_ref)
                chunks(g_scrs[par][...], e_scrs[par][...])


def _gla_call(proj, proj_g, y_a, w_gate, b_gate, g_gain, w_a_out, w_b_out, s0, *, n_batch, seq_len, t, n_sub):
    tokens = proj.shape[0]
    has_s0 = s0 is not None
    tt = t * n_sub
    if has_s0:
        ls = seq_len
        grid = (tokens // tt, 1)
        tok = lambda b, c: b
        seq_block = tt // ls
    else:
        ls = t
        nct = seq_len // tt
        grid = (n_batch, nct)
        tok = lambda b, c: b * nct + c
        seq_block = 1
    ms, lv = _gla_levels(t, ls)
    lv = jnp.asarray(lv, dtype=BF16)
    const = lambda shape: _const_spec(shape, 2)
    st_spec = pl.BlockSpec((seq_block, N_HEADS_B, DK_B, DV_B), lambda b, c: (b, 0, 0, 0))
    in_specs = [pl.BlockSpec((tt, QK_W), lambda b, c: (tok(b, c), COL_Q // QK_W)),
                pl.BlockSpec((tt, QK_W), lambda b, c: (tok(b, c), COL_K // QK_W)),
                pl.BlockSpec((tt, W_B), lambda b, c: (tok(b, c), COL_V // W_B)),
                pl.BlockSpec((tt, W_B), lambda b, c: (tok(b, c), COL_ZB // W_B)),
                pl.BlockSpec((tt, W_A), lambda b, c: (tok(b, c), PCOL_MA // W_A)),
                pl.BlockSpec((tt, W_A), lambda b, c: (tok(b, c), PCOL_MA // W_A + 1)),
                pl.BlockSpec((tt, W_B), lambda b, c: (tok(b, c), PCOL_MB // W_B)),
                pl.BlockSpec((tt, W_B), lambda b, c: (tok(b, c), PCOL_MB // W_B + 1)),
                pl.BlockSpec((tt, GATE_RANK), lambda b, c: (tok(b, c), 0))]
    args = [proj, proj, proj, proj, proj, proj, proj, proj, proj_g]
    scratch = [pltpu.VMEM((N_HEADS_B, DK_B, DV_B), F32)]
    if not has_s0:
        last = tokens // tt - 1
        in_specs.append(pl.BlockSpec((tt, GATE_RANK), lambda b, c: (jnp.minimum(tok(b, c) + 1, last), 0)))
        args.append(proj_g)
        scratch += [pltpu.VMEM((lv.shape[0], n_sub * QK_W), F32)] * 2 + [pltpu.VMEM((tt, QK_W), F32)] * 2
    in_specs.append(pl.BlockSpec((tt, W_A), lambda b, c: (tok(b, c), 0)))
    args.append(y_a)
    if has_s0:
        in_specs.append(st_spec)
        args.append(s0)
    in_specs += [const(lv.shape), const((GATE_RANK, QK_W)), const((1, QK_W)), const((1, W_B)),
                 const((W_A, D_MODEL)), const((W_B, D_MODEL))]
    args += [lv, w_gate, b_gate, g_gain, w_a_out, w_b_out]
    n_seq_total = tokens // seq_len
    return pl.pallas_call(
        functools.partial(_gla_kernel, t=t, n_sub=n_sub, ls=ls, has_s0=has_s0, ms=tuple(ms)),
        grid=grid,
        in_specs=in_specs,
        out_specs=[pl.BlockSpec((tt, D_MODEL), lambda b, c: (tok(b, c), 0)), st_spec],
        out_shape=[jax.ShapeDtypeStruct((tokens, D_MODEL), BF16),
                   jax.ShapeDtypeStruct((n_seq_total, N_HEADS_B, DK_B, DV_B), F32)],
        scratch_shapes=scratch,
        compiler_params=_cparams(2),
        name="gla_branch",
    )(*args)


def _out_kernel(m_ref, x_ref, gate_ref, w_ref, fg_ref, y_ref, acc0, acc1,
                *, tm, rc, rows_per_seq, seq_base, n_tiles):
    s = pl.program_id(0)

    def matmul(acc):
        acc[...] = _dot(m_ref[...], w_ref[...])

    def epilogue(acc):
        fg = fg_ref[...]
        for i in range(tm // rc):
            r0 = i * rc
            gate = _seq_rows(gate_ref, (s - 1) * tm + r0, rc, rows_per_seq, seq_base)
            yy = x_ref[r0:r0 + rc, :] + gate * acc[r0:r0 + rc, :]
            y_ref[r0:r0 + rc, :] = yy * lax.rsqrt(jnp.mean(yy * yy, axis=-1, keepdims=True) + EPS) * fg

    accs = (acc0, acc1)

    pl.when(s == 0)(lambda: matmul(acc0))
    for par in (0, 1):
        @pl.when((s > 0) & (s < n_tiles) & (lax.rem(s, 2) == par))
        def _(par=par):
            matmul(accs[par])
            epilogue(accs[1 - par])

    pl.when(s == n_tiles)(lambda: epilogue(accs[(n_tiles - 1) % 2]))


def _out_call(merged, x2d, mod, w_out, fgain, *, rows_per_seq, seq_base, tm):
    tokens = x2d.shape[0]
    ns = mod.shape[0]
    n_tiles = tokens // tm
    rc = 128 if rows_per_seq >= 128 else 8 * rows_per_seq
    kern = functools.partial(_out_kernel, tm=tm, rc=rc, rows_per_seq=rows_per_seq, seq_base=seq_base,
                             n_tiles=n_tiles)
    prev = lambda m: (jnp.maximum(m - 1, 0), 0)
    return pl.pallas_call(
        kern,
        grid=(n_tiles + 1,),
        in_specs=[pl.BlockSpec((tm, D_MODEL), lambda m: (jnp.minimum(m, n_tiles - 1), 0)),
                  pl.BlockSpec((tm, D_MODEL), prev),
                  pl.BlockSpec((ns, D_MODEL), lambda m: (0, 2), pipeline_mode=pl.Buffered(1)),
                  _const_spec((D_MODEL, D_MODEL), 1),
                  _const_spec((1, D_MODEL), 1)],
        out_specs=pl.BlockSpec((tm, D_MODEL), prev),
        out_shape=jax.ShapeDtypeStruct((tokens, D_MODEL), F32),
        scratch_shapes=[pltpu.VMEM((tm, D_MODEL), F32), pltpu.VMEM((tm, D_MODEL), F32)],
        compiler_params=_cparams(1),
        name="out_proj",
    )(merged, x2d, mod, w_out, fgain)


def _tile(n, target):
    t = min(n, target)
    assert n % t == 0
    return t


def _plan(seq_len, tokens, long_seq):
    span = seq_len if long_seq else tokens
    s5_tc = _tile(span, 256)
    gla_t = _tile(span, 128)
    return dict(in_tm=_tile(span, 2048), in_tn=512 if long_seq else 1024,
                s5_tc=s5_tc, s5_sub=2 if span % (2 * s5_tc) == 0 else 1,
                gla_t=gla_t, gla_sub=2 if long_seq and seq_len % (2 * gla_t) == 0 else 1,
                out_tm=_tile(span, 512))


def _layer(x, n_seq_before, mod, weights, prep, s0_ssm, s0_gla):
    (gain, w_in_t, d_skip, w_glu, b_glu, w_a_out, w_gate, b_gate, g_gain, w_b_out, w_out, fgain) = weights
    n_batch, seq_len, _ = x.shape
    tokens = n_batch * seq_len
    x2d = x.reshape(tokens, D_MODEL)
    plan = _plan(seq_len, tokens, long_seq=s0_ssm is None)
    proj, proj_g = _inproj_call(x2d, mod, gain, w_in_t, rows_per_seq=seq_len, seq_base=n_seq_before,
                                tm=plan["in_tm"], tn=plan["in_tn"])
    y_a, f_re, f_im = _s5_call(proj, prep, d_skip, w_glu, b_glu, s0_ssm, n_batch=n_batch, seq_len=seq_len,
                               tc=plan["s5_tc"], n_sub=plan["s5_sub"])
    merged, s_fin = _gla_call(proj, proj_g, y_a, w_gate, b_gate, g_gain, w_a_out, w_b_out, s0_gla,
                              n_batch=n_batch, seq_len=seq_len, t=plan["gla_t"], n_sub=plan["gla_sub"])
    y = _out_call(merged, x2d, mod, w_out, fgain, rows_per_seq=seq_len, seq_base=n_seq_before,
                  tm=plan["out_tm"])
    st = lambda f: f.reshape(1, n_batch, N_GROUPS_A, P_STATE)
    return (y.reshape(n_batch, seq_len, D_MODEL), st(f_re), st(f_im),
            s_fin.reshape(1, n_batch, N_HEADS_B, DK_B, DV_B))


def kernel(x_prompt, x_sample, c_prompt, c_sample, state_ssm_re, state_ssm_im, state_gla, w_ada, b_ada, norm_gain, w_in, lambda_re, lambda_im, log_dt, ssm_b_re, ssm_b_im, ssm_c_re, ssm_c_im, d_skip, w_glu, b_glu, w_gate_up, b_gate, gla_norm_gain, w_a_out, w_b_out, w_out, final_norm_gain):
    assert w_ada.shape[0] == 1, "single-layer step"
    assert w_in.shape[2] == IN_COLS
    n_prompt, n_sample = x_prompt.shape[0], x_sample.shape[0]

    mod = _mod_call(jnp.concatenate([c_prompt, c_sample], axis=0), w_ada[0], b_ada)
    a_re, a_im, wb_re, wb_im, wc = _s5prep_call(lambda_re[0], lambda_im[0], log_dt[0], ssm_b_re[0], ssm_b_im[0],
                                                ssm_c_re[0], ssm_c_im[0])
    state_tiles = lambda a: a.reshape(-1, S5_TILES, SUBLANES, LANES)
    prep = (state_tiles(a_re)[0], state_tiles(a_im)[0], wb_re, wb_im, wc)

    w_in_t = jnp.swapaxes(w_in, 1, 2).reshape(IN_COLS, D_MODEL)
    w_gate = w_gate_up[0].astype(BF16)
    weights = (norm_gain, w_in_t, d_skip, w_glu[0].astype(BF16), b_glu, w_a_out[0].astype(BF16),
               w_gate, b_gate, gla_norm_gain, w_b_out[0].astype(BF16), w_out[0].astype(BF16),
               final_norm_gain.reshape(1, D_MODEL))

    y_p, pre, pim, pgla = _layer(x_prompt, 0, mod, weights, prep, None, None)
    y_s, sre, sim, sgla = _layer(x_sample, n_prompt, mod, weights, prep,
                                 (state_tiles(state_ssm_re[0]), state_tiles(state_ssm_im[0])), state_gla[0])
    return (y_p, y_s, pre, pim, pgla, sre, sim, sgla)
```
